```python
import jax, jax.numpy as jnp
from jax import lax
import numpy as np

D_MODEL = 1024
BATCH = 8
SEQ = 2048
DEPTH = 1
DEC_BATCH = 32
DEC_SEQ = 1
PAST_LEN = 16384
PAGE_SIZE = 128

MIX_WIDTH = D_MODEL
ATT_WIDTH = MIX_WIDTH // 2
RWKV_WIDTH = MIX_WIDTH - ATT_WIDTH
HEAD_DIM = 64
H_ATT = ATT_WIDTH // HEAD_DIM
MOBA_BLOCK = 256
MOBA_TOPK = 3
Q_CHUNK = 128
RWKV_HEAD = 64
H_RWKV = RWKV_WIDTH // RWKV_HEAD
W_LORA = 64
A_LORA = 64
G_LORA = 128
RWKV_COLS = 3 * RWKV_WIDTH + W_LORA + A_LORA + G_LORA
PROJ_COLS = 3 * ATT_WIDTH + RWKV_COLS
N_MEM = 256
MEM_HEADS = 4
MEM_HEAD_DIM = D_MODEL // MEM_HEADS
D_FF = 4 * D_MODEL
LN_EPS = 1e-5
GN_EPS = 64e-5
DEEPNORM_ALPHA = (2.0 * DEPTH) ** 0.25
DEEPNORM_BETA = (8.0 * DEPTH) ** -0.25

kernel_name = "moba_rwkv7_hymba_deepnorm_step"


def layer_norm(x, g, b):
    xf = x.astype(jnp.float32)
    mu = xf.mean(-1, keepdims=True)
    var = jnp.square(xf - mu).mean(-1, keepdims=True)
    return ((xf - mu) * lax.rsqrt(var + LN_EPS) * g + b).astype(x.dtype)


def alibi_slopes():
    return jnp.exp2(-8.0 * (jnp.arange(H_ATT, dtype=jnp.float32) + 1.0) / H_ATT)


def moba_seq(q, k_seq, v_seq, q_pos):
    tc = q.shape[0]
    n_blk = k_seq.shape[0] // MOBA_BLOCK
    kb = k_seq.reshape(n_blk, MOBA_BLOCK, H_ATT, HEAD_DIM).transpose(2, 0, 1, 3)
    vb = v_seq.reshape(n_blk, MOBA_BLOCK, H_ATT, HEAD_DIM).transpose(2, 0, 1, 3)
    qf = q.astype(jnp.float32)
    means = kb.astype(jnp.float32).mean(2)
    gate = jnp.einsum('thd,hnd->htn', qf, means)
    q_blk = q_pos // MOBA_BLOCK
    n_cand = max(n_blk, MOBA_TOPK)
    gate = jnp.pad(gate, ((0, 0), (0, 0), (0, n_cand - n_blk)))
    fully_past = jnp.arange(n_cand)[None, None, :] < q_blk[None, :, None]
    gate = jnp.where(fully_past, gate, -jnp.inf)
    _, top = lax.top_k(gate, MOBA_TOPK)
    top = jnp.minimum(top, n_blk - 1)
    own = jnp.broadcast_to(q_blk[None, :, None], (H_ATT, tc, 1))
    sel = jnp.concatenate([top, own], axis=-1)
    hidx = jnp.arange(H_ATT)[:, None, None]
    kg = kb[hidx, sel]
    vg = vb[hidx, sel]
    key_pos = sel[..., None] * MOBA_BLOCK + jnp.arange(MOBA_BLOCK)
    slot = jnp.arange(MOBA_TOPK + 1)
    n_valid = jnp.minimum(q_blk, MOBA_TOPK)
    slot_ok = (slot[None, :] < n_valid[:, None]) | (slot[None, :] == MOBA_TOPK)
    mask = slot_ok[None, :, :, None] & (key_pos <= q_pos[None, :, None, None])
    dist = (q_pos[None, :, None, None] - key_pos).astype(jnp.float32)
    logits = jnp.einsum('thd,htjsd->htjs', qf, kg.astype(jnp.float32)) * (HEAD_DIM ** -0.5)
    logits = logits - alibi_slopes()[:, None, None, None] * dist
    logits = jnp.where(mask, logits, -jnp.inf)
    p = jax.nn.softmax(logits.reshape(H_ATT, tc, -1), axis=-1).reshape(logits.shape)
    return jnp.einsum('htjs,htjsd->thd', p.astype(v_seq.dtype), vg)


def moba_prompt(q, k, v):
    b_sz, t_len = q.shape[:2]
    t_pad = -(-t_len // MOBA_BLOCK) * MOBA_BLOCK
    pad = ((0, 0), (0, t_pad - t_len), (0, 0), (0, 0))
    kp = jnp.pad(k, pad)
    vp = jnp.pad(v, pad)
    qc = Q_CHUNK if t_len % Q_CHUNK == 0 else t_len
    n_c = t_len // qc

    def step(idx):
        b = idx // n_c
        c = idx % n_c
        q_c = lax.dynamic_slice_in_dim(q[b], c * qc, qc, axis=0)
        pos = c * qc + jnp.arange(qc)
        return moba_seq(q_c, kp[b], vp[b], pos)

    out = lax.map(step, jnp.arange(b_sz * n_c))
    return out.reshape(b_sz, t_len, H_ATT, HEAD_DIM)


def moba_sample(q, k_new, v_new, k_pool, v_pool, page_table):
    b_sz, t_new = q.shape[:2]
    past = page_table.shape[1] * PAGE_SIZE
    total = past + t_new
    l_pad = -(-total // MOBA_BLOCK) * MOBA_BLOCK
    pos = past + jnp.arange(t_new)
    zpad = jnp.zeros((l_pad - total, H_ATT, HEAD_DIM), k_new.dtype)

    def step(b):
        pages = page_table[b]
        k_past = k_pool[pages].reshape(past, H_ATT, HEAD_DIM).astype(k_new.dtype)
        v_past = v_pool[pages].reshape(past, H_ATT, HEAD_DIM).astype(v_new.dtype)
        k_seq = jnp.concatenate([k_past, k_new[b], zpad], axis=0)
        v_seq = jnp.concatenate([v_past, v_new[b], zpad], axis=0)
        return moba_seq(q[b], k_seq, v_seq, pos)

    return lax.map(step, jnp.arange(b_sz))


def rwkv7_scan(r, decay, k, v, aa, bb, s0):
    def step(s, inp):
        r_t, w_t, k_t, v_t, a_t, b_t = inp
        sa = jnp.einsum('bhij,bhj->bhi', s, a_t)
        s = s * w_t[:, :, None, :] + sa[..., None] * b_t[:, :, None, :] + v_t[..., None] * k_t[:, :, None, :]
        return s, jnp.einsum('bhij,bhj->bhi', s, r_t)
    xs = tuple(jnp.swapaxes(t, 0, 1) for t in (r, decay, k, v, aa, bb))
    s_fin, y = lax.scan(step, s0, xs)
    return jnp.swapaxes(y, 0, 1), s_fin


def rwkv7_group(m, s0, w0, w_up, a0, a_up, g_up, k_k, k_a, r_k, gn_g, gn_b):
    b_sz, t_len = m.shape[:2]
    f32 = jnp.float32
    m = m.astype(f32)
    splits = [RWKV_WIDTH, 2 * RWKV_WIDTH, 3 * RWKV_WIDTH, 3 * RWKV_WIDTH + W_LORA,
              3 * RWKV_WIDTH + W_LORA + A_LORA]
    r, k, v, xw, xa, xg = jnp.split(m, splits, axis=-1)
    w_log = -jax.nn.softplus(-(w0.astype(f32) + jnp.tanh(xw) @ w_up.astype(f32))) - 0.5
    decay = jnp.exp(-jnp.exp(w_log))
    a = jax.nn.sigmoid(a0.astype(f32) + xa @ a_up.astype(f32))
    g = jax.nn.sigmoid(xg) @ g_up.astype(f32)

    def heads(t):
        return t.reshape(b_sz, t_len, H_RWKV, RWKV_HEAD)

    kk = heads(k * k_k.astype(f32))
    kk = kk / jnp.maximum(jnp.sqrt(jnp.sum(kk * kk, axis=-1, keepdims=True)), 1e-12)
    k = k * (1.0 + (a - 1.0) * k_a.astype(f32))
    rh, kh, vh, ah = heads(r), heads(k), heads(v), heads(a)
    y, s_fin = rwkv7_scan(rh, heads(decay), kh, vh, -kk, kk * ah, s0.astype(f32))
    mu = y.mean(-1, keepdims=True)
    var = jnp.square(y - mu).mean(-1, keepdims=True)
    y = ((y - mu) * lax.rsqrt(var + GN_EPS)).reshape(b_sz, t_len, RWKV_WIDTH)
    y = y * gn_g.astype(f32) + gn_b.astype(f32)
    bonus = jnp.sum(rh * kh * r_k.astype(f32), axis=-1, keepdims=True) * vh
    y = y + bonus.reshape(b_sz, t_len, RWKV_WIDTH)
    return y * g, s_fin


def hybrid_mixer(x, rw_prev_row, s0, attend, w_in, mu_shift, w0, w_up, a0, a_up, g_up,
                 k_k, k_a, r_k, gn_g, gn_b, w_out):
    b_sz, t_len = x.shape[:2]
    proj = x @ w_in
    q, k, v, rw = jnp.split(proj, [ATT_WIDTH, 2 * ATT_WIDTH, 3 * ATT_WIDTH], axis=-1)
    q = q.reshape(b_sz, t_len, H_ATT, HEAD_DIM)
    k = k.reshape(b_sz, t_len, H_ATT, HEAD_DIM)
    v = v.reshape(b_sz, t_len, H_ATT, HEAD_DIM)
    y_att = attend(q, k, v).reshape(b_sz, t_len, ATT_WIDTH)
    rw_prev = jnp.concatenate([rw_prev_row[:, None, :].astype(rw.dtype), rw[:, :-1]], axis=1)
    m = rw + (rw_prev - rw) * mu_shift
    y_rw, s_fin = rwkv7_group(m, s0, w0, w_up, a0, a_up, g_up, k_k, k_a, r_k, gn_g, gn_b)
    out = jnp.concatenate([y_att, y_rw.astype(x.dtype)], axis=-1) @ w_out
    return out, k, v, rw[:, -1], s_fin


def memory_cross(x, mem_k, mem_v, w_q):
    b_sz, t_len = x.shape[:2]
    q = (x @ w_q).reshape(b_sz, t_len, MEM_HEADS, MEM_HEAD_DIM)
    logits = jnp.einsum('bthd,bshd->bhts', q, mem_k).astype(jnp.float32) * (MEM_HEAD_DIM ** -0.5)
    p = jax.nn.softmax(logits, axis=-1).astype(mem_v.dtype)
    return jnp.einsum('bhts,bshd->bthd', p, mem_v).reshape(b_sz, t_len, D_MODEL)


def mem_kv(mem, w_kv):
    b_sz = mem.shape[0]
    kv = mem @ w_kv
    mk, mv = jnp.split(kv, 2, axis=-1)
    return (mk.reshape(b_sz, N_MEM, MEM_HEADS, MEM_HEAD_DIM),
            mv.reshape(b_sz, N_MEM, MEM_HEADS, MEM_HEAD_DIM))


def sqrelu_mlp(x, w1, w2):
    return jnp.square(jax.nn.relu(x @ w1)) @ w2


def setup_inputs(seed: int = 0) -> dict:
    key = jax.random.key(seed)
    ks = jax.random.split(key, 40)
    f32 = jnp.float32
    beta = DEEPNORM_BETA

    def nrm(k, shape, s):
        return jax.random.normal(k, shape, f32) * s

    n_pages = PAST_LEN // PAGE_SIZE
    n_used = DEC_BATCH * n_pages
    n_pool = n_used + (n_used + 3) // 4
    page_table = jax.random.permutation(ks[0], n_pool)[:n_used].reshape(DEC_BATCH, n_pages).astype(jnp.int32)

    col_scale = jnp.concatenate([
        jnp.ones((2 * ATT_WIDTH,), f32), jnp.full((ATT_WIDTH,), beta, f32),
        jnp.ones((2 * RWKV_WIDTH,), f32), jnp.full((RWKV_WIDTH,), beta, f32),
        jnp.ones((W_LORA + A_LORA + G_LORA,), f32)])
    kv_scale = jnp.concatenate([jnp.ones((D_MODEL,), f32), jnp.full((D_MODEL,), beta, f32)])

    return {
        "x_prompt": nrm(ks[1], (BATCH, SEQ, D_MODEL), 1.0),
        "x_sample": nrm(ks[2], (DEC_BATCH, DEC_SEQ, D_MODEL), 1.0),
        "mem_prompt": nrm(ks[3], (BATCH, N_MEM, D_MODEL), 1.0),
        "cache_k_pool": nrm(ks[4], (DEPTH, n_pool, PAGE_SIZE, H_ATT, HEAD_DIM), 1.0),
        "cache_v_pool": nrm(ks[5], (DEPTH, n_pool, PAGE_SIZE, H_ATT, HEAD_DIM), beta),
        "page_table": page_table,
        "cache_mem_k": nrm(ks[6], (DEPTH, DEC_BATCH, N_MEM, MEM_HEADS, MEM_HEAD_DIM), 1.0),
        "cache_mem_v": nrm(ks[7], (DEPTH, DEC_BATCH, N_MEM, MEM_HEADS, MEM_HEAD_DIM), beta),
        "state_wkv": nrm(ks[8], (DEPTH, DEC_BATCH, H_RWKV, RWKV_HEAD, RWKV_HEAD), 0.3),
        "state_shift": nrm(ks[9], (DEPTH, DEC_BATCH, RWKV_COLS), 1.0),
        "w_in": nrm(ks[10], (DEPTH, D_MODEL, PROJ_COLS), D_MODEL ** -0.5) * col_scale,
        "mu_shift": jax.random.uniform(ks[11], (DEPTH, RWKV_COLS), f32),
        "w0": jax.random.uniform(ks[12], (DEPTH, RWKV_WIDTH), f32, -6.0, -1.0),
        "w_up": nrm(ks[13], (DEPTH, W_LORA, RWKV_WIDTH), 0.1),
        "a0": nrm(ks[14], (DEPTH, RWKV_WIDTH), 0.5),
        "a_up": nrm(ks[15], (DEPTH, A_LORA, RWKV_WIDTH), 0.5 * A_LORA ** -0.5),
        "g_up": nrm(ks[16], (DEPTH, G_LORA, RWKV_WIDTH), G_LORA ** -0.5),
        "k_k": 0.85 + nrm(ks[17], (DEPTH, RWKV_WIDTH), 0.02),
        "k_a": 1.0 + nrm(ks[18], (DEPTH, RWKV_WIDTH), 0.02),
        "r_k": nrm(ks[19], (DEPTH, H_RWKV, RWKV_HEAD), 0.1),
        "gn_g": 1.0 + nrm(ks[20], (DEPTH, RWKV_WIDTH), 0.01),
        "gn_b": nrm(ks[21], (DEPTH, RWKV_WIDTH), 0.01),
        "w_out": nrm(ks[22], (DEPTH, MIX_WIDTH, D_MODEL), beta * MIX_WIDTH ** -0.5),
        "ln1_g": 1.0 + nrm(ks[23], (DEPTH, D_MODEL), 0.01),
        "ln1_b": nrm(ks[24], (DEPTH, D_MODEL), 0.01),
        "w_q_mem": nrm(ks[25], (DEPTH, D_MODEL, D_MODEL), D_MODEL ** -0.5),
        "w_kv_mem": nrm(ks[26], (DEPTH, D_MODEL, 2 * D_MODEL), D_MODEL ** -0.5) * kv_scale,
        "w_o_mem": nrm(ks[27], (DEPTH, D_MODEL, D_MODEL), beta * D_MODEL ** -0.5),
        "ln2_g": 1.0 + nrm(ks[28], (DEPTH, D_MODEL), 0.01),
        "ln2_b": nrm(ks[29], (DEPTH, D_MODEL), 0.01),
        "w_ff1": nrm(ks[30], (DEPTH, D_MODEL, D_FF), D_MODEL ** -0.5),
        "w_ff2": nrm(ks[31], (DEPTH, D_FF, D_MODEL), beta * D_FF ** -0.5),
        "ln3_g": 1.0 + nrm(ks[32], (DEPTH, D_MODEL), 0.01),
        "ln3_b": nrm(ks[33], (DEPTH, D_MODEL), 0.01),
    }


def reference(x_prompt, x_sample, mem_prompt, cache_k_pool, cache_v_pool, page_table,
              cache_mem_k, cache_mem_v, state_wkv, state_shift,
              w_in, mu_shift, w0, w_up, a0, a_up, g_up, k_k, k_a, r_k, gn_g, gn_b, w_out,
              ln1_g, ln1_b, w_q_mem, w_kv_mem, w_o_mem, ln2_g, ln2_b,
              w_ff1, w_ff2, ln3_g, ln3_b):
    alpha = DEEPNORM_ALPHA
    xp, xs = x_prompt, x_sample
    kp_l, vp_l, ks_l, vs_l, mkp_l, mvp_l = [], [], [], [], [], []
    wkvp_l, shp_l, wkvs_l, shs_l = [], [], [], []
    b_p = x_prompt.shape[0]
    for l in range(DEPTH):
        zero_row = jnp.zeros((b_p, RWKV_COLS), xp.dtype)
        zero_s = jnp.zeros((b_p, H_RWKV, RWKV_HEAD, RWKV_HEAD), jnp.float32)
        mix_p, k_p, v_p, last_p, s_p = hybrid_mixer(
            xp, zero_row, zero_s, moba_prompt, w_in[l], mu_shift[l], w0[l], w_up[l], a0[l],
            a_up[l], g_up[l], k_k[l], k_a[l], r_k[l], gn_g[l], gn_b[l], w_out[l])
        xp = layer_norm(alpha * xp + mix_p, ln1_g[l], ln1_b[l])
        mk_p, mv_p = mem_kv(mem_prompt, w_kv_mem[l])
        xp = layer_norm(alpha * xp + memory_cross(xp, mk_p, mv_p, w_q_mem[l]) @ w_o_mem[l], ln2_g[l], ln2_b[l])
        xp = layer_norm(alpha * xp + sqrelu_mlp(xp, w_ff1[l], w_ff2[l]), ln3_g[l], ln3_b[l])

        k_pool_l, v_pool_l = cache_k_pool[l], cache_v_pool[l]

        def attend_sample(q, k, v, k_pool_l=k_pool_l, v_pool_l=v_pool_l):
            return moba_sample(q, k, v, k_pool_l, v_pool_l, page_table)

        mix_s, k_s, v_s, last_s, s_s = hybrid_mixer(
            xs, state_shift[l], state_wkv[l], attend_sample, w_in[l], mu_shift[l], w0[l], w_up[l],
            a0[l], a_up[l], g_up[l], k_k[l], k_a[l], r_k[l], gn_g[l], gn_b[l], w_out[l])
        xs = layer_norm(alpha * xs + mix_s, ln1_g[l], ln1_b[l])
        xs = layer_norm(alpha * xs + memory_cross(xs, cache_mem_k[l], cache_mem_v[l], w_q_mem[l]) @ w_o_mem[l],
                        ln2_g[l], ln2_b[l])
        xs = layer_norm(alpha * xs + sqrelu_mlp(xs, w_ff1[l], w_ff2[l]), ln3_g[l], ln3_b[l])

        kp_l.append(k_p); vp_l.append(v_p); ks_l.append(k_s); vs_l.append(v_s)
        mkp_l.append(mk_p); mvp_l.append(mv_p)
        wkvp_l.append(s_p.astype(state_wkv.dtype)); shp_l.append(last_p)
        wkvs_l.append(s_s.astype(state_wkv.dtype)); shs_l.append(last_s)

    y_prompt, y_sample = xp, xs
    k_prompt, v_prompt = jnp.stack(kp_l, 0), jnp.stack(vp_l, 0)
    k_sample, v_sample = jnp.stack(ks_l, 0), jnp.stack(vs_l, 0)
    mem_k_prompt, mem_v_prompt = jnp.stack(mkp_l, 0), jnp.stack(mvp_l, 0)
    wkv_prompt, shift_prompt = jnp.stack(wkvp_l, 0), jnp.stack(shp_l, 0)
    wkv_sample, shift_sample = jnp.stack(wkvs_l, 0), jnp.stack(shs_l, 0)
    return (y_prompt, y_sample, k_prompt, v_prompt, k_sample, v_sample, mem_k_prompt, mem_v_prompt,
            wkv_prompt, shift_prompt, wkv_sample, shift_sample)
```

```python
import functools

import jax
import jax.numpy as jnp
from jax import lax
from jax.experimental import pallas as pl
from jax.experimental.pallas import tpu as pltpu

F32 = jnp.float32
BF16 = jnp.bfloat16
HIGHEST = lax.Precision.HIGHEST

D_MODEL = 1024
PAGE_SIZE = 128
ATT_WIDTH = 512
RWKV_WIDTH = 512
HEAD_DIM = 64
H_ATT = 8
MOBA_BLOCK = 256
MOBA_TOPK = 3
Q_CHUNK = 128
RWKV_HEAD = 64
H_RWKV = 8
W_LORA = 64
A_LORA = 64
G_LORA = 128
RWKV_COLS = 3 * RWKV_WIDTH + W_LORA + A_LORA + G_LORA
PROJ_COLS = 3 * ATT_WIDTH + RWKV_COLS
N_MEM = 256
MEM_HEADS = 4
MEM_HEAD_DIM = 256
D_FF = 4 * D_MODEL
LN_EPS = 1e-5
GN_EPS = 64e-5
DEPTH = 1
ALPHA = (2.0 * DEPTH) ** 0.25

RWKV_CHUNK = 64
NEG = -1e30
VMEM_LIMIT = 56 * 1024 * 1024

_NT = (((1,), (1,)), ((), ()))
_TN = (((0,), (0,)), ((), ()))


def _cparams(*sem):
    return pltpu.CompilerParams(dimension_semantics=sem, vmem_limit_bytes=VMEM_LIMIT)


def _dot(a, b, dims=None, precision=None):
    if dims is None:
        return jnp.dot(a, b, preferred_element_type=F32, precision=precision)
    return lax.dot_general(a, b, dims, preferred_element_type=F32, precision=precision)


def _bdot(a, b, dims=None):
    return _dot(a.astype(BF16), b.astype(BF16), dims)


def _layer_norm(z, g, b):
    mu = jnp.mean(z, axis=-1, keepdims=True)
    d = z - mu
    var = jnp.mean(d * d, axis=-1, keepdims=True)
    return d * lax.rsqrt(var + LN_EPS) * g + b


def _full(shape):
    n = len(shape)
    return pl.BlockSpec(shape, lambda *_: (0,) * n)


def _col_chunks(n, width=512):
    out, c = [], 0
    while c < n:
        w = min(width, n - c)
        out.append((c, w))
        c += w
    return out


def _mm_multi_body(x_ref, w_ref, *o_refs, col0s):
    xb = x_ref[...].astype(BF16)
    for o_ref, c0 in zip(o_refs, col0s):
        for c, w in _col_chunks(o_ref.shape[1]):
            o_ref[:, c:c + w] = _dot(xb, w_ref[:, c0 + c:c0 + c + w])


def _mm_multi(x, w_bf16, widths, tm):
    m, k = x.shape
    col0s, c = [], 0
    for wd in widths:
        col0s.append(c)
        c += wd
    assert c == w_bf16.shape[1] and m % tm == 0
    return pl.pallas_call(
        functools.partial(_mm_multi_body, col0s=tuple(col0s)),
        grid=(m // tm,),
        in_specs=[pl.BlockSpec((tm, k), lambda i: (i, 0)), _full(w_bf16.shape)],
        out_specs=[pl.BlockSpec((tm, wd), lambda i: (i, 0)) for wd in widths],
        out_shape=[jax.ShapeDtypeStruct((m, wd), F32) for wd in widths],
        compiler_params=_cparams("parallel"),
    )(x, w_bf16)


def _proj_prompt_body(x_ref, w_ref, qt_ref, kt_ref, vt_ref, kb_ref, rw_ref, kmean_ref):
    j = pl.program_id(1)
    tm = x_ref.shape[1]
    xb = x_ref[0].astype(BF16)
    aw = ATT_WIDTH
    q = _dot(xb, w_ref[:, 0:aw])
    k = _dot(xb, w_ref[:, aw:2 * aw])
    v = _dot(xb, w_ref[:, 2 * aw:3 * aw])
    qt_ref[0] = q.T
    kt_ref[0] = k.T
    vt_ref[0] = v.T
    kb_ref[0] = k.astype(BF16)
    for c, w in _col_chunks(RWKV_COLS):
        rw_ref[0, :, c:c + w] = _dot(xb, w_ref[:, 3 * aw + c:3 * aw + c + w])
    blk_per_tile = tm // MOBA_BLOCK
    for i in range(blk_per_tile):
        kmean_ref[0, pl.ds(j * blk_per_tile + i, 1), :] = jnp.mean(
            k[i * MOBA_BLOCK:(i + 1) * MOBA_BLOCK], axis=0, keepdims=True)


def _proj_prompt(x, w_bf16, tm):
    bsz, t, _ = x.shape
    assert t % tm == 0 and tm % MOBA_BLOCK == 0
    n_blk = t // MOBA_BLOCK
    tr_spec = pl.BlockSpec((1, ATT_WIDTH, tm), lambda i, j: (i, 0, j))
    tr_shape = jax.ShapeDtypeStruct((bsz, ATT_WIDTH, t), F32)
    return pl.pallas_call(
        _proj_prompt_body,
        grid=(bsz, t // tm),
        in_specs=[pl.BlockSpec((1, tm, D_MODEL), lambda i, j: (i, j, 0)), _full(w_bf16.shape)],
        out_specs=[tr_spec, tr_spec, tr_spec,
                   pl.BlockSpec((1, tm, ATT_WIDTH), lambda i, j: (i, j, 0)),
                   pl.BlockSpec((1, tm, RWKV_COLS), lambda i, j: (i, j, 0)),
                   pl.BlockSpec((1, n_blk, ATT_WIDTH), lambda i, j: (i, 0, 0))],
        out_shape=[tr_shape, tr_shape, tr_shape,
                   jax.ShapeDtypeStruct((bsz, t, ATT_WIDTH), BF16),
                   jax.ShapeDtypeStruct((bsz, t, RWKV_COLS), F32),
                   jax.ShapeDtypeStruct((bsz, n_blk, ATT_WIDTH), F32)],
        compiler_params=_cparams("parallel", "arbitrary"),
    )(x, w_bf16)


def _mm_res_ln_body(*refs, n_in):
    x_refs = refs[:n_in]
    w_refs = refs[n_in:2 * n_in]
    res_ref, g_ref, b_ref, o_ref = refs[2 * n_in:]
    acc = _dot(x_refs[0][...].astype(BF16), w_refs[0][...])
    for x_ref, w_ref in zip(x_refs[1:], w_refs[1:]):
        acc = acc + _dot(x_ref[...].astype(BF16), w_ref[...])
    o_ref[...] = _layer_norm(ALPHA * res_ref[...] + acc, g_ref[...], b_ref[...])


def _mm_res_ln(xs, ws, res, g, b, tm):
    m = res.shape[0]
    n_in = len(xs)
    assert m % tm == 0
    in_specs = [pl.BlockSpec((tm, x.shape[1]), lambda i: (i, 0)) for x in xs]
    in_specs += [_full(w.shape) for w in ws]
    in_specs += [pl.BlockSpec((tm, D_MODEL), lambda i: (i, 0)), _full((1, D_MODEL)), _full((1, D_MODEL))]
    return pl.pallas_call(
        functools.partial(_mm_res_ln_body, n_in=n_in),
        grid=(m // tm,),
        in_specs=in_specs,
        out_specs=pl.BlockSpec((tm, D_MODEL), lambda i: (i, 0)),
        out_shape=jax.ShapeDtypeStruct((m, D_MODEL), F32),
        compiler_params=_cparams("parallel"),
    )(*xs, *ws, res, g, b)


def _mlp_body(x_ref, w1_ref, w2_ref, g_ref, b_ref, o_ref):
    x = x_ref[...]
    xb = x.astype(BF16)
    acc = jnp.zeros(x.shape, F32)
    for c, w in _col_chunks(D_FF):
        h = jnp.maximum(_dot(xb, w1_ref[:, c:c + w]), 0.0)
        acc = acc + _dot((h * h).astype(BF16), w2_ref[c:c + w, :])
    o_ref[...] = _layer_norm(ALPHA * x + acc, g_ref[...], b_ref[...])


def _mlp(x, w1, w2, g, b, tm):
    m = x.shape[0]
    assert m % tm == 0
    return pl.pallas_call(
        _mlp_body,
        grid=(m // tm,),
        in_specs=[pl.BlockSpec((tm, D_MODEL), lambda i: (i, 0)), _full(w1.shape), _full(w2.shape),
                  _full((1, D_MODEL)), _full((1, D_MODEL))],
        out_specs=pl.BlockSpec((tm, D_MODEL), lambda i: (i, 0)),
        out_shape=jax.ShapeDtypeStruct((m, D_MODEL), F32),
        compiler_params=_cparams("parallel"),
    )(x, w1, w2, g, b)


def _mem_attend(q, k_head, v_head):
    outs = []
    for h in range(MEM_HEADS):
        sl = slice(h * MEM_HEAD_DIM, (h + 1) * MEM_HEAD_DIM)
        s = _bdot(q[:, sl], k_head(h), _NT)
        p = jnp.exp(s - jnp.max(s, axis=-1, keepdims=True))
        l = jnp.sum(p, axis=-1, keepdims=True)
        outs.append(_bdot(p, v_head(h)) / l)
    return jnp.concatenate(outs, axis=-1)


def _memx_prompt_body(x_ref, mk_ref, mv_ref, wq_ref, wo_ref, g_ref, b_ref, o_ref):
    x = x_ref[0]
    q = _dot(x.astype(BF16), wq_ref[...]) * (MEM_HEAD_DIM ** -0.5)
    cols = lambda h: slice(h * MEM_HEAD_DIM, (h + 1) * MEM_HEAD_DIM)
    att = _mem_attend(q, lambda h: mk_ref[0, :, cols(h)], lambda h: mv_ref[0, :, cols(h)])
    acc = _dot(att.astype(BF16), wo_ref[...])
    o_ref[0] = _layer_norm(ALPHA * x + acc, g_ref[...], b_ref[...])


def _memx_prompt(x, mk, mv, wq, wo, g, b, tm):
    bsz, t, _ = x.shape
    assert t % tm == 0
    return pl.pallas_call(
        _memx_prompt_body,
        grid=(bsz, t // tm),
        in_specs=[pl.BlockSpec((1, tm, D_MODEL), lambda i, j: (i, j, 0)),
                  pl.BlockSpec((1, N_MEM, D_MODEL), lambda i, j: (i, 0, 0)),
                  pl.BlockSpec((1, N_MEM, D_MODEL), lambda i, j: (i, 0, 0)),
                  _full(wq.shape), _full(wo.shape), _full((1, D_MODEL)), _full((1, D_MODEL))],
        out_specs=pl.BlockSpec((1, tm, D_MODEL), lambda i, j: (i, j, 0)),
        out_shape=jax.ShapeDtypeStruct(x.shape, F32),
        compiler_params=_cparams("parallel", "parallel"),
    )(x, mk, mv, wq, wo, g, b)


def _memx_sample_body(q_ref, mk_ref, mv_ref, o_ref):
    q = jnp.broadcast_to(q_ref[0] * (MEM_HEAD_DIM ** -0.5), (8, D_MODEL))
    o_ref[0] = _mem_attend(q, lambda h: mk_ref[0, :, h, :], lambda h: mv_ref[0, :, h, :])[0:1]


def _memx_sample(q, mk, mv):
    bd = q.shape[0]
    mem_spec = pl.BlockSpec((1, N_MEM, MEM_HEADS, MEM_HEAD_DIM), lambda i: (i, 0, 0, 0))
    return pl.pallas_call(
        _memx_sample_body,
        grid=(bd,),
        in_specs=[pl.BlockSpec((1, 1, D_MODEL), lambda i: (i, 0, 0)), mem_spec, mem_spec],
        out_specs=pl.BlockSpec((1, 1, D_MODEL), lambda i: (i, 0, 0)),
        out_shape=jax.ShapeDtypeStruct((bd, 1, D_MODEL), F32),
        compiler_params=_cparams("parallel"),
    )(q, mk, mv)


def _top3_rows(g, n_iota, n_lim):
    g = jnp.where(n_iota < n_lim, g, -jnp.inf)
    n_rows = g.shape[0]
    sel = n_iota < 0
    for _ in range(MOBA_TOPK):
        mx = jnp.max(g, axis=0, keepdims=True)
        idx = jnp.min(jnp.where(g == mx, n_iota, n_rows), axis=0, keepdims=True)
        hit = n_iota == idx
        sel = sel | hit
        g = jnp.where(hit, -jnp.inf, g)
    return sel & (n_iota < n_lim)


def _moba_prompt_body(qt_ref, kb_ref, vt_ref, kmean_ref, o_ref, vt_s, sel_s, m_s, l_s, acc_s, *, n_blk):
    c = pl.program_id(1)

    @pl.when(c == 0)
    def _():
        for n in range(n_blk):
            vt_s[n] = vt_ref[0, :, n * MOBA_BLOCK:(n + 1) * MOBA_BLOCK].astype(BF16)

    q_blk = (c * Q_CHUNK) // MOBA_BLOCK
    qt = qt_ref[0]
    n_iota = lax.broadcasted_iota(jnp.int32, (n_blk, Q_CHUNK), 0)
    for h in range(H_ATT):
        hs = slice(h * HEAD_DIM, (h + 1) * HEAD_DIM)
        gate = _dot(kmean_ref[0, :, hs], qt[hs, :], precision=HIGHEST)
        sel = _top3_rows(gate, n_iota, q_blk) | (n_iota == q_blk)
        sel_s[h * n_blk:(h + 1) * n_blk, :] = sel.astype(F32)
    m_s[...] = jnp.full(m_s.shape, NEG, F32)
    l_s[...] = jnp.zeros(l_s.shape, F32)
    acc_s[...] = jnp.zeros(acc_s.shape, F32)

    qtb = (qt * (HEAD_DIM ** -0.5)).astype(BF16)
    rel = (lax.broadcasted_iota(jnp.int32, (MOBA_BLOCK, Q_CHUNK), 1)
           - lax.broadcasted_iota(jnp.int32, (MOBA_BLOCK, Q_CHUNK), 0))

    def blk_body(n, carry):
        dist = (rel + (c * Q_CHUNK - n * MOBA_BLOCK)).astype(F32)
        causal = dist >= 0.0
        kblk = kb_ref[0, pl.ds(pl.multiple_of(n * MOBA_BLOCK, MOBA_BLOCK), MOBA_BLOCK), :]
        vblk = vt_s[n]
        for h in range(H_ATT):
            hs = slice(h * HEAD_DIM, (h + 1) * HEAD_DIM)
            s = _dot(kblk[:, hs], qtb[hs, :]) - (2.0 ** -(h + 1)) * dist
            valid = causal & (sel_s[pl.ds(h * n_blk + n, 1), :] > 0.0)
            s = jnp.where(valid, s, NEG)
            m_old = m_s[h:h + 1, :]
            m_new = jnp.maximum(m_old, jnp.max(s, axis=0, keepdims=True))
            p = jnp.where(valid, jnp.exp(s - m_new), 0.0)
            corr = jnp.exp(m_old - m_new)
            l_s[h:h + 1, :] = l_s[h:h + 1, :] * corr + jnp.sum(p, axis=0, keepdims=True)
            acc_s[hs, :] = acc_s[hs, :] * corr + _dot(vblk[hs, :], p.astype(BF16))
            m_s[h:h + 1, :] = m_new
        return carry

    lax.fori_loop(0, q_blk + 1, blk_body, 0)
    for h in range(H_ATT):
        hs = slice(h * HEAD_DIM, (h + 1) * HEAD_DIM)
        acc_s[hs, :] = acc_s[hs, :] / l_s[h:h + 1, :]
    o_ref[0] = acc_s[...].T


def _moba_prompt(qt, kb, vt, kmean):
    bsz, t, _ = kb.shape
    assert t % MOBA_BLOCK == 0 and MOBA_BLOCK % Q_CHUNK == 0
    n_blk = t // MOBA_BLOCK
    return pl.pallas_call(
        functools.partial(_moba_prompt_body, n_blk=n_blk),
        grid=(bsz, t // Q_CHUNK),
        in_specs=[pl.BlockSpec((1, ATT_WIDTH, Q_CHUNK), lambda i, j: (i, 0, j)),
                  pl.BlockSpec((1, t, ATT_WIDTH), lambda i, j: (i, 0, 0)),
                  pl.BlockSpec((1, ATT_WIDTH, t), lambda i, j: (i, 0, 0)),
                  pl.BlockSpec((1, n_blk, ATT_WIDTH), lambda i, j: (i, 0, 0))],
        out_specs=pl.BlockSpec((1, Q_CHUNK, ATT_WIDTH), lambda i, j: (i, j, 0)),
        out_shape=jax.ShapeDtypeStruct((bsz, t, ATT_WIDTH), F32),
        scratch_shapes=[pltpu.VMEM((n_blk, ATT_WIDTH, MOBA_BLOCK), BF16),
                        pltpu.VMEM((H_ATT * n_blk, Q_CHUNK), F32),
                        pltpu.VMEM((H_ATT, Q_CHUNK), F32),
                        pltpu.VMEM((H_ATT, Q_CHUNK), F32),
                        pltpu.VMEM((ATT_WIDTH, Q_CHUNK), F32)],
        compiler_params=_cparams("parallel", "arbitrary"),
    )(qt, kb, vt, kmean)


PAGES_PER_STEP = 16
PAGES_PER_BLOCK = MOBA_BLOCK // PAGE_SIZE
N_SEL_PAGES = MOBA_TOPK * PAGES_PER_BLOCK


def _moba_scores_body(pt_ref, q_ref, *refs, n_pages):
    page_refs = refs[:PAGES_PER_STEP]
    s_ref, top_ref, qcol_s, gate_s = refs[PAGES_PER_STEP:]
    g = pl.program_id(1)
    n_blk = n_pages // PAGES_PER_BLOCK
    blk_per_step = PAGES_PER_STEP // PAGES_PER_BLOCK

    @pl.when(g == 0)
    def _():
        for h in range(H_ATT):
            qcol_s[h] = jnp.broadcast_to(q_ref[0, h:h + 1, :], (PAGE_SIZE, HEAD_DIM)).T
        gate_s[...] = jnp.zeros(gate_s.shape, F32)

    lane = lax.broadcasted_iota(jnp.int32, (H_ATT, 128), 1)
    for i in range(blk_per_step):
        blk_sum = jnp.zeros((H_ATT, PAGE_SIZE), F32)
        for jj in range(PAGES_PER_BLOCK):
            p = i * PAGES_PER_BLOCK + jj
            s = jnp.concatenate(
                [jnp.sum(page_refs[p][0, h] * qcol_s[h], axis=0, keepdims=True) for h in range(H_ATT)], axis=0)
            s_ref[0, p] = s
            blk_sum = blk_sum + s
        gate = jnp.sum(blk_sum, axis=1, keepdims=True) * (1.0 / MOBA_BLOCK)
        gate_s[...] = jnp.where(lane == g * blk_per_step + i, gate, gate_s[...])

    @pl.when(g == pl.num_programs(1) - 1)
    def _():
        g_work = jnp.where(lane < n_blk, gate_s[...], -jnp.inf)
        out = jnp.zeros((H_ATT, 128), jnp.int32)
        for r in range(MOBA_TOPK):
            mx = jnp.max(g_work, axis=1, keepdims=True)
            idx = jnp.min(jnp.where(g_work == mx, lane, 128), axis=1, keepdims=True)
            out = jnp.where(lane == r, idx, out)
            g_work = jnp.where(lane == idx, -jnp.inf, g_work)
        top_ref[0] = out


def _moba_scores(q, kt_pool, page_table):
    bd, n_pages = page_table.shape
    assert n_pages % PAGES_PER_STEP == 0 and PAGES_PER_STEP % PAGES_PER_BLOCK == 0
    n_blk = n_pages // PAGES_PER_BLOCK
    assert MOBA_TOPK <= n_blk <= 128

    def page_spec(i):
        return pl.BlockSpec((1, H_ATT, HEAD_DIM, PAGE_SIZE),
                            lambda b, g, pt: (pt[b * n_pages + g * PAGES_PER_STEP + i], 0, 0, 0))

    grid_spec = pltpu.PrefetchScalarGridSpec(
        num_scalar_prefetch=1,
        grid=(bd, n_pages // PAGES_PER_STEP),
        in_specs=[pl.BlockSpec((1, H_ATT, HEAD_DIM), lambda b, g, pt: (b, 0, 0))]
        + [page_spec(i) for i in range(PAGES_PER_STEP)],
        out_specs=[pl.BlockSpec((1, PAGES_PER_STEP, H_ATT, PAGE_SIZE), lambda b, g, pt: (b, g, 0, 0)),
                   pl.BlockSpec((1, H_ATT, 128), lambda b, g, pt: (b, 0, 0))],
        scratch_shapes=[pltpu.VMEM((H_ATT, HEAD_DIM, PAGE_SIZE), F32), pltpu.VMEM((H_ATT, 128), F32)],
    )
    return pl.pallas_call(
        functools.partial(_moba_scores_body, n_pages=n_pages),
        grid_spec=grid_spec,
        out_shape=[jax.ShapeDtypeStruct((bd, n_pages, H_ATT, PAGE_SIZE), F32),
                   jax.ShapeDtypeStruct((bd, H_ATT, 128), jnp.int32)],
        compiler_params=_cparams("parallel", "arbitrary"),
    )(page_table.reshape(-1), q, *([kt_pool] * PAGES_PER_STEP))


def _moba_sample_body(pt_ref, top_ref, q_ref, kn_ref, vn_ref, *refs, past):
    s_refs = refs[:N_SEL_PAGES]
    v_refs = refs[N_SEL_PAGES:2 * N_SEL_PAGES]
    o_ref = refs[2 * N_SEL_PAGES]
    b = pl.program_id(0)
    h = pl.program_id(1)
    scale = HEAD_DIM ** -0.5
    slope = jnp.exp2(-(jnp.full((1, PAGE_SIZE), h, jnp.int32) + 1).astype(F32))
    lane = lax.broadcasted_iota(jnp.int32, (1, PAGE_SIZE), 1)

    logits = []
    for j in range(N_SEL_PAGES):
        blk = top_ref[(b * H_ATT + h) * MOBA_TOPK + j // PAGES_PER_BLOCK]
        key_pos = blk * MOBA_BLOCK + (j % PAGES_PER_BLOCK) * PAGE_SIZE + lane
        logits.append(s_refs[j][0, 0, pl.ds(h, 1), :] * scale - slope * (past - key_pos).astype(F32))
    s_self = jnp.sum(q_ref[0, pl.ds(h, 1), :] * kn_ref[0, pl.ds(h, 1), :], axis=-1, keepdims=True) * scale

    m = s_self
    for s in logits:
        m = jnp.maximum(m, jnp.max(s, axis=-1, keepdims=True))
    p_self = jnp.exp(s_self - m)
    l = p_self
    acc = p_self * vn_ref[0, pl.ds(h, 1), :]
    for s, v_ref in zip(logits, v_refs):
        p = jnp.exp(s - m)
        l = l + jnp.sum(p, axis=-1, keepdims=True)
        acc = acc + _bdot(jnp.broadcast_to(p, (8, PAGE_SIZE)), v_ref[0, 0], _NT)[0:1]
    o_ref[0, 0] = acc / l


def _moba_sample(q, k_new, v_new, scores, vt_pool, page_table, top):
    bd, n_pages = page_table.shape
    past = n_pages * PAGE_SIZE
    assert past % MOBA_BLOCK == 0

    def tok_spec():
        return pl.BlockSpec((1, H_ATT, HEAD_DIM), lambda b, h, pt, tp: (b, 0, 0))

    def sel_page(b, h, tp, j):
        return tp[(b * H_ATT + h) * MOBA_TOPK + j // PAGES_PER_BLOCK] * PAGES_PER_BLOCK + j % PAGES_PER_BLOCK

    def score_spec(j):
        return pl.BlockSpec((1, 1, H_ATT, PAGE_SIZE), lambda b, h, pt, tp: (b, sel_page(b, h, tp, j), 0, 0))

    def value_spec(j):
        return pl.BlockSpec((1, 1, HEAD_DIM, PAGE_SIZE),
                            lambda b, h, pt, tp: (pt[b * n_pages + sel_page(b, h, tp, j)], h, 0, 0))

    grid_spec = pltpu.PrefetchScalarGridSpec(
        num_scalar_prefetch=2,
        grid=(bd, H_ATT),
        in_specs=[tok_spec(), tok_spec(), tok_spec()]
        + [score_spec(j) for j in range(N_SEL_PAGES)] + [value_spec(j) for j in range(N_SEL_PAGES)],
        out_specs=pl.BlockSpec((1, 1, 1, HEAD_DIM), lambda b, h, pt, tp: (b, h, 0, 0)),
    )
    return pl.pallas_call(
        functools.partial(_moba_sample_body, past=past),
        grid_spec=grid_spec,
        out_shape=jax.ShapeDtypeStruct((bd, H_ATT, 1, HEAD_DIM), F32),
        compiler_params=_cparams("parallel", "arbitrary"),
    )(page_table.reshape(-1), top.reshape(-1), q, k_new, v_new,
      *([scores] * N_SEL_PAGES), *([vt_pool] * N_SEL_PAGES))


def _softplus(z):
    return jnp.maximum(z, 0.0) + jnp.log1p(jnp.exp(-jnp.abs(z)))


def _sigmoid(z):
    return 1.0 / (1.0 + jnp.exp(-z))


def _rwkv_token_terms(m, w0, w_up, a0, a_up, g_up, k_k, k_a):
    rw = RWKV_WIDTH
    r, k, v = m[:, 0:rw], m[:, rw:2 * rw], m[:, 2 * rw:3 * rw]
    xw = m[:, 3 * rw:3 * rw + W_LORA]
    xa = m[:, 3 * rw + W_LORA:3 * rw + W_LORA + A_LORA]
    xg = m[:, 3 * rw + W_LORA + A_LORA:]
    w_log = -_softplus(-(w0 + _dot(jnp.tanh(xw), w_up, precision=HIGHEST))) - 0.5
    log_decay = -jnp.exp(w_log)
    a = _sigmoid(a0 + _dot(xa, a_up, precision=HIGHEST))
    g = _bdot(_sigmoid(xg), g_up)
    kk = k * k_k
    k2 = k * (1.0 + (a - 1.0) * k_a)
    return r, log_decay, k2, v, kk, a, g


def _head_norm(kk_h):
    return kk_h / jnp.maximum(jnp.sqrt(jnp.sum(kk_h * kk_h, axis=-1, keepdims=True)), 1e-12)


def _rwkv_head_out(y, r_h, k_h, v_h, g_h, rk_h, gng_h, gnb_h):
    mu = jnp.mean(y, axis=-1, keepdims=True)
    d = y - mu
    var = jnp.mean(d * d, axis=-1, keepdims=True)
    yn = d * lax.rsqrt(var + GN_EPS) * gng_h + gnb_h
    bonus = jnp.sum(r_h * k_h * rk_h, axis=-1, keepdims=True) * v_h
    return (yn + bonus) * g_h


def _rwkv_prompt_body(rw_ref, prev_ref, s0_ref, mu_ref, w0_ref, wup_ref, a0_ref, aup_ref, gup_ref,
                      kk_ref, ka_ref, rk_ref, gng_ref, gnb_ref, y_ref, sfin_ref, state_s, prev_s):
    c = pl.program_id(1)
    ch = RWKV_CHUNK

    @pl.when(c == 0)
    def _():
        state_s[...] = s0_ref[0]
        prev_s[...] = prev_ref[0]

    rw = rw_ref[0]
    row = lax.broadcasted_iota(jnp.int32, rw.shape, 0)
    rw_prev = jnp.where(row == 0, prev_s[...], pltpu.roll(rw, 1, 0))
    prev_s[...] = rw[ch - 1:ch, :]
    m = rw + (rw_prev - rw) * mu_ref[...]
    r, log_decay, k2, v, kk, a, g = _rwkv_token_terms(
        m, w0_ref[...], wup_ref[...], a0_ref[...], aup_ref[...], gup_ref[...], kk_ref[...], ka_ref[...])

    ti = lax.broadcasted_iota(jnp.int32, (ch, ch), 0)
    si = lax.broadcasted_iota(jnp.int32, (ch, ch), 1)
    lower = si <= ti
    strict = si < ti
    eye = (si == ti).astype(F32)
    cs = _dot(lower.astype(F32), log_decay, precision=HIGHEST)
    gam = jnp.exp(cs)
    gam_prev = jnp.exp(cs - log_decay)
    gam_inv = jnp.exp(-cs)
    gam_tail = jnp.exp(cs[ch - 1:ch, :] - cs)

    for h in range(H_RWKV):
        hs = slice(h * RWKV_HEAD, (h + 1) * RWKV_HEAD)
        kkn = _head_norm(kk[:, hs])
        a_t = -kkn * gam_prev[:, hs]
        b_h = kkn * a[:, hs]
        b_t = b_h * gam_inv[:, hs]
        k_t = k2[:, hs] * gam_inv[:, hs]
        r_t = r[:, hs] * gam[:, hs]
        v_h = v[:, hs]
        s_0 = state_s[hs, :]

        l_ab = jnp.where(strict, _dot(a_t, b_t, _NT, HIGHEST), 0.0)
        l_ak = jnp.where(strict, _dot(a_t, k_t, _NT, HIGHEST), 0.0)
        m_rb = jnp.where(lower, _dot(r_t, b_t, _NT, HIGHEST), 0.0)
        m_rk = jnp.where(lower, _dot(r_t, k_t, _NT, HIGHEST), 0.0)
        inv = eye + l_ab
        pw = l_ab
        span = 2
        while span < ch:
            pw = _dot(pw, pw, precision=HIGHEST)
            inv = inv + _dot(inv, pw, precision=HIGHEST)
            span *= 2
        u = _dot(inv, _bdot(a_t, s_0, _NT) + _bdot(l_ak, v_h), precision=HIGHEST)
        y = _bdot(r_t, s_0, _NT) + _bdot(m_rb, u) + _bdot(m_rk, v_h)
        state_s[hs, :] = (s_0 * gam[ch - 1:ch, hs]
                          + _bdot(u, b_h * gam_tail[:, hs], _TN)
                          + _bdot(v_h, k2[:, hs] * gam_tail[:, hs], _TN))
        y_ref[0, :, hs] = _rwkv_head_out(y, r[:, hs], k2[:, hs], v_h, g[:, hs],
                                         rk_ref[:, hs], gng_ref[:, hs], gnb_ref[:, hs])

    @pl.when(c == pl.num_programs(1) - 1)
    def _():
        sfin_ref[0] = state_s[...]


def _rwkv_prompt(rw, prev_row, s0, p):
    bsz, t, _ = rw.shape
    assert t % RWKV_CHUNK == 0
    hn = H_RWKV * RWKV_HEAD
    vec = lambda n: _full((1, n))
    return pl.pallas_call(
        _rwkv_prompt_body,
        grid=(bsz, t // RWKV_CHUNK),
        in_specs=[pl.BlockSpec((1, RWKV_CHUNK, RWKV_COLS), lambda i, j: (i, j, 0)),
                  pl.BlockSpec((1, 1, RWKV_COLS), lambda i, j: (i, 0, 0)),
                  pl.BlockSpec((1, hn, RWKV_HEAD), lambda i, j: (i, 0, 0)),
                  vec(RWKV_COLS), vec(RWKV_WIDTH), _full((W_LORA, RWKV_WIDTH)), vec(RWKV_WIDTH),
                  _full((A_LORA, RWKV_WIDTH)), _full((G_LORA, RWKV_WIDTH)),
                  vec(RWKV_WIDTH), vec(RWKV_WIDTH), vec(RWKV_WIDTH), vec(RWKV_WIDTH), vec(RWKV_WIDTH)],
        out_specs=[pl.BlockSpec((1, RWKV_CHUNK, RWKV_WIDTH), lambda i, j: (i, j, 0)),
                   pl.BlockSpec((1, hn, RWKV_HEAD), lambda i, j: (i, 0, 0))],
        out_shape=[jax.ShapeDtypeStruct((bsz, t, RWKV_WIDTH), F32),
                   jax.ShapeDtypeStruct((bsz, hn, RWKV_HEAD), F32)],
        scratch_shapes=[pltpu.VMEM((hn, RWKV_HEAD), F32), pltpu.VMEM((1, RWKV_COLS), F32)],
        compiler_params=_cparams("parallel", "arbitrary"),
    )(rw, prev_row, s0, p["mu_shift"], p["w0"], p["w_up"], p["a0"], p["a_up"], p["g_up"],
      p["k_k"], p["k_a"], p["r_k"], p["gn_g"], p["gn_b"])


def _rows8(*rows):
    ri = lax.broadcasted_iota(jnp.int32, (8, rows[0].shape[1]), 0)
    out = jnp.zeros((8, rows[0].shape[1]), F32)
    for i, x in enumerate(rows):
        out = jnp.where(ri == i, x, out)
    return out


def _rwkv_step_body(rw_ref, prev_ref, s0_ref, mu_ref, w0_ref, wup_ref, a0_ref, aup_ref, gup_ref,
                    kk_ref, ka_ref, rk_ref, gng_ref, gnb_ref, y_ref, sfin_ref):
    rw = rw_ref[0]
    m = rw + (prev_ref[0] - rw) * mu_ref[...]
    m8 = jnp.broadcast_to(m, (8, RWKV_COLS))
    r, log_decay, k2, v, kk, a, g = _rwkv_token_terms(
        m8, w0_ref[...], wup_ref[...], a0_ref[...], aup_ref[...], gup_ref[...], kk_ref[...], ka_ref[...])
    decay = jnp.exp(log_decay)
    for h in range(H_RWKV):
        hs = slice(h * RWKV_HEAD, (h + 1) * RWKV_HEAD)
        kkn = _head_norm(kk[0:1, hs])
        a_h = -kkn
        b_h = kkn * a[0:1, hs]
        s_0 = s0_ref[0, hs, :]
        sa = _dot(_rows8(a_h), s_0, _NT, HIGHEST)[0:1]
        s_1 = (s_0 * decay[0:1, hs]
               + _dot(_rows8(sa, v[0:1, hs]), _rows8(b_h, k2[0:1, hs]), _TN, HIGHEST))
        y = _dot(_rows8(r[0:1, hs]), s_1, _NT, HIGHEST)[0:1]
        sfin_ref[0, hs, :] = s_1
        y_ref[0, :, hs] = _rwkv_head_out(y, r[0:1, hs], k2[0:1, hs], v[0:1, hs], g[0:1, hs],
                                         rk_ref[:, hs], gng_ref[:, hs], gnb_ref[:, hs])


def _rwkv_step(rw, prev_row, s0, p):
    bd = rw.shape[0]
    hn = H_RWKV * RWKV_HEAD
    vec = lambda n: _full((1, n))
    return pl.pallas_call(
        _rwkv_step_body,
        grid=(bd,),
        in_specs=[pl.BlockSpec((1, 1, RWKV_COLS), lambda i: (i, 0, 0)),
                  pl.BlockSpec((1, 1, RWKV_COLS), lambda i: (i, 0, 0)),
                  pl.BlockSpec((1, hn, RWKV_HEAD), lambda i: (i, 0, 0)),
                  vec(RWKV_COLS), vec(RWKV_WIDTH), _full((W_LORA, RWKV_WIDTH)), vec(RWKV_WIDTH),
                  _full((A_LORA, RWKV_WIDTH)), _full((G_LORA, RWKV_WIDTH)),
                  vec(RWKV_WIDTH), vec(RWKV_WIDTH), vec(RWKV_WIDTH), vec(RWKV_WIDTH), vec(RWKV_WIDTH)],
        out_specs=[pl.BlockSpec((1, 1, RWKV_WIDTH), lambda i: (i, 0, 0)),
                   pl.BlockSpec((1, hn, RWKV_HEAD), lambda i: (i, 0, 0))],
        out_shape=[jax.ShapeDtypeStruct((bd, 1, RWKV_WIDTH), F32),
                   jax.ShapeDtypeStruct((bd, hn, RWKV_HEAD), F32)],
        compiler_params=_cparams("parallel"),
    )(rw, prev_row, s0, p["mu_shift"], p["w0"], p["w_up"], p["a0"], p["a_up"], p["g_up"],
      p["k_k"], p["k_a"], p["r_k"], p["gn_g"], p["gn_b"])


def _row(x):
    return x.reshape(1, -1)


def kernel(x_prompt, x_sample, mem_prompt, cache_k_pool, cache_v_pool, page_table, cache_mem_k, cache_mem_v, state_wkv, state_shift, w_in, mu_shift, w0, w_up, a0, a_up, g_up, k_k, k_a, r_k, gn_g, gn_b, w_out, ln1_g, ln1_b, w_q_mem, w_kv_mem, w_o_mem, ln2_g, ln2_b, w_ff1, w_ff2, ln3_g, ln3_b):
    assert w_in.shape[0] == DEPTH == 1
    bsz, t, _ = x_prompt.shape
    bd = x_sample.shape[0]
    hn = H_RWKV * RWKV_HEAD

    w_in_b = w_in[0].astype(BF16)
    w_out_b = w_out[0].astype(BF16)
    wq_b = w_q_mem[0].astype(BF16)
    wkv_b = w_kv_mem[0].astype(BF16)
    wo_b = w_o_mem[0].astype(BF16)
    w1_b = w_ff1[0].astype(BF16)
    w2_b = w_ff2[0].astype(BF16)
    ln1 = (_row(ln1_g[0]), _row(ln1_b[0]))
    ln2 = (_row(ln2_g[0]), _row(ln2_b[0]))
    ln3 = (_row(ln3_g[0]), _row(ln3_b[0]))
    rwkv_p = dict(mu_shift=_row(mu_shift[0]), w0=_row(w0[0]), w_up=w_up[0], a0=_row(a0[0]), a_up=a_up[0],
                  g_up=g_up[0], k_k=_row(k_k[0]), k_a=_row(k_a[0]), r_k=_row(r_k[0]),
                  gn_g=_row(gn_g[0]), gn_b=_row(gn_b[0]))
    widths = (ATT_WIDTH, ATT_WIDTH, ATT_WIDTH, RWKV_COLS)

    xp = x_prompt.reshape(bsz * t, D_MODEL)
    qt_p, kt_p, vt_p, kb_p, rw_p3, kmean_p = _proj_prompt(x_prompt, w_in_b, tm=512)
    y_att_p = _moba_prompt(qt_p, kb_p, vt_p, kmean_p)
    y_rw_p, s_p = _rwkv_prompt(rw_p3, jnp.zeros((bsz, 1, RWKV_COLS), F32), jnp.zeros((bsz, hn, RWKV_HEAD), F32), rwkv_p)
    x1_p = _mm_res_ln([y_att_p.reshape(bsz * t, ATT_WIDTH), y_rw_p.reshape(bsz * t, RWKV_WIDTH)],
                      [w_out_b[:ATT_WIDTH], w_out_b[ATT_WIDTH:]], xp, *ln1, tm=256)
    mk_p, mv_p = _mm_multi(mem_prompt.reshape(bsz * N_MEM, D_MODEL), wkv_b, (D_MODEL, D_MODEL), tm=512)
    x2_p = _memx_prompt(x1_p.reshape(bsz, t, D_MODEL), mk_p.reshape(bsz, N_MEM, D_MODEL),
                        mv_p.reshape(bsz, N_MEM, D_MODEL), wq_b, wo_b, *ln2, tm=256)
    y_p = _mlp(x2_p.reshape(bsz * t, D_MODEL), w1_b, w2_b, *ln3, tm=256)

    kt_pool = cache_k_pool[0].transpose(0, 2, 3, 1)
    vt_pool = cache_v_pool[0].transpose(0, 2, 3, 1)
    xs = x_sample.reshape(bd, D_MODEL)
    q_s, k_s, v_s, rw_s = _mm_multi(xs, w_in_b, widths, tm=bd)
    q_s3, k_s3, v_s3 = (z.reshape(bd, H_ATT, HEAD_DIM) for z in (q_s, k_s, v_s))
    scores_s, top = _moba_scores(q_s3, kt_pool, page_table)
    y_att_s = _moba_sample(q_s3, k_s3, v_s3, scores_s, vt_pool, page_table, top[:, :, :MOBA_TOPK])
    y_rw_s, s_s = _rwkv_step(rw_s.reshape(bd, 1, RWKV_COLS), state_shift[0].reshape(bd, 1, RWKV_COLS),
                             state_wkv[0].reshape(bd, hn, RWKV_HEAD), rwkv_p)
    x1_s = _mm_res_ln([y_att_s.reshape(bd, ATT_WIDTH), y_rw_s.reshape(bd, RWKV_WIDTH)],
                      [w_out_b[:ATT_WIDTH], w_out_b[ATT_WIDTH:]], xs, *ln1, tm=bd)
    (qm_s,) = _mm_multi(x1_s, wq_b, (D_MODEL,), tm=bd)
    att_s = _memx_sample(qm_s.reshape(bd, 1, D_MODEL), cache_mem_k[0], cache_mem_v[0])
    x2_s = _mm_res_ln([att_s.reshape(bd, D_MODEL)], [wo_b], x1_s, *ln2, tm=bd)
    y_s = _mlp(x2_s, w1_b, w2_b, *ln3, tm=bd)

    return (y_p.reshape(bsz, t, D_MODEL), y_s.reshape(bd, 1, D_MODEL),
            kt_p.reshape(bsz, H_ATT, HEAD_DIM, t).transpose(0, 3, 1, 2)[None],
            vt_p.reshape(bsz, H_ATT, HEAD_DIM, t).transpose(0, 3, 1, 2)[None],
            k_s.reshape(1, bd, 1, H_ATT, HEAD_DIM), v_s.reshape(1, bd, 1, H_ATT, HEAD_DIM),
            mk_p.reshape(1, bsz, N_MEM, MEM_HEADS, MEM_HEAD_DIM), mv_p.reshape(1, bsz, N_MEM, MEM_HEADS, MEM_HEAD_DIM),
            s_p.reshape(1, bsz, H_RWKV, RWKV_HEAD, RWKV_HEAD), rw_p3[:, -1][None],
            s_s.reshape(1, bd, H_RWKV, RWKV_HEAD, RWKV_HEAD), rw_s[None])
```

```python
import functools

import jax
import jax.numpy as jnp
from jax import lax
from jax.experimental import pallas as pl
from jax.experimental.pallas import tpu as pltpu

F32 = jnp.float32
BF16 = jnp.bfloat16
HIGHEST = lax.Precision.HIGHEST

D_MODEL = 1024
PAGE_SIZE = 128
ATT_WIDTH = 512
RWKV_WIDTH = 512
HEAD_DIM = 64
H_ATT = 8
MOBA_BLOCK = 256
MOBA_TOPK = 3
Q_CHUNK = 128
RWKV_HEAD = 64
H_RWKV = 8
W_LORA = 64
A_LORA = 64
G_LORA = 128
RWKV_COLS = 3 * RWKV_WIDTH + W_LORA + A_LORA + G_LORA
PROJ_COLS = 3 * ATT_WIDTH + RWKV_COLS
N_MEM = 256
MEM_HEADS = 4
MEM_HEAD_DIM = 256
D_FF = 4 * D_MODEL
LN_EPS = 1e-5
GN_EPS = 64e-5
DEPTH = 1
ALPHA = (2.0 * DEPTH) ** 0.25

RWKV_CHUNK = 64
NEG = -1e30
VMEM_LIMIT = 56 * 1024 * 1024

_NT = (((1,), (1,)), ((), ()))
_TN = (((0,), (0,)), ((), ()))


def _cparams(*sem):
    return pltpu.CompilerParams(dimension_semantics=sem, vmem_limit_bytes=VMEM_LIMIT)


def _dot(a, b, dims=None, precision=None):
    if dims is None:
        return jnp.dot(a, b, preferred_element_type=F32, precision=precision)
    return lax.dot_general(a, b, dims, preferred_element_type=F32, precision=precision)


def _bdot(a, b, dims=None):
    return _dot(a.astype(BF16), b.astype(BF16), dims)


def _layer_norm(z, g, b):
    mu = jnp.mean(z, axis=-1, keepdims=True)
    d = z - mu
    var = jnp.mean(d * d, axis=-1, keepdims=True)
    return d * lax.rsqrt(var + LN_EPS) * g + b


def _full(shape):
    n = len(shape)
    return pl.BlockSpec(shape, lambda *_: (0,) * n)


def _col_chunks(n, width=512):
    out, c = [], 0
    while c < n:
        w = min(width, n - c)
        out.append((c, w))
        c += w
    return out


def _mm_multi_body(x_ref, w_ref, *o_refs, col0s):
    xb = x_ref[...].astype(BF16)
    for o_ref, c0 in zip(o_refs, col0s):
        for c, w in _col_chunks(o_ref.shape[1]):
            o_ref[:, c:c + w] = _dot(xb, w_ref[:, c0 + c:c0 + c + w])


def _mm_multi(x, w_bf16, widths, tm):
    m, k = x.shape
    col0s, c = [], 0
    for wd in widths:
        col0s.append(c)
        c += wd
    assert c == w_bf16.shape[1] and m % tm == 0
    return pl.pallas_call(
        functools.partial(_mm_multi_body, col0s=tuple(col0s)),
        grid=(m // tm,),
        in_specs=[pl.BlockSpec((tm, k), lambda i: (i, 0)), _full(w_bf16.shape)],
        out_specs=[pl.BlockSpec((tm, wd), lambda i: (i, 0)) for wd in widths],
        out_shape=[jax.ShapeDtypeStruct((m, wd), F32) for wd in widths],
        compiler_params=_cparams("parallel"),
    )(x, w_bf16)


def _proj_prompt_body(x_ref, w_ref, qt_ref, kt_ref, vt_ref, kb_ref, rw_ref, kmean_ref):
    j = pl.program_id(1)
    tm = x_ref.shape[1]
    xb = x_ref[0].astype(BF16)
    aw = ATT_WIDTH
    q = _dot(xb, w_ref[:, 0:aw])
    k = _dot(xb, w_ref[:, aw:2 * aw])
    v = _dot(xb, w_ref[:, 2 * aw:3 * aw])
    qt_ref[0] = q.T
    kt_ref[0] = k.T
    vt_ref[0] = v.T
    kb_ref[0] = k.astype(BF16)
    for c, w in _col_chunks(RWKV_COLS):
        rw_ref[0, :, c:c + w] = _dot(xb, w_ref[:, 3 * aw + c:3 * aw + c + w])
    blk_per_tile = tm // MOBA_BLOCK
    for i in range(blk_per_tile):
        kmean_ref[0, pl.ds(j * blk_per_tile + i, 1), :] = jnp.mean(
            k[i * MOBA_BLOCK:(i + 1) * MOBA_BLOCK], axis=0, keepdims=True)


def _proj_prompt(x, w_bf16, tm):
    bsz, t, _ = x.shape
    assert t % tm == 0 and tm % MOBA_BLOCK == 0
    n_blk = t // MOBA_BLOCK
    tr_spec = pl.BlockSpec((1, ATT_WIDTH, tm), lambda i, j: (i, 0, j))
    tr_shape = jax.ShapeDtypeStruct((bsz, ATT_WIDTH, t), F32)
    return pl.pallas_call(
        _proj_prompt_body,
        grid=(bsz, t // tm),
        in_specs=[pl.BlockSpec((1, tm, D_MODEL), lambda i, j: (i, j, 0)), _full(w_bf16.shape)],
        out_specs=[tr_spec, tr_spec, tr_spec,
                   pl.BlockSpec((1, tm, ATT_WIDTH), lambda i, j: (i, j, 0)),
                   pl.BlockSpec((1, tm, RWKV_COLS), lambda i, j: (i, j, 0)),
                   pl.BlockSpec((1, n_blk, ATT_WIDTH), lambda i, j: (i, 0, 0))],
        out_shape=[tr_shape, tr_shape, tr_shape,
                   jax.ShapeDtypeStruct((bsz, t, ATT_WIDTH), BF16),
                   jax.ShapeDtypeStruct((bsz, t, RWKV_COLS), F32),
                   jax.ShapeDtypeStruct((bsz, n_blk, ATT_WIDTH), F32)],
        compiler_params=_cparams("parallel", "arbitrary"),
    )(x, w_bf16)


def _mm_res_ln_body(*refs, n_in):
    x_refs = refs[:n_in]
    w_refs = refs[n_in:2 * n_in]
    res_ref, g_ref, b_ref, o_ref = refs[2 * n_in:]
    acc = _dot(x_refs[0][...].astype(BF16), w_refs[0][...])
    for x_ref, w_ref in zip(x_refs[1:], w_refs[1:]):
        acc = acc + _dot(x_ref[...].astype(BF16), w_ref[...])
    o_ref[...] = _layer_norm(ALPHA * res_ref[...] + acc, g_ref[...], b_ref[...])


def _mm_res_ln(xs, ws, res, g, b, tm):
    m = res.shape[0]
    n_in = len(xs)
    assert m % tm == 0
    in_specs = [pl.BlockSpec((tm, x.shape[1]), lambda i: (i, 0)) for x in xs]
    in_specs += [_full(w.shape) for w in ws]
    in_specs += [pl.BlockSpec((tm, D_MODEL), lambda i: (i, 0)), _full((1, D_MODEL)), _full((1, D_MODEL))]
    return pl.pallas_call(
        functools.partial(_mm_res_ln_body, n_in=n_in),
        grid=(m // tm,),
        in_specs=in_specs,
        out_specs=pl.BlockSpec((tm, D_MODEL), lambda i: (i, 0)),
        out_shape=jax.ShapeDtypeStruct((m, D_MODEL), F32),
        compiler_params=_cparams("parallel"),
    )(*xs, *ws, res, g, b)


def _mlp_body(x_ref, w1_ref, w2_ref, g_ref, b_ref, o_ref):
    x = x_ref[...]
    xb = x.astype(BF16)
    acc = jnp.zeros(x.shape, F32)
    for c, w in _col_chunks(D_FF):
        h = jnp.maximum(_dot(xb, w1_ref[:, c:c + w]), 0.0)
        acc = acc + _dot((h * h).astype(BF16), w2_ref[c:c + w, :])
    o_ref[...] = _layer_norm(ALPHA * x + acc, g_ref[...], b_ref[...])


def _mlp(x, w1, w2, g, b, tm):
    m = x.shape[0]
    assert m % tm == 0
    return pl.pallas_call(
        _mlp_body,
        grid=(m // tm,),
        in_specs=[pl.BlockSpec((tm, D_MODEL), lambda i: (i, 0)), _full(w1.shape), _full(w2.shape),
                  _full((1, D_MODEL)), _full((1, D_MODEL))],
        out_specs=pl.BlockSpec((tm, D_MODEL), lambda i: (i, 0)),
        out_shape=jax.ShapeDtypeStruct((m, D_MODEL), F32),
        compiler_params=_cparams("parallel"),
    )(x, w1, w2, g, b)


def _mem_attend(q, k_head, v_head):
    heads = range(MEM_HEADS)
    s = [_bdot(q[:, h * MEM_HEAD_DIM:(h + 1) * MEM_HEAD_DIM], k_head(h), _NT) for h in heads]
    p = [jnp.exp(x - jnp.max(x, axis=-1, keepdims=True)) for x in s]
    l = [jnp.sum(x, axis=-1, keepdims=True) for x in p]
    return jnp.concatenate([_bdot(x, v_head(h)) / z for h, x, z in zip(heads, p, l)], axis=-1)


def _memx_prompt_body(x_ref, mk_ref, mv_ref, wq_ref, wo_ref, g_ref, b_ref, o_ref):
    x = x_ref[0]
    q = _dot(x.astype(BF16), wq_ref[...]) * (MEM_HEAD_DIM ** -0.5)
    cols = lambda h: slice(h * MEM_HEAD_DIM, (h + 1) * MEM_HEAD_DIM)
    att = _mem_attend(q, lambda h: mk_ref[0, :, cols(h)], lambda h: mv_ref[0, :, cols(h)])
    acc = _dot(att.astype(BF16), wo_ref[...])
    o_ref[0] = _layer_norm(ALPHA * x + acc, g_ref[...], b_ref[...])


def _memx_prompt(x, mk, mv, wq, wo, g, b, tm):
    bsz, t, _ = x.shape
    assert t % tm == 0
    return pl.pallas_call(
        _memx_prompt_body,
        grid=(bsz, t // tm),
        in_specs=[pl.BlockSpec((1, tm, D_MODEL), lambda i, j: (i, j, 0)),
                  pl.BlockSpec((1, N_MEM, D_MODEL), lambda i, j: (i, 0, 0)),
                  pl.BlockSpec((1, N_MEM, D_MODEL), lambda i, j: (i, 0, 0)),
                  _full(wq.shape), _full(wo.shape), _full((1, D_MODEL)), _full((1, D_MODEL))],
        out_specs=pl.BlockSpec((1, tm, D_MODEL), lambda i, j: (i, j, 0)),
        out_shape=jax.ShapeDtypeStruct(x.shape, F32),
        compiler_params=_cparams("parallel", "parallel"),
    )(x, mk, mv, wq, wo, g, b)


def _memx_sample_body(q_ref, mk_ref, mv_ref, o_ref):
    n_rows = N_MEM * MEM_HEADS
    k_all = mk_ref[0].reshape(n_rows, MEM_HEAD_DIM)
    v_all = mv_ref[0].reshape(n_rows, MEM_HEAD_DIM)
    q = q_ref[0] * (MEM_HEAD_DIM ** -0.5)
    q_rows = [q[:, h * MEM_HEAD_DIM:(h + 1) * MEM_HEAD_DIM] for h in range(MEM_HEADS)]
    q8 = jnp.concatenate(q_rows + [jnp.zeros((8 - MEM_HEADS, MEM_HEAD_DIM), F32)], axis=0)
    s = _bdot(q8, k_all, _NT)
    row = lax.broadcasted_iota(jnp.int32, s.shape, 0)
    own = (lax.broadcasted_iota(jnp.int32, s.shape, 1) % MEM_HEADS) == row
    s = jnp.where(own, s, NEG)
    p = jnp.where(own, jnp.exp(s - jnp.max(s, axis=-1, keepdims=True)), 0.0)
    l = jnp.sum(p, axis=-1, keepdims=True)
    o8 = _bdot(p, v_all) / jnp.where(l > 0.0, l, 1.0)
    o_ref[0] = jnp.concatenate([o8[h:h + 1] for h in range(MEM_HEADS)], axis=-1)


def _memx_sample(q, mk, mv):
    bd = q.shape[0]
    mem_spec = pl.BlockSpec((1, N_MEM, MEM_HEADS, MEM_HEAD_DIM), lambda i: (i, 0, 0, 0))
    return pl.pallas_call(
        _memx_sample_body,
        grid=(bd,),
        in_specs=[pl.BlockSpec((1, 1, D_MODEL), lambda i: (i, 0, 0)), mem_spec, mem_spec],
        out_specs=pl.BlockSpec((1, 1, D_MODEL), lambda i: (i, 0, 0)),
        out_shape=jax.ShapeDtypeStruct((bd, 1, D_MODEL), F32),
        compiler_params=_cparams("parallel"),
    )(q, mk, mv)


def _top3_rows(g, n_iota, n_lim):
    g = jnp.where(n_iota < n_lim, g, -jnp.inf)
    n_rows = g.shape[0]
    sel = n_iota < 0
    for _ in range(MOBA_TOPK):
        mx = jnp.max(g, axis=0, keepdims=True)
        idx = jnp.min(jnp.where(g == mx, n_iota, n_rows), axis=0, keepdims=True)
        hit = n_iota == idx
        sel = sel | hit
        g = jnp.where(hit, -jnp.inf, g)
    return sel & (n_iota < n_lim)


def _moba_prompt_body(qt_ref, kb_ref, vt_ref, kmean_ref, o_ref, vt_s, sel_s, acc_s, *, n_blk):
    c = pl.program_id(1)

    @pl.when(c == 0)
    def _():
        for n in range(n_blk):
            vt_s[n] = vt_ref[0, :, n * MOBA_BLOCK:(n + 1) * MOBA_BLOCK].astype(BF16)

    heads = [slice(h * HEAD_DIM, (h + 1) * HEAD_DIM) for h in range(H_ATT)]
    q_blk = (c * Q_CHUNK) // MOBA_BLOCK
    qt = qt_ref[0]
    n_iota = lax.broadcasted_iota(jnp.int32, (n_blk, Q_CHUNK), 0)
    gates = [_dot(kmean_ref[0, :, hs], qt[hs, :], precision=HIGHEST) for hs in heads]
    for h, gate in enumerate(gates):
        sel_s[h * n_blk:(h + 1) * n_blk, :] = _top3_rows(gate, n_iota, q_blk).astype(F32)
    acc_s[...] = jnp.zeros(acc_s.shape, F32)

    qtb = (qt * (HEAD_DIM ** -0.5)).astype(BF16)
    rel = (lax.broadcasted_iota(jnp.int32, (MOBA_BLOCK, Q_CHUNK), 1)
           - lax.broadcasted_iota(jnp.int32, (MOBA_BLOCK, Q_CHUNK), 0))
    head_row = lax.broadcasted_iota(jnp.int32, (H_ATT, Q_CHUNK), 0)

    def attend(n, m_all, l_all, valid_of):
        dist = (rel + (c * Q_CHUNK - n * MOBA_BLOCK)).astype(F32)
        kblk = kb_ref[0, pl.ds(pl.multiple_of(n * MOBA_BLOCK, MOBA_BLOCK), MOBA_BLOCK), :]
        vblk = vt_s[n]
        s = [_dot(kblk[:, hs], qtb[hs, :]) - (2.0 ** -(h + 1)) * dist for h, hs in enumerate(heads)]
        valid = [valid_of(h, dist) for h in range(H_ATT)]
        s = [jnp.where(ok, x, NEG) for ok, x in zip(valid, s)]
        m_old = [m_all[h:h + 1, :] for h in range(H_ATT)]
        m_new = [jnp.maximum(mo, jnp.max(x, axis=0, keepdims=True)) for mo, x in zip(m_old, s)]
        p = [jnp.where(ok, jnp.exp(x - mn), 0.0) for ok, x, mn in zip(valid, s, m_new)]
        corr = [jnp.exp(mo - mn) for mo, mn in zip(m_old, m_new)]
        pv = [_dot(vblk[hs, :], x.astype(BF16)) for hs, x in zip(heads, p)]
        for h, hs in enumerate(heads):
            acc_s[hs, :] = acc_s[hs, :] * corr[h] + pv[h]
            l_h = l_all[h:h + 1, :] * corr[h] + jnp.sum(p[h], axis=0, keepdims=True)
            l_all = jnp.where(head_row == h, l_h, l_all)
            m_all = jnp.where(head_row == h, m_new[h], m_all)
        return m_all, l_all

    def past_block(n, carry):
        return attend(n, *carry, lambda h, dist: sel_s[pl.ds(h * n_blk + n, 1), :] > 0.0)

    m0 = jnp.full((H_ATT, Q_CHUNK), NEG, F32)
    l0 = jnp.zeros((H_ATT, Q_CHUNK), F32)
    m_all, l_all = lax.fori_loop(0, q_blk, past_block, (m0, l0))
    m_all, l_all = attend(q_blk, m_all, l_all, lambda h, dist: dist >= 0.0)
    for h, hs in enumerate(heads):
        acc_s[hs, :] = acc_s[hs, :] / l_all[h:h + 1, :]
    o_ref[0] = acc_s[...].T


def _moba_prompt(qt, kb, vt, kmean):
    bsz, t, _ = kb.shape
    assert t % MOBA_BLOCK == 0 and MOBA_BLOCK % Q_CHUNK == 0
    n_blk = t // MOBA_BLOCK
    return pl.pallas_call(
        functools.partial(_moba_prompt_body, n_blk=n_blk),
        grid=(bsz, t // Q_CHUNK),
        in_specs=[pl.BlockSpec((1, ATT_WIDTH, Q_CHUNK), lambda i, j: (i, 0, j)),
                  pl.BlockSpec((1, t, ATT_WIDTH), lambda i, j: (i, 0, 0)),
                  pl.BlockSpec((1, ATT_WIDTH, t), lambda i, j: (i, 0, 0)),
                  pl.BlockSpec((1, n_blk, ATT_WIDTH), lambda i, j: (i, 0, 0))],
        out_specs=pl.BlockSpec((1, Q_CHUNK, ATT_WIDTH), lambda i, j: (i, j, 0)),
        out_shape=jax.ShapeDtypeStruct((bsz, t, ATT_WIDTH), F32),
        scratch_shapes=[pltpu.VMEM((n_blk, ATT_WIDTH, MOBA_BLOCK), BF16),
                        pltpu.VMEM((H_ATT * n_blk, Q_CHUNK), F32),
                        pltpu.VMEM((ATT_WIDTH, Q_CHUNK), F32)],
        compiler_params=_cparams("parallel", "arbitrary"),
    )(qt, kb, vt, kmean)


PAGES_PER_STEP = 32
PAGES_PER_BLOCK = MOBA_BLOCK // PAGE_SIZE
N_SEL_PAGES = MOBA_TOPK * PAGES_PER_BLOCK


def _moba_scores_body(pt_ref, q_ref, *refs, n_pages):
    page_refs = refs[:PAGES_PER_STEP]
    s_ref, top_ref, qcol_s, gate_s = refs[PAGES_PER_STEP:]
    g = pl.program_id(1)
    n_blk = n_pages // PAGES_PER_BLOCK
    blk_per_step = PAGES_PER_STEP // PAGES_PER_BLOCK

    @pl.when(g == 0)
    def _():
        for h in range(H_ATT):
            qcol_s[h] = jnp.broadcast_to(q_ref[0, h:h + 1, :], (PAGE_SIZE, HEAD_DIM)).T
        gate_s[...] = jnp.zeros(gate_s.shape, F32)

    lane = lax.broadcasted_iota(jnp.int32, (H_ATT, 128), 1)
    for i in range(blk_per_step):
        blk_sum = jnp.zeros((H_ATT, PAGE_SIZE), F32)
        for jj in range(PAGES_PER_BLOCK):
            p = i * PAGES_PER_BLOCK + jj
            s = jnp.sum(page_refs[p][0] * qcol_s[...], axis=1)
            s_ref[0, p] = s
            blk_sum = blk_sum + s
        gate = jnp.sum(blk_sum, axis=1, keepdims=True) * (1.0 / MOBA_BLOCK)
        gate_s[...] = jnp.where(lane == g * blk_per_step + i, gate, gate_s[...])

    @pl.when(g == pl.num_programs(1) - 1)
    def _():
        g_work = jnp.where(lane < n_blk, gate_s[...], -jnp.inf)
        out = jnp.zeros((H_ATT, 128), jnp.int32)
        for r in range(MOBA_TOPK):
            mx = jnp.max(g_work, axis=1, keepdims=True)
            idx = jnp.min(jnp.where(g_work == mx, lane, 128), axis=1, keepdims=True)
            out = jnp.where(lane == r, idx, out)
            g_work = jnp.where(lane == idx, -jnp.inf, g_work)
        top_ref[0] = out


def _moba_scores(q, kt_pool, page_table):
    bd, n_pages = page_table.shape
    assert n_pages % PAGES_PER_STEP == 0 and PAGES_PER_STEP % PAGES_PER_BLOCK == 0
    n_blk = n_pages // PAGES_PER_BLOCK
    assert MOBA_TOPK <= n_blk <= 128

    def page_spec(i):
        return pl.BlockSpec((1, H_ATT, HEAD_DIM, PAGE_SIZE),
                            lambda b, g, pt: (pt[b * n_pages + g * PAGES_PER_STEP + i], 0, 0, 0))

    grid_spec = pltpu.PrefetchScalarGridSpec(
        num_scalar_prefetch=1,
        grid=(bd, n_pages // PAGES_PER_STEP),
        in_specs=[pl.BlockSpec((1, H_ATT, HEAD_DIM), lambda b, g, pt: (b, 0, 0))]
        + [page_spec(i) for i in range(PAGES_PER_STEP)],
        out_specs=[pl.BlockSpec((1, PAGES_PER_STEP, H_ATT, PAGE_SIZE), lambda b, g, pt: (b, g, 0, 0)),
                   pl.BlockSpec((1, H_ATT, 128), lambda b, g, pt: (b, 0, 0))],
        scratch_shapes=[pltpu.VMEM((H_ATT, HEAD_DIM, PAGE_SIZE), F32), pltpu.VMEM((H_ATT, 128), F32)],
    )
    return pl.pallas_call(
        functools.partial(_moba_scores_body, n_pages=n_pages),
        grid_spec=grid_spec,
        out_shape=[jax.ShapeDtypeStruct((bd, n_pages, H_ATT, PAGE_SIZE), F32),
                   jax.ShapeDtypeStruct((bd, H_ATT, 128), jnp.int32)],
        compiler_params=_cparams("parallel", "arbitrary"),
    )(page_table.reshape(-1), q, *([kt_pool] * PAGES_PER_STEP))


def _moba_sample_body(pt_ref, top_ref, q_ref, kn_ref, vn_ref, *refs, past):
    s_ref = refs[0]
    v_refs = refs[1:1 + N_SEL_PAGES]
    o_ref = refs[1 + N_SEL_PAGES]
    b = pl.program_id(0)
    h = pl.program_id(1)
    scale = HEAD_DIM ** -0.5
    slope = jnp.exp2(-(jnp.full((1, PAGE_SIZE), h, jnp.int32) + 1).astype(F32))
    lane = lax.broadcasted_iota(jnp.int32, (1, PAGE_SIZE), 1)

    logits = []
    for j in range(N_SEL_PAGES):
        blk = top_ref[(b * H_ATT + h) * MOBA_TOPK + j // PAGES_PER_BLOCK]
        page = blk * PAGES_PER_BLOCK + j % PAGES_PER_BLOCK
        key_pos = page * PAGE_SIZE + lane
        logits.append(s_ref[0, page, pl.ds(h, 1), :] * scale - slope * (past - key_pos).astype(F32))
    s_self = jnp.sum(q_ref[0, pl.ds(h, 1), :] * kn_ref[0, pl.ds(h, 1), :], axis=-1, keepdims=True) * scale

    m = s_self
    for s in logits:
        m = jnp.maximum(m, jnp.max(s, axis=-1, keepdims=True))
    p_self = jnp.exp(s_self - m)
    l = p_self
    acc = p_self * vn_ref[0, pl.ds(h, 1), :]
    for s, v_ref in zip(logits, v_refs):
        p = jnp.exp(s - m)
        l = l + jnp.sum(p, axis=-1, keepdims=True)
        acc = acc + _bdot(jnp.broadcast_to(p, (8, PAGE_SIZE)), v_ref[0, 0], _NT)[0:1]
    o_ref[0, 0] = acc / l


def _moba_sample(q, k_new, v_new, scores, vt_pool, page_table, top):
    bd, n_pages = page_table.shape
    past = n_pages * PAGE_SIZE
    assert past % MOBA_BLOCK == 0

    def tok_spec():
        return pl.BlockSpec((1, H_ATT, HEAD_DIM), lambda b, h, pt, tp: (b, 0, 0))

    def sel_page(b, h, tp, j):
        return tp[(b * H_ATT + h) * MOBA_TOPK + j // PAGES_PER_BLOCK] * PAGES_PER_BLOCK + j % PAGES_PER_BLOCK

    def value_spec(j):
        return pl.BlockSpec((1, 1, HEAD_DIM, PAGE_SIZE),
                            lambda b, h, pt, tp: (pt[b * n_pages + sel_page(b, h, tp, j)], h, 0, 0))

    grid_spec = pltpu.PrefetchScalarGridSpec(
        num_scalar_prefetch=2,
        grid=(bd, H_ATT),
        in_specs=[tok_spec(), tok_spec(), tok_spec(),
                  pl.BlockSpec((1, n_pages, H_ATT, PAGE_SIZE), lambda b, h, pt, tp: (b, 0, 0, 0))]
        + [value_spec(j) for j in range(N_SEL_PAGES)],
        out_specs=pl.BlockSpec((1, 1, 1, HEAD_DIM), lambda b, h, pt, tp: (b, h, 0, 0)),
    )
    return pl.pallas_call(
        functools.partial(_moba_sample_body, past=past),
        grid_spec=grid_spec,
        out_shape=jax.ShapeDtypeStruct((bd, H_ATT, 1, HEAD_DIM), F32),
        compiler_params=_cparams("parallel", "arbitrary"),
    )(page_table.reshape(-1), top.reshape(-1), q, k_new, v_new, scores, *([vt_pool] * N_SEL_PAGES))


def _softplus(z):
    return jnp.maximum(z, 0.0) + jnp.log1p(jnp.exp(-jnp.abs(z)))


def _sigmoid(z):
    return 1.0 / (1.0 + jnp.exp(-z))


def _rwkv_token_terms(m, w0, w_up, a0, a_up, g_up, k_k, k_a):
    rw = RWKV_WIDTH
    r, k, v = m[:, 0:rw], m[:, rw:2 * rw], m[:, 2 * rw:3 * rw]
    xw = m[:, 3 * rw:3 * rw + W_LORA]
    xa = m[:, 3 * rw + W_LORA:3 * rw + W_LORA + A_LORA]
    xg = m[:, 3 * rw + W_LORA + A_LORA:]
    w_log = -_softplus(-(w0 + _dot(jnp.tanh(xw), w_up, precision=HIGHEST))) - 0.5
    log_decay = -jnp.exp(w_log)
    a = _sigmoid(a0 + _dot(xa, a_up, precision=HIGHEST))
    g = _bdot(_sigmoid(xg), g_up)
    kk = k * k_k
    k2 = k * (1.0 + (a - 1.0) * k_a)
    return r, log_decay, k2, v, kk, a, g


def _head_norm(kk_h):
    return kk_h / jnp.maximum(jnp.sqrt(jnp.sum(kk_h * kk_h, axis=-1, keepdims=True)), 1e-12)


def _rwkv_head_out(y, r_h, k_h, v_h, g_h, rk_h, gng_h, gnb_h):
    mu = jnp.mean(y, axis=-1, keepdims=True)
    d = y - mu
    var = jnp.mean(d * d, axis=-1, keepdims=True)
    yn = d * lax.rsqrt(var + GN_EPS) * gng_h + gnb_h
    bonus = jnp.sum(r_h * k_h * rk_h, axis=-1, keepdims=True) * v_h
    return (yn + bonus) * g_h


def _rwkv_prompt_body(rw_ref, prev_ref, s0_ref, mu_ref, w0_ref, wup_ref, a0_ref, aup_ref, gup_ref,
                      kk_ref, ka_ref, rk_ref, gng_ref, gnb_ref, y_ref, sfin_ref, state_s, prev_s):
    c = pl.program_id(1)
    ch = RWKV_CHUNK

    @pl.when(c == 0)
    def _():
        state_s[...] = s0_ref[0]
        prev_s[...] = prev_ref[0]

    rw = rw_ref[0]
    row = lax.broadcasted_iota(jnp.int32, rw.shape, 0)
    rw_prev = jnp.where(row == 0, prev_s[...], pltpu.roll(rw, 1, 0))
    prev_s[...] = rw[ch - 1:ch, :]
    m = rw + (rw_prev - rw) * mu_ref[...]
    r, log_decay, k2, v, kk, a, g = _rwkv_token_terms(
        m, w0_ref[...], wup_ref[...], a0_ref[...], aup_ref[...], gup_ref[...], kk_ref[...], ka_ref[...])

    ti = lax.broadcasted_iota(jnp.int32, (ch, ch), 0)
    si = lax.broadcasted_iota(jnp.int32, (ch, ch), 1)
    lower = si <= ti
    strict = si < ti
    eye = (si == ti).astype(F32)
    cs = _dot(lower.astype(F32), log_decay, precision=HIGHEST)
    gam = jnp.exp(cs)
    gam_prev = jnp.exp(cs - log_decay)
    gam_inv = jnp.exp(-cs)
    gam_tail = jnp.exp(cs[ch - 1:ch, :] - cs)

    heads = [slice(h * RWKV_HEAD, (h + 1) * RWKV_HEAD) for h in range(H_RWKV)]
    kkn = [_head_norm(kk[:, hs]) for hs in heads]
    a_t = [-n * gam_prev[:, hs] for n, hs in zip(kkn, heads)]
    b_h = [n * a[:, hs] for n, hs in zip(kkn, heads)]
    b_t = [x * gam_inv[:, hs] for x, hs in zip(b_h, heads)]
    k_t = [k2[:, hs] * gam_inv[:, hs] for hs in heads]
    r_t = [r[:, hs] * gam[:, hs] for hs in heads]
    v_h = [v[:, hs] for hs in heads]
    s_0 = [state_s[hs, :] for hs in heads]
    l_ab = [jnp.where(strict, _bdot(x, z, _NT), 0.0) for x, z in zip(a_t, b_t)]
    l_ak = [jnp.where(strict, _bdot(x, z, _NT), 0.0) for x, z in zip(a_t, k_t)]
    m_rb = [jnp.where(lower, _bdot(x, z, _NT), 0.0) for x, z in zip(r_t, b_t)]
    m_rk = [jnp.where(lower, _bdot(x, z, _NT), 0.0) for x, z in zip(r_t, k_t)]
    inv = [eye + x for x in l_ab]
    pw = list(l_ab)
    span = 2
    while span < ch:
        pw = [_bdot(x, x) for x in pw]
        inv = [i + _bdot(i, x) for i, x in zip(inv, pw)]
        span *= 2
    w_ = [_bdot(x, s, _NT) + _bdot(l, vv) for x, s, l, vv in zip(a_t, s_0, l_ak, v_h)]
    u = [_bdot(i, x) for i, x in zip(inv, w_)]
    y = [_bdot(x, s, _NT) + _bdot(mb, uu) + _bdot(mk, vv)
         for x, s, mb, uu, mk, vv in zip(r_t, s_0, m_rb, u, m_rk, v_h)]
    s_new = [s * gam[ch - 1:ch, hs] + _bdot(uu, bb * gam_tail[:, hs], _TN) + _bdot(vv, k2[:, hs] * gam_tail[:, hs], _TN)
             for s, hs, uu, bb, vv in zip(s_0, heads, u, b_h, v_h)]
    for hs, sn, yy, vv in zip(heads, s_new, y, v_h):
        state_s[hs, :] = sn
        y_ref[0, :, hs] = _rwkv_head_out(yy, r[:, hs], k2[:, hs], vv, g[:, hs],
                                         rk_ref[:, hs], gng_ref[:, hs], gnb_ref[:, hs])

    @pl.when(c == pl.num_programs(1) - 1)
    def _():
        sfin_ref[0] = state_s[...]


def _rwkv_prompt(rw, prev_row, s0, p):
    bsz, t, _ = rw.shape
    assert t % RWKV_CHUNK == 0
    hn = H_RWKV * RWKV_HEAD
    vec = lambda n: _full((1, n))
    return pl.pallas_call(
        _rwkv_prompt_body,
        grid=(bsz, t // RWKV_CHUNK),
        in_specs=[pl.BlockSpec((1, RWKV_CHUNK, RWKV_COLS), lambda i, j: (i, j, 0)),
                  pl.BlockSpec((1, 1, RWKV_COLS), lambda i, j: (i, 0, 0)),
                  pl.BlockSpec((1, hn, RWKV_HEAD), lambda i, j: (i, 0, 0)),
                  vec(RWKV_COLS), vec(RWKV_WIDTH), _full((W_LORA, RWKV_WIDTH)), vec(RWKV_WIDTH),
                  _full((A_LORA, RWKV_WIDTH)), _full((G_LORA, RWKV_WIDTH)),
                  vec(RWKV_WIDTH), vec(RWKV_WIDTH), vec(RWKV_WIDTH), vec(RWKV_WIDTH), vec(RWKV_WIDTH)],
        out_specs=[pl.BlockSpec((1, RWKV_CHUNK, RWKV_WIDTH), lambda i, j: (i, j, 0)),
                   pl.BlockSpec((1, hn, RWKV_HEAD), lambda i, j: (i, 0, 0))],
        out_shape=[jax.ShapeDtypeStruct((bsz, t, RWKV_WIDTH), F32),
                   jax.ShapeDtypeStruct((bsz, hn, RWKV_HEAD), F32)],
        scratch_shapes=[pltpu.VMEM((hn, RWKV_HEAD), F32), pltpu.VMEM((1, RWKV_COLS), F32)],
        compiler_params=_cparams("parallel", "arbitrary"),
    )(rw, prev_row, s0, p["mu_shift"], p["w0"], p["w_up"], p["a0"], p["a_up"], p["g_up"],
      p["k_k"], p["k_a"], p["r_k"], p["gn_g"], p["gn_b"])


def _rows8(*rows):
    ri = lax.broadcasted_iota(jnp.int32, (8, rows[0].shape[1]), 0)
    out = jnp.zeros((8, rows[0].shape[1]), F32)
    for i, x in enumerate(rows):
        out = jnp.where(ri == i, x, out)
    return out


def _rwkv_step_body(rw_ref, prev_ref, s0_ref, mu_ref, w0_ref, wup_ref, a0_ref, aup_ref, gup_ref,
                    kk_ref, ka_ref, rk_ref, gng_ref, gnb_ref, y_ref, sfin_ref):
    rw = rw_ref[0]
    m = rw + (prev_ref[0] - rw) * mu_ref[...]
    m8 = jnp.broadcast_to(m, (8, RWKV_COLS))
    r, log_decay, k2, v, kk, a, g = _rwkv_token_terms(
        m8, w0_ref[...], wup_ref[...], a0_ref[...], aup_ref[...], gup_ref[...], kk_ref[...], ka_ref[...])
    decay = jnp.exp(log_decay)
    heads = [slice(h * RWKV_HEAD, (h + 1) * RWKV_HEAD) for h in range(H_RWKV)]
    kkn = [_head_norm(kk[0:1, hs]) for hs in heads]
    b_h = [n * a[0:1, hs] for n, hs in zip(kkn, heads)]
    s_0 = [s0_ref[0, hs, :] for hs in heads]
    sa = [_dot(_rows8(-n), s, _NT, HIGHEST)[0:1] for n, s in zip(kkn, s_0)]
    s_1 = [s * decay[0:1, hs] + _dot(_rows8(x, v[0:1, hs]), _rows8(bb, k2[0:1, hs]), _TN, HIGHEST)
           for s, hs, x, bb in zip(s_0, heads, sa, b_h)]
    y = [_dot(_rows8(r[0:1, hs]), s, _NT, HIGHEST)[0:1] for hs, s in zip(heads, s_1)]
    for hs, s, yy in zip(heads, s_1, y):
        sfin_ref[0, hs, :] = s
        y_ref[0, :, hs] = _rwkv_head_out(yy, r[0:1, hs], k2[0:1, hs], v[0:1, hs], g[0:1, hs],
                                         rk_ref[:, hs], gng_ref[:, hs], gnb_ref[:, hs])


def _rwkv_step(rw, prev_row, s0, p):
    bd = rw.shape[0]
    hn = H_RWKV * RWKV_HEAD
    vec = lambda n: _full((1, n))
    return pl.pallas_call(
        _rwkv_step_body,
        grid=(bd,),
        in_specs=[pl.BlockSpec((1, 1, RWKV_COLS), lambda i: (i, 0, 0)),
                  pl.BlockSpec((1, 1, RWKV_COLS), lambda i: (i, 0, 0)),
                  pl.BlockSpec((1, hn, RWKV_HEAD), lambda i: (i, 0, 0)),
                  vec(RWKV_COLS), vec(RWKV_WIDTH), _full((W_LORA, RWKV_WIDTH)), vec(RWKV_WIDTH),
                  _full((A_LORA, RWKV_WIDTH)), _full((G_LORA, RWKV_WIDTH)),
                  vec(RWKV_WIDTH), vec(RWKV_WIDTH), vec(RWKV_WIDTH), vec(RWKV_WIDTH), vec(RWKV_WIDTH)],
        out_specs=[pl.BlockSpec((1, 1, RWKV_WIDTH), lambda i: (i, 0, 0)),
                   pl.BlockSpec((1, hn, RWKV_HEAD), lambda i: (i, 0, 0))],
        out_shape=[jax.ShapeDtypeStruct((bd, 1, RWKV_WIDTH), F32),
                   jax.ShapeDtypeStruct((bd, hn, RWKV_HEAD), F32)],
        compiler_params=_cparams("parallel"),
    )(rw, prev_row, s0, p["mu_shift"], p["w0"], p["w_up"], p["a0"], p["a_up"], p["g_up"],
      p["k_k"], p["k_a"], p["r_k"], p["gn_g"], p["gn_b"])


def _row(x):
    return x.reshape(1, -1)


def kernel(x_prompt, x_sample, mem_prompt, cache_k_pool, cache_v_pool, page_table, cache_mem_k, cache_mem_v, state_wkv, state_shift, w_in, mu_shift, w0, w_up, a0, a_up, g_up, k_k, k_a, r_k, gn_g, gn_b, w_out, ln1_g, ln1_b, w_q_mem, w_kv_mem, w_o_mem, ln2_g, ln2_b, w_ff1, w_ff2, ln3_g, ln3_b):
    assert w_in.shape[0] == DEPTH == 1
    bsz, t, _ = x_prompt.shape
    bd = x_sample.shape[0]
    hn = H_RWKV * RWKV_HEAD

    w_in_b = w_in[0].astype(BF16)
    w_out_b = w_out[0].astype(BF16)
    wq_b = w_q_mem[0].astype(BF16)
    wkv_b = w_kv_mem[0].astype(BF16)
    wo_b = w_o_mem[0].astype(BF16)
    w1_b = w_ff1[0].astype(BF16)
    w2_b = w_ff2[0].astype(BF16)
    ln1 = (_row(ln1_g[0]), _row(ln1_b[0]))
    ln2 = (_row(ln2_g[0]), _row(ln2_b[0]))
    ln3 = (_row(ln3_g[0]), _row(ln3_b[0]))
    rwkv_p = dict(mu_shift=_row(mu_shift[0]), w0=_row(w0[0]), w_up=w_up[0], a0=_row(a0[0]), a_up=a_up[0],
                  g_up=g_up[0], k_k=_row(k_k[0]), k_a=_row(k_a[0]), r_k=_row(r_k[0]),
                  gn_g=_row(gn_g[0]), gn_b=_row(gn_b[0]))
    widths = (ATT_WIDTH, ATT_WIDTH, ATT_WIDTH, RWKV_COLS)

    xp = x_prompt.reshape(bsz * t, D_MODEL)
    qt_p, kt_p, vt_p, kb_p, rw_p3, kmean_p = _proj_prompt(x_prompt, w_in_b, tm=512)
    y_att_p = _moba_prompt(qt_p, kb_p, vt_p, kmean_p)
    y_rw_p, s_p = _rwkv_prompt(rw_p3, jnp.zeros((bsz, 1, RWKV_COLS), F32), jnp.zeros((bsz, hn, RWKV_HEAD), F32), rwkv_p)
    x1_p = _mm_res_ln([y_att_p.reshape(bsz * t, ATT_WIDTH), y_rw_p.reshape(bsz * t, RWKV_WIDTH)],
                      [w_out_b[:ATT_WIDTH], w_out_b[ATT_WIDTH:]], xp, *ln1, tm=256)
    mk_p, mv_p = _mm_multi(mem_prompt.reshape(bsz * N_MEM, D_MODEL), wkv_b, (D_MODEL, D_MODEL), tm=512)
    x2_p = _memx_prompt(x1_p.reshape(bsz, t, D_MODEL), mk_p.reshape(bsz, N_MEM, D_MODEL),
                        mv_p.reshape(bsz, N_MEM, D_MODEL), wq_b, wo_b, *ln2, tm=256)
    y_p = _mlp(x2_p.reshape(bsz * t, D_MODEL), w1_b, w2_b, *ln3, tm=256)

    kt_pool = cache_k_pool[0].transpose(0, 2, 3, 1)
    vt_pool = cache_v_pool[0].transpose(0, 2, 3, 1)
    xs = x_sample.reshape(bd, D_MODEL)
    q_s, k_s, v_s, rw_s = _mm_multi(xs, w_in_b, widths, tm=bd)
    q_s3, k_s3, v_s3 = (z.reshape(bd, H_ATT, HEAD_DIM) for z in (q_s, k_s, v_s))
    scores_s, top = _moba_scores(q_s3, kt_pool, page_table)
    y_att_s = _moba_sample(q_s3, k_s3, v_s3, scores_s, vt_pool, page_table, top[:, :, :MOBA_TOPK])
    y_rw_s, s_s = _rwkv_step(rw_s.reshape(bd, 1, RWKV_COLS), state_shift[0].reshape(bd, 1, RWKV_COLS),
                             state_wkv[0].reshape(bd, hn, RWKV_HEAD), rwkv_p)
    x1_s = _mm_res_ln([y_att_s.reshape(bd, ATT_WIDTH), y_rw_s.reshape(bd, RWKV_WIDTH)],
                      [w_out_b[:ATT_WIDTH], w_out_b[ATT_WIDTH:]], xs, *ln1, tm=bd)
    (qm_s,) = _mm_multi(x1_s, wq_b, (D_MODEL,), tm=bd)
    att_s = _memx_sample(qm_s.reshape(bd, 1, D_MODEL), cache_mem_k[0], cache_mem_v[0])
    x2_s = _mm_res_ln([att_s.reshape(bd, D_MODEL)], [wo_b], x1_s, *ln2, tm=bd)
    y_s = _mlp(x2_s, w1_b, w2_b, *ln3, tm=bd)

    return (y_p.reshape(bsz, t, D_MODEL), y_s.reshape(bd, 1, D_MODEL),
            kt_p.reshape(bsz, H_ATT, HEAD_DIM, t).transpose(0, 3, 1, 2)[None],
            vt_p.reshape(bsz, H_ATT, HEAD_DIM, t).transpose(0, 3, 1, 2)[None],
            k_s.reshape(1, bd, 1, H_ATT, HEAD_DIM), v_s.reshape(1, bd, 1, H_ATT, HEAD_DIM),
            mk_p.reshape(1, bsz, N_MEM, MEM_HEADS, MEM_HEAD_DIM), mv_p.reshape(1, bsz, N_MEM, MEM_HEADS, MEM_HEAD_DIM),
            s_p.reshape(1, bsz, H_RWKV, RWKV_HEAD, RWKV_HEAD), rw_p3[:, -1][None],
            s_s.reshape(1, bd, H_RWKV, RWKV_HEAD, RWKV_HEAD), rw_s[None])
```

```python
import functools

import jax
import jax.numpy as jnp
from jax import lax
from jax.experimental import pallas as pl
from jax.experimental.pallas import tpu as pltpu

F32 = jnp.float32
BF16 = jnp.bfloat16
HIGHEST = lax.Precision.HIGHEST

D_MODEL = 1024
PAGE_SIZE = 128
ATT_WIDTH = 512
RWKV_WIDTH = 512
HEAD_DIM = 64
H_ATT = 8
MOBA_BLOCK = 256
MOBA_TOPK = 3
RWKV_HEAD = 64
H_RWKV = 8
W_LORA = 64
A_LORA = 64
G_LORA = 128
RWKV_COLS = 3 * RWKV_WIDTH + W_LORA + A_LORA + G_LORA
PROJ_COLS = 3 * ATT_WIDTH + RWKV_COLS
N_MEM = 256
MEM_HEADS = 4
MEM_HEAD_DIM = 256
D_FF = 4 * D_MODEL
LN_EPS = 1e-5
GN_EPS = 64e-5
DEPTH = 1
ALPHA = (2.0 * DEPTH) ** 0.25

RWKV_CHUNK = 64
NEG = -1e30
VMEM_LIMIT = 56 * 1024 * 1024

_NT = (((1,), (1,)), ((), ()))
_TN = (((0,), (0,)), ((), ()))


def _cparams(*sem):
    return pltpu.CompilerParams(dimension_semantics=sem, vmem_limit_bytes=VMEM_LIMIT)


def _dot(a, b, dims=None, precision=None):
    if dims is None:
        return jnp.dot(a, b, preferred_element_type=F32, precision=precision)
    return lax.dot_general(a, b, dims, preferred_element_type=F32, precision=precision)


def _bdot(a, b, dims=None):
    return _dot(a.astype(BF16), b.astype(BF16), dims)


def _layer_norm(z, g, b):
    mu = jnp.mean(z, axis=-1, keepdims=True)
    d = z - mu
    var = jnp.mean(d * d, axis=-1, keepdims=True)
    return d * lax.rsqrt(var + LN_EPS) * g + b


def _full(shape):
    n = len(shape)
    return pl.BlockSpec(shape, lambda *_: (0,) * n)


def _col_chunks(n, width=512):
    out, c = [], 0
    while c < n:
        w = min(width, n - c)
        out.append((c, w))
        c += w
    return out


def _mm_multi_body(x_ref, w_ref, *o_refs, col0s):
    xb = x_ref[...].astype(BF16)
    for o_ref, c0 in zip(o_refs, col0s):
        for c, w in _col_chunks(o_ref.shape[1]):
            o_ref[:, c:c + w] = _dot(xb, w_ref[:, c0 + c:c0 + c + w])


def _mm_f32_body(x_ref, w_ref, o_ref):
    o_ref[...] = _dot(x_ref[...], w_ref[...], precision=HIGHEST)


def _mm_f32(x, w, n_cols):
    m, k = x.shape
    return pl.pallas_call(
        _mm_f32_body,
        grid=(1,),
        in_specs=[_full(x.shape), pl.BlockSpec((k, n_cols), lambda i: (0, 0))],
        out_specs=_full((m, n_cols)),
        out_shape=jax.ShapeDtypeStruct((m, n_cols), F32),
        compiler_params=_cparams("arbitrary"),
    )(x, w)


def _mm_multi(x, w_bf16, widths, tm, first_col=0):
    m, k = x.shape
    col0s, c = [], first_col
    for wd in widths:
        col0s.append(c)
        c += wd
    assert c == w_bf16.shape[1] and m % tm == 0
    return pl.pallas_call(
        functools.partial(_mm_multi_body, col0s=tuple(col0s)),
        grid=(m // tm,),
        in_specs=[pl.BlockSpec((tm, k), lambda i: (i, 0)), _full(w_bf16.shape)],
        out_specs=[pl.BlockSpec((tm, wd), lambda i: (i, 0)) for wd in widths],
        out_shape=[jax.ShapeDtypeStruct((m, wd), F32) for wd in widths],
        compiler_params=_cparams("parallel"),
    )(x, w_bf16)


def _proj_prompt_body(x_ref, w_ref, qt_ref, kt_ref, vt_ref, kb_ref, rw_ref, kmean_ref):
    j = pl.program_id(1)
    tm = x_ref.shape[1]
    xb = x_ref[0].astype(BF16)
    aw = ATT_WIDTH
    q = _dot(xb, w_ref[:, 0:aw])
    k = _dot(xb, w_ref[:, aw:2 * aw])
    v = _dot(xb, w_ref[:, 2 * aw:3 * aw])
    qt_ref[0] = q.T
    kt_ref[0] = k.T
    vt_ref[0] = v.T
    kb_ref[0] = k.astype(BF16)
    for c, w in _col_chunks(RWKV_COLS):
        rw_ref[0, :, c:c + w] = _dot(xb, w_ref[:, 3 * aw + c:3 * aw + c + w])
    blk_per_tile = tm // MOBA_BLOCK
    for i in range(blk_per_tile):
        kmean_ref[0, pl.ds(j * blk_per_tile + i, 1), :] = jnp.mean(
            k[i * MOBA_BLOCK:(i + 1) * MOBA_BLOCK], axis=0, keepdims=True)


def _proj_prompt(x, w_bf16, tm):
    bsz, t, _ = x.shape
    assert t % tm == 0 and tm % MOBA_BLOCK == 0
    n_blk = t // MOBA_BLOCK
    tr_spec = pl.BlockSpec((1, ATT_WIDTH, tm), lambda i, j: (i, 0, j))
    tr_shape = jax.ShapeDtypeStruct((bsz, ATT_WIDTH, t), F32)
    return pl.pallas_call(
        _proj_prompt_body,
        grid=(bsz, t // tm),
        in_specs=[pl.BlockSpec((1, tm, D_MODEL), lambda i, j: (i, j, 0)), _full(w_bf16.shape)],
        out_specs=[tr_spec, tr_spec, tr_spec,
                   pl.BlockSpec((1, tm, ATT_WIDTH), lambda i, j: (i, j, 0)),
                   pl.BlockSpec((1, tm, RWKV_COLS), lambda i, j: (i, j, 0)),
                   pl.BlockSpec((1, n_blk, ATT_WIDTH), lambda i, j: (i, 0, 0))],
        out_shape=[tr_shape, tr_shape, tr_shape,
                   jax.ShapeDtypeStruct((bsz, t, ATT_WIDTH), BF16),
                   jax.ShapeDtypeStruct((bsz, t, RWKV_COLS), F32),
                   jax.ShapeDtypeStruct((bsz, n_blk, ATT_WIDTH), F32)],
        compiler_params=_cparams("parallel", "arbitrary"),
    )(x, w_bf16)


def _mm_res_ln_body(*refs, n_in):
    x_refs = refs[:n_in]
    w_refs = refs[n_in:2 * n_in]
    res_ref, g_ref, b_ref, o_ref = refs[2 * n_in:]
    acc = _dot(x_refs[0][...].astype(BF16), w_refs[0][...])
    for x_ref, w_ref in zip(x_refs[1:], w_refs[1:]):
        acc = acc + _dot(x_ref[...].astype(BF16), w_ref[...])
    o_ref[...] = _layer_norm(ALPHA * res_ref[...] + acc, g_ref[...], b_ref[...])


def _mm_res_ln(xs, ws, res, g, b, tm):
    m = res.shape[0]
    n_in = len(xs)
    assert m % tm == 0
    in_specs = [pl.BlockSpec((tm, x.shape[1]), lambda i: (i, 0)) for x in xs]
    in_specs += [_full(w.shape) for w in ws]
    in_specs += [pl.BlockSpec((tm, D_MODEL), lambda i: (i, 0)), _full((1, D_MODEL)), _full((1, D_MODEL))]
    return pl.pallas_call(
        functools.partial(_mm_res_ln_body, n_in=n_in),
        grid=(m // tm,),
        in_specs=in_specs,
        out_specs=pl.BlockSpec((tm, D_MODEL), lambda i: (i, 0)),
        out_shape=jax.ShapeDtypeStruct((m, D_MODEL), F32),
        compiler_params=_cparams("parallel"),
    )(*xs, *ws, res, g, b)


def _mlp_body(x_ref, w1_ref, w2_ref, g_ref, b_ref, o_ref):
    x = x_ref[...]
    xb = x.astype(BF16)
    acc = jnp.zeros(x.shape, F32)
    for c, w in _col_chunks(D_FF):
        h = jnp.maximum(_dot(xb, w1_ref[:, c:c + w]), 0.0)
        acc = acc + _dot((h * h).astype(BF16), w2_ref[c:c + w, :])
    o_ref[...] = _layer_norm(ALPHA * x + acc, g_ref[...], b_ref[...])


def _mlp(x, w1, w2, g, b, tm):
    m = x.shape[0]
    assert m % tm == 0
    return pl.pallas_call(
        _mlp_body,
        grid=(m // tm,),
        in_specs=[pl.BlockSpec((tm, D_MODEL), lambda i: (i, 0)), _full(w1.shape), _full(w2.shape),
                  _full((1, D_MODEL)), _full((1, D_MODEL))],
        out_specs=pl.BlockSpec((tm, D_MODEL), lambda i: (i, 0)),
        out_shape=jax.ShapeDtypeStruct((m, D_MODEL), F32),
        compiler_params=_cparams("parallel"),
    )(x, w1, w2, g, b)


def _mem_attend(q, k_head, v_head):
    heads = range(MEM_HEADS)
    s = [_bdot(q[:, h * MEM_HEAD_DIM:(h + 1) * MEM_HEAD_DIM], k_head(h), _NT) for h in heads]
    p = [jnp.exp(x - jnp.max(x, axis=-1, keepdims=True)) for x in s]
    l = [jnp.sum(x, axis=-1, keepdims=True) for x in p]
    return jnp.concatenate([_bdot(x, v_head(h)) / z for h, x, z in zip(heads, p, l)], axis=-1)


def _mix_memx_prompt_body(x_ref, ya_ref, yr_ref, mk_ref, mv_ref, wa_ref, wr_ref, wq_ref, wo_ref,
                          g1_ref, b1_ref, g2_ref, b2_ref, o_ref):
    mix = _dot(ya_ref[0].astype(BF16), wa_ref[...]) + _dot(yr_ref[0].astype(BF16), wr_ref[...])
    x1 = _layer_norm(ALPHA * x_ref[0] + mix, g1_ref[...], b1_ref[...])
    q = _dot(x1.astype(BF16), wq_ref[...]) * (MEM_HEAD_DIM ** -0.5)
    cols = lambda h: slice(h * MEM_HEAD_DIM, (h + 1) * MEM_HEAD_DIM)
    att = _mem_attend(q, lambda h: mk_ref[0, :, cols(h)], lambda h: mv_ref[0, :, cols(h)])
    acc = _dot(att.astype(BF16), wo_ref[...])
    o_ref[0] = _layer_norm(ALPHA * x1 + acc, g2_ref[...], b2_ref[...])


def _mix_memx_prompt(x, y_att, y_rw, mk, mv, w_att, w_rw, wq, wo, ln1, ln2, tm):
    bsz, t, _ = x.shape
    assert t % tm == 0
    row = lambda w: pl.BlockSpec((1, tm, w), lambda i, j: (i, j, 0))
    mem = pl.BlockSpec((1, N_MEM, D_MODEL), lambda i, j: (i, 0, 0))
    vec = _full((1, D_MODEL))
    return pl.pallas_call(
        _mix_memx_prompt_body,
        grid=(bsz, t // tm),
        in_specs=[row(D_MODEL), row(ATT_WIDTH), row(RWKV_WIDTH), mem, mem,
                  _full(w_att.shape), _full(w_rw.shape), _full(wq.shape), _full(wo.shape), vec, vec, vec, vec],
        out_specs=row(D_MODEL),
        out_shape=jax.ShapeDtypeStruct(x.shape, F32),
        compiler_params=_cparams("parallel", "parallel"),
    )(x, y_att, y_rw, mk, mv, w_att, w_rw, wq, wo, *ln1, *ln2)


def _memx_sample_body(q_ref, mk_ref, mv_ref, o_ref):
    n_rows = N_MEM * MEM_HEADS
    k_all = mk_ref[0].reshape(n_rows, MEM_HEAD_DIM)
    v_all = mv_ref[0].reshape(n_rows, MEM_HEAD_DIM)
    q = q_ref[0] * (MEM_HEAD_DIM ** -0.5)
    q_rows = [q[:, h * MEM_HEAD_DIM:(h + 1) * MEM_HEAD_DIM] for h in range(MEM_HEADS)]
    q8 = jnp.concatenate(q_rows + [jnp.zeros((8 - MEM_HEADS, MEM_HEAD_DIM), F32)], axis=0)
    s = _bdot(q8, k_all, _NT)
    row = lax.broadcasted_iota(jnp.int32, s.shape, 0)
    own = (lax.broadcasted_iota(jnp.int32, s.shape, 1) % MEM_HEADS) == row
    s = jnp.where(own, s, NEG)
    p = jnp.where(own, jnp.exp(s - jnp.max(s, axis=-1, keepdims=True)), 0.0)
    l = jnp.sum(p, axis=-1, keepdims=True)
    o8 = _bdot(p, v_all) / jnp.where(l > 0.0, l, 1.0)
    o_ref[0] = jnp.concatenate([o8[h:h + 1] for h in range(MEM_HEADS)], axis=-1)


def _memx_sample(q, mk, mv):
    bd = q.shape[0]
    mem_spec = pl.BlockSpec((1, N_MEM, MEM_HEADS, MEM_HEAD_DIM), lambda i: (i, 0, 0, 0))
    return pl.pallas_call(
        _memx_sample_body,
        grid=(bd,),
        in_specs=[pl.BlockSpec((1, 1, D_MODEL), lambda i: (i, 0, 0)), mem_spec, mem_spec],
        out_specs=pl.BlockSpec((1, 1, D_MODEL), lambda i: (i, 0, 0)),
        out_shape=jax.ShapeDtypeStruct((bd, 1, D_MODEL), F32),
        compiler_params=_cparams("parallel"),
    )(q, mk, mv)


def _top3_rows(g, n_iota, n_lim):
    g = jnp.where(n_iota < n_lim, g, -jnp.inf)
    n_rows = g.shape[0]
    sel = n_iota < 0
    for _ in range(MOBA_TOPK):
        mx = jnp.max(g, axis=0, keepdims=True)
        idx = jnp.min(jnp.where(g == mx, n_iota, n_rows), axis=0, keepdims=True)
        hit = n_iota == idx
        sel = sel | hit
        g = jnp.where(hit, -jnp.inf, g)
    return sel & (n_iota < n_lim)


MOBA_QT = 256
MASK_BIG = 1e30
FEAT_ROWS = 8


def _moba_prompt_body(qt_ref, kb_ref, vt_ref, kmean_ref, o_ref, vt_s, fe_s, fo_s, acc_s, *, n_blk):
    c = pl.program_id(1)
    half = HEAD_DIM
    lane_k = lax.broadcasted_iota(jnp.int32, (MOBA_BLOCK, 2 * half), 1)
    row_k = lax.broadcasted_iota(jnp.int32, (MOBA_BLOCK, 2 * half), 0)

    @pl.when(c == 0)
    def _():
        for n in range(n_blk):
            vt_s[n] = vt_ref[0, :, n * MOBA_BLOCK:(n + 1) * MOBA_BLOCK].astype(BF16)
            for tbl, f in ((fe_s, lane_k - half), (fo_s, lane_k)):
                feat = jnp.where((f == 0) | (f == 1) | (f == FEAT_ROWS + n), 1.0, 0.0)
                feat = jnp.where(f == 2, row_k.astype(F32), feat)
                feat = jnp.where(f == 3, float(n * MOBA_BLOCK), feat)
                tbl[n] = feat.astype(BF16)

    heads = [slice(h * HEAD_DIM, (h + 1) * HEAD_DIM) for h in range(H_ATT)]
    q_blk = (c * MOBA_QT) // MOBA_BLOCK
    qt = qt_ref[0]
    n_iota = lax.broadcasted_iota(jnp.int32, (n_blk, MOBA_QT), 0)
    gates = [_dot(kmean_ref[0, :, hs], qt[hs, :], precision=HIGHEST) for hs in heads]
    sel = [_top3_rows(g, n_iota, q_blk) | (n_iota == q_blk) for g in gates]
    acc_s[...] = jnp.zeros(acc_s.shape, F32)

    r8 = lax.broadcasted_iota(jnp.int32, (FEAT_ROWS, MOBA_QT), 0)
    q_rel = lax.broadcasted_iota(jnp.int32, (FEAT_ROWS, MOBA_QT), 1).astype(F32)
    chunk_start = (c * MOBA_QT).astype(F32)
    pad = jnp.zeros((half - FEAT_ROWS - n_blk, MOBA_QT), F32)
    q_aug = []
    for h, hs in enumerate(heads):
        slope = 2.0 ** -(h + 1)
        base = jnp.where(r8 == 0, -slope * chunk_start, 0.0)
        base = jnp.where(r8 == 1, -slope * q_rel, base)
        base = jnp.where((r8 == 2) | (r8 == 3), slope, base)
        feat = jnp.concatenate([base, (sel[h].astype(F32) - 1.0) * MASK_BIG, pad], axis=0)
        q_h = qt[hs, :] * (HEAD_DIM ** -0.5)
        q_aug.append(jnp.concatenate([q_h, feat] if h % 2 == 0 else [feat, q_h], axis=0).astype(BF16))

    left = lane_k < half
    causal = (lax.broadcasted_iota(jnp.int32, (MOBA_BLOCK, MOBA_QT), 0)
              <= lax.broadcasted_iota(jnp.int32, (MOBA_BLOCK, MOBA_QT), 1) + (c * MOBA_QT - q_blk * MOBA_BLOCK))
    head_row = lax.broadcasted_iota(jnp.int32, (H_ATT, MOBA_QT), 0)

    def attend(n, m_all, l_all, own):
        rows = pl.ds(pl.multiple_of(n * MOBA_BLOCK, MOBA_BLOCK), MOBA_BLOCK)
        f_even, f_odd = fe_s[n], fo_s[n]
        vblk = vt_s[n]
        k_aug = []
        for j in range(H_ATT // 2):
            k_pair = kb_ref[0, rows, j * 2 * half:(j + 1) * 2 * half]
            k_aug += [jnp.where(left, k_pair, f_even), jnp.where(left, f_odd, k_pair)]
        s = [_dot(k, q) for k, q in zip(k_aug, q_aug)]
        if own:
            s = [jnp.where(causal, x, NEG) for x in s]
        m_old = [m_all[h:h + 1, :] for h in range(H_ATT)]
        m_new = [jnp.maximum(mo, jnp.max(x, axis=0, keepdims=True)) for mo, x in zip(m_old, s)]
        p = [jnp.exp(x - mn) for x, mn in zip(s, m_new)]
        corr = [jnp.exp(mo - mn) for mo, mn in zip(m_old, m_new)]
        pv = [_dot(vblk[hs, :], x.astype(BF16)) for hs, x in zip(heads, p)]
        for h, hs in enumerate(heads):
            acc_s[hs, :] = acc_s[hs, :] * corr[h] + pv[h]
            l_h = l_all[h:h + 1, :] * corr[h] + jnp.sum(p[h], axis=0, keepdims=True)
            l_all = jnp.where(head_row == h, l_h, l_all)
            m_all = jnp.where(head_row == h, m_new[h], m_all)
        return m_all, l_all

    m0 = jnp.full((H_ATT, MOBA_QT), NEG, F32)
    l0 = jnp.zeros((H_ATT, MOBA_QT), F32)
    m_all, l_all = lax.fori_loop(0, q_blk, lambda n, carry: attend(n, *carry, False), (m0, l0))
    m_all, l_all = attend(q_blk, m_all, l_all, True)
    for h, hs in enumerate(heads):
        acc_s[hs, :] = acc_s[hs, :] / l_all[h:h + 1, :]
    o_ref[0] = acc_s[...].T


def _moba_prompt(qt, kb, vt, kmean):
    bsz, t, _ = kb.shape
    assert t % MOBA_BLOCK == 0 and MOBA_BLOCK % MOBA_QT == 0
    n_blk = t // MOBA_BLOCK
    assert n_blk % 8 == 0 and FEAT_ROWS + n_blk <= HEAD_DIM and H_ATT % 2 == 0
    return pl.pallas_call(
        functools.partial(_moba_prompt_body, n_blk=n_blk),
        grid=(bsz, t // MOBA_QT),
        in_specs=[pl.BlockSpec((1, ATT_WIDTH, MOBA_QT), lambda i, j: (i, 0, j)),
                  pl.BlockSpec((1, t, ATT_WIDTH), lambda i, j: (i, 0, 0)),
                  pl.BlockSpec((1, ATT_WIDTH, t), lambda i, j: (i, 0, 0)),
                  pl.BlockSpec((1, n_blk, ATT_WIDTH), lambda i, j: (i, 0, 0))],
        out_specs=pl.BlockSpec((1, MOBA_QT, ATT_WIDTH), lambda i, j: (i, j, 0)),
        out_shape=jax.ShapeDtypeStruct((bsz, t, ATT_WIDTH), F32),
        scratch_shapes=[pltpu.VMEM((n_blk, ATT_WIDTH, MOBA_BLOCK), BF16),
                        pltpu.VMEM((n_blk, MOBA_BLOCK, 2 * HEAD_DIM), BF16),
                        pltpu.VMEM((n_blk, MOBA_BLOCK, 2 * HEAD_DIM), BF16),
                        pltpu.VMEM((ATT_WIDTH, MOBA_QT), F32)],
        compiler_params=_cparams("parallel", "arbitrary"),
    )(qt, kb, vt, kmean)


PAGES_PER_STEP = 32
PAGES_PER_BLOCK = MOBA_BLOCK // PAGE_SIZE
N_SEL_PAGES = MOBA_TOPK * PAGES_PER_BLOCK


def _moba_scores_body(pt_ref, q_ref, *refs, n_pages):
    page_refs = refs[:PAGES_PER_STEP]
    s_ref, top_ref, qcol_s, gate_s = refs[PAGES_PER_STEP:]
    g = pl.program_id(1)
    n_blk = n_pages // PAGES_PER_BLOCK
    blk_per_step = PAGES_PER_STEP // PAGES_PER_BLOCK

    @pl.when(g == 0)
    def _():
        for h in range(H_ATT):
            qcol_s[h] = jnp.broadcast_to(q_ref[0, h:h + 1, :], (PAGE_SIZE, HEAD_DIM)).T
        gate_s[...] = jnp.zeros(gate_s.shape, F32)

    lane = lax.broadcasted_iota(jnp.int32, (H_ATT, 128), 1)
    for i in range(blk_per_step):
        blk_sum = jnp.zeros((H_ATT, PAGE_SIZE), F32)
        for jj in range(PAGES_PER_BLOCK):
            p = i * PAGES_PER_BLOCK + jj
            s = jnp.sum(page_refs[p][0] * qcol_s[...], axis=1)
            s_ref[0, p] = s
            blk_sum = blk_sum + s
        gate = jnp.sum(blk_sum, axis=1, keepdims=True) * (1.0 / MOBA_BLOCK)
        gate_s[...] = jnp.where(lane == g * blk_per_step + i, gate, gate_s[...])

    @pl.when(g == pl.num_programs(1) - 1)
    def _():
        g_work = jnp.where(lane < n_blk, gate_s[...], -jnp.inf)
        out = jnp.zeros((H_ATT, 128), jnp.int32)
        for r in range(MOBA_TOPK):
            mx = jnp.max(g_work, axis=1, keepdims=True)
            idx = jnp.min(jnp.where(g_work == mx, lane, 128), axis=1, keepdims=True)
            out = jnp.where(lane == r, idx, out)
            g_work = jnp.where(lane == idx, -jnp.inf, g_work)
        top_ref[0] = out


def _moba_scores(q, kt_pool, page_table):
    bd, n_pages = page_table.shape
    assert n_pages % PAGES_PER_STEP == 0 and PAGES_PER_STEP % PAGES_PER_BLOCK == 0
    n_blk = n_pages // PAGES_PER_BLOCK
    assert MOBA_TOPK <= n_blk <= 128

    def page_spec(i):
        return pl.BlockSpec((1, H_ATT, HEAD_DIM, PAGE_SIZE),
                            lambda b, g, pt: (pt[b * n_pages + g * PAGES_PER_STEP + i], 0, 0, 0))

    grid_spec = pltpu.PrefetchScalarGridSpec(
        num_scalar_prefetch=1,
        grid=(bd, n_pages // PAGES_PER_STEP),
        in_specs=[pl.BlockSpec((1, H_ATT, HEAD_DIM), lambda b, g, pt: (b, 0, 0))]
        + [page_spec(i) for i in range(PAGES_PER_STEP)],
        out_specs=[pl.BlockSpec((1, PAGES_PER_STEP, H_ATT, PAGE_SIZE), lambda b, g, pt: (b, g, 0, 0)),
                   pl.BlockSpec((1, H_ATT, 128), lambda b, g, pt: (b, 0, 0))],
        scratch_shapes=[pltpu.VMEM((H_ATT, HEAD_DIM, PAGE_SIZE), F32), pltpu.VMEM((H_ATT, 128), F32)],
    )
    return pl.pallas_call(
        functools.partial(_moba_scores_body, n_pages=n_pages),
        grid_spec=grid_spec,
        out_shape=[jax.ShapeDtypeStruct((bd, n_pages, H_ATT, PAGE_SIZE), F32),
                   jax.ShapeDtypeStruct((bd, H_ATT, 128), jnp.int32)],
        compiler_params=_cparams("parallel", "arbitrary"),
    )(page_table.reshape(-1), q, *([kt_pool] * PAGES_PER_STEP))


def _moba_sample_body(pt_ref, top_ref, q_ref, kn_ref, vn_ref, s_ref, vt_hbm, o_ref, vbuf, sem, *, past, n_pages):
    b = pl.program_id(0)
    units = [(h, j) for h in range(H_ATT) for j in range(N_SEL_PAGES)]

    def seq_page(seq, h, j):
        return top_ref[(seq * H_ATT + h) * MOBA_TOPK + j // PAGES_PER_BLOCK] * PAGES_PER_BLOCK + j % PAGES_PER_BLOCK

    def tile_copies(seq, slot):
        return [pltpu.make_async_copy(vt_hbm.at[pt_ref[seq * n_pages + seq_page(seq, h, j)], h],
                                      vbuf.at[slot, h * N_SEL_PAGES + j], sem.at[slot]) for h, j in units]

    @pl.when(b == 0)
    def _():
        for cp in tile_copies(0, 0):
            cp.start()

    @pl.when(b + 1 < pl.num_programs(0))
    def _():
        for cp in tile_copies(b + 1, (b + 1) % 2):
            cp.start()

    slot = b % 2
    scale = HEAD_DIM ** -0.5
    lane = lax.broadcasted_iota(jnp.int32, (1, PAGE_SIZE), 1)
    logits = []
    for h, j in units:
        page = seq_page(b, h, j)
        dist = (past - (page * PAGE_SIZE + lane)).astype(F32)
        logits.append(s_ref[0, page, h:h + 1, :] * scale - (2.0 ** -(h + 1)) * dist)
    s_self = jnp.sum(q_ref[0] * kn_ref[0], axis=-1, keepdims=True) * scale
    m, p_self = [], []
    for h in range(H_ATT):
        m_h = s_self[h:h + 1]
        for s in logits[h * N_SEL_PAGES:(h + 1) * N_SEL_PAGES]:
            m_h = jnp.maximum(m_h, jnp.max(s, axis=-1, keepdims=True))
        m.append(m_h)
        p_self.append(jnp.exp(s_self[h:h + 1] - m_h))
    p = [jnp.exp(s - m[h]) for s, (h, j) in zip(logits, units)]
    for cp in tile_copies(b, slot):
        cp.wait()
    pv = [_bdot(jnp.broadcast_to(x, (8, PAGE_SIZE)), vbuf[slot, h * N_SEL_PAGES + j], _NT)[0:1]
          for x, (h, j) in zip(p, units)]
    for h in range(H_ATT):
        sel = range(h * N_SEL_PAGES, (h + 1) * N_SEL_PAGES)
        l = p_self[h] + sum(jnp.sum(p[i], axis=-1, keepdims=True) for i in sel)
        acc = p_self[h] * vn_ref[0, h:h + 1, :] + sum(pv[i] for i in sel)
        o_ref[0, h] = acc / l


def _moba_sample(q, k_new, v_new, scores, vt_pool, page_table, top):
    bd, n_pages = page_table.shape
    past = n_pages * PAGE_SIZE
    assert past % MOBA_BLOCK == 0

    def tok_spec():
        return pl.BlockSpec((1, H_ATT, HEAD_DIM), lambda b, pt, tp: (b, 0, 0))

    grid_spec = pltpu.PrefetchScalarGridSpec(
        num_scalar_prefetch=2,
        grid=(bd,),
        in_specs=[tok_spec(), tok_spec(), tok_spec(),
                  pl.BlockSpec((1, n_pages, H_ATT, PAGE_SIZE), lambda b, pt, tp: (b, 0, 0, 0)),
                  pl.BlockSpec(memory_space=pl.ANY)],
        out_specs=pl.BlockSpec((1, H_ATT, 1, HEAD_DIM), lambda b, pt, tp: (b, 0, 0, 0)),
        scratch_shapes=[pltpu.VMEM((2, H_ATT * N_SEL_PAGES, HEAD_DIM, PAGE_SIZE), F32),
                        pltpu.SemaphoreType.DMA((2,))],
    )
    return pl.pallas_call(
        functools.partial(_moba_sample_body, past=past, n_pages=n_pages),
        grid_spec=grid_spec,
        out_shape=jax.ShapeDtypeStruct((bd, H_ATT, 1, HEAD_DIM), F32),
        compiler_params=_cparams("arbitrary"),
    )(page_table.reshape(-1), top.reshape(-1), q, k_new, v_new, scores, vt_pool)


def _softplus(z):
    return jnp.maximum(z, 0.0) + jnp.log1p(jnp.exp(-jnp.abs(z)))


def _sigmoid(z):
    return 1.0 / (1.0 + jnp.exp(-z))


def _rwkv_token_terms(m, w0, w_up, a0, a_up, g_up, k_k, k_a):
    rw = RWKV_WIDTH
    r, k, v = m[:, 0:rw], m[:, rw:2 * rw], m[:, 2 * rw:3 * rw]
    xw = m[:, 3 * rw:3 * rw + W_LORA]
    xa = m[:, 3 * rw + W_LORA:3 * rw + W_LORA + A_LORA]
    xg = m[:, 3 * rw + W_LORA + A_LORA:]
    w_log = -_softplus(-(w0 + _dot(jnp.tanh(xw), w_up, precision=HIGHEST))) - 0.5
    log_decay = -jnp.exp(w_log)
    a = _sigmoid(a0 + _dot(xa, a_up, precision=HIGHEST))
    g = _bdot(_sigmoid(xg), g_up)
    kk = k * k_k
    k2 = k * (1.0 + (a - 1.0) * k_a)
    return r, log_decay, k2, v, kk, a, g


def _head_norm(kk_h):
    return kk_h / jnp.maximum(jnp.sqrt(jnp.sum(kk_h * kk_h, axis=-1, keepdims=True)), 1e-12)


def _rwkv_head_out(y, r_h, k_h, v_h, g_h, rk_h, gng_h, gnb_h):
    mu = jnp.mean(y, axis=-1, keepdims=True)
    d = y - mu
    var = jnp.mean(d * d, axis=-1, keepdims=True)
    yn = d * lax.rsqrt(var + GN_EPS) * gng_h + gnb_h
    bonus = jnp.sum(r_h * k_h * rk_h, axis=-1, keepdims=True) * v_h
    return (yn + bonus) * g_h


def _rwkv_prompt_body(rw_ref, prev_ref, s0_ref, mu_ref, w0_ref, wup_ref, a0_ref, aup_ref, gup_ref,
                      kk_ref, ka_ref, rk_ref, gng_ref, gnb_ref, y_ref, sfin_ref, state_s, prev_s):
    c = pl.program_id(1)
    ch = RWKV_CHUNK

    @pl.when(c == 0)
    def _():
        state_s[...] = s0_ref[0]
        prev_s[...] = prev_ref[0]

    rw = rw_ref[0]
    row = lax.broadcasted_iota(jnp.int32, rw.shape, 0)
    rw_prev = jnp.where(row == 0, prev_s[...], pltpu.roll(rw, 1, 0))
    prev_s[...] = rw[ch - 1:ch, :]
    m = rw + (rw_prev - rw) * mu_ref[...]
    r, log_decay, k2, v, kk, a, g = _rwkv_token_terms(
        m, w0_ref[...], wup_ref[...], a0_ref[...], aup_ref[...], gup_ref[...], kk_ref[...], ka_ref[...])

    ti = lax.broadcasted_iota(jnp.int32, (ch, ch), 0)
    si = lax.broadcasted_iota(jnp.int32, (ch, ch), 1)
    lower = si <= ti
    strict = si < ti
    eye = (si == ti).astype(F32)
    cs = _dot(lower.astype(F32), log_decay, precision=HIGHEST)
    gam = jnp.exp(cs)
    gam_prev = jnp.exp(cs - log_decay)
    gam_inv = jnp.exp(-cs)
    gam_tail = jnp.exp(cs[ch - 1:ch, :] - cs)

    heads = [slice(h * RWKV_HEAD, (h + 1) * RWKV_HEAD) for h in range(H_RWKV)]
    kkn = [_head_norm(kk[:, hs]) for hs in heads]
    a_t = [-n * gam_prev[:, hs] for n, hs in zip(kkn, heads)]
    b_h = [n * a[:, hs] for n, hs in zip(kkn, heads)]
    b_t = [x * gam_inv[:, hs] for x, hs in zip(b_h, heads)]
    k_t = [k2[:, hs] * gam_inv[:, hs] for hs in heads]
    r_t = [r[:, hs] * gam[:, hs] for hs in heads]
    v_h = [v[:, hs] for hs in heads]
    s_0 = [state_s[hs, :] for hs in heads]
    l_ab = [jnp.where(strict, _bdot(x, z, _NT), 0.0) for x, z in zip(a_t, b_t)]
    l_ak = [jnp.where(strict, _bdot(x, z, _NT), 0.0) for x, z in zip(a_t, k_t)]
    m_rb = [jnp.where(lower, _bdot(x, z, _NT), 0.0) for x, z in zip(r_t, b_t)]
    m_rk = [jnp.where(lower, _bdot(x, z, _NT), 0.0) for x, z in zip(r_t, k_t)]
    inv = [eye + x for x in l_ab]
    pw = list(l_ab)
    span = 2
    while span < ch:
        pw = [_bdot(x, x) for x in pw]
        inv = [i + _bdot(i, x) for i, x in zip(inv, pw)]
        span *= 2
    w_ = [_bdot(x, s, _NT) + _bdot(l, vv) for x, s, l, vv in zip(a_t, s_0, l_ak, v_h)]
    u = [_bdot(i, x) for i, x in zip(inv, w_)]
    y = [_bdot(x, s, _NT) + _bdot(mb, uu) + _bdot(mk, vv)
         for x, s, mb, uu, mk, vv in zip(r_t, s_0, m_rb, u, m_rk, v_h)]
    s_new = [s * gam[ch - 1:ch, hs] + _bdot(uu, bb * gam_tail[:, hs], _TN) + _bdot(vv, k2[:, hs] * gam_tail[:, hs], _TN)
             for s, hs, uu, bb, vv in zip(s_0, heads, u, b_h, v_h)]
    for hs, sn, yy, vv in zip(heads, s_new, y, v_h):
        state_s[hs, :] = sn
        y_ref[0, :, hs] = _rwkv_head_out(yy, r[:, hs], k2[:, hs], vv, g[:, hs],
                                         rk_ref[:, hs], gng_ref[:, hs], gnb_ref[:, hs])

    @pl.when(c == pl.num_programs(1) - 1)
    def _():
        sfin_ref[0] = state_s[...]


def _rwkv_prompt(rw, prev_row, s0, p):
    bsz, t, _ = rw.shape
    assert t % RWKV_CHUNK == 0
    hn = H_RWKV * RWKV_HEAD
    vec = lambda n: _full((1, n))
    return pl.pallas_call(
        _rwkv_prompt_body,
        grid=(bsz, t // RWKV_CHUNK),
        in_specs=[pl.BlockSpec((1, RWKV_CHUNK, RWKV_COLS), lambda i, j: (i, j, 0)),
                  pl.BlockSpec((1, 1, RWKV_COLS), lambda i, j: (i, 0, 0)),
                  pl.BlockSpec((1, hn, RWKV_HEAD), lambda i, j: (i, 0, 0)),
                  vec(RWKV_COLS), vec(RWKV_WIDTH), _full((W_LORA, RWKV_WIDTH)), vec(RWKV_WIDTH),
                  _full((A_LORA, RWKV_WIDTH)), _full((G_LORA, RWKV_WIDTH)),
                  vec(RWKV_WIDTH), vec(RWKV_WIDTH), vec(RWKV_WIDTH), vec(RWKV_WIDTH), vec(RWKV_WIDTH)],
        out_specs=[pl.BlockSpec((1, RWKV_CHUNK, RWKV_WIDTH), lambda i, j: (i, j, 0)),
                   pl.BlockSpec((1, hn, RWKV_HEAD), lambda i, j: (i, 0, 0))],
        out_shape=[jax.ShapeDtypeStruct((bsz, t, RWKV_WIDTH), F32),
                   jax.ShapeDtypeStruct((bsz, hn, RWKV_HEAD), F32)],
        scratch_shapes=[pltpu.VMEM((hn, RWKV_HEAD), F32), pltpu.VMEM((1, RWKV_COLS), F32)],
        compiler_params=_cparams("parallel", "arbitrary"),
    )(rw, prev_row, s0, p["mu_shift"], p["w0"], p["w_up"], p["a0"], p["a_up"], p["g_up"],
      p["k_k"], p["k_a"], p["r_k"], p["gn_g"], p["gn_b"])


def _rows8(*rows):
    ri = lax.broadcasted_iota(jnp.int32, (8, rows[0].shape[1]), 0)
    out = jnp.zeros((8, rows[0].shape[1]), F32)
    for i, x in enumerate(rows):
        out = jnp.where(ri == i, x, out)
    return out


def _rwkv_step_body(rw_ref, prev_ref, s0_ref, mu_ref, w0_ref, wup_ref, a0_ref, aup_ref, gup_ref,
                    kk_ref, ka_ref, rk_ref, gng_ref, gnb_ref, y_ref, sfin_ref):
    rw = rw_ref[0]
    m = rw + (prev_ref[0] - rw) * mu_ref[...]
    m8 = jnp.broadcast_to(m, (8, RWKV_COLS))
    r, log_decay, k2, v, kk, a, g = _rwkv_token_terms(
        m8, w0_ref[...], wup_ref[...], a0_ref[...], aup_ref[...], gup_ref[...], kk_ref[...], ka_ref[...])
    decay = jnp.exp(log_decay)
    heads = [slice(h * RWKV_HEAD, (h + 1) * RWKV_HEAD) for h in range(H_RWKV)]
    kkn = [_head_norm(kk[0:1, hs]) for hs in heads]
    b_h = [n * a[0:1, hs] for n, hs in zip(kkn, heads)]
    s_0 = [s0_ref[0, hs, :] for hs in heads]
    sa = [_dot(_rows8(-n), s, _NT, HIGHEST)[0:1] for n, s in zip(kkn, s_0)]
    s_1 = [s * decay[0:1, hs] + _dot(_rows8(x, v[0:1, hs]), _rows8(bb, k2[0:1, hs]), _TN, HIGHEST)
           for s, hs, x, bb in zip(s_0, heads, sa, b_h)]
    y = [_dot(_rows8(r[0:1, hs]), s, _NT, HIGHEST)[0:1] for hs, s in zip(heads, s_1)]
    for hs, s, yy in zip(heads, s_1, y):
        sfin_ref[0, hs, :] = s
        y_ref[0, :, hs] = _rwkv_head_out(yy, r[0:1, hs], k2[0:1, hs], v[0:1, hs], g[0:1, hs],
                                         rk_ref[:, hs], gng_ref[:, hs], gnb_ref[:, hs])


def _rwkv_step(rw, prev_row, s0, p):
    bd = rw.shape[0]
    hn = H_RWKV * RWKV_HEAD
    vec = lambda n: _full((1, n))
    return pl.pallas_call(
        _rwkv_step_body,
        grid=(bd,),
        in_specs=[pl.BlockSpec((1, 1, RWKV_COLS), lambda i: (i, 0, 0)),
                  pl.BlockSpec((1, 1, RWKV_COLS), lambda i: (i, 0, 0)),
                  pl.BlockSpec((1, hn, RWKV_HEAD), lambda i: (i, 0, 0)),
                  vec(RWKV_COLS), vec(RWKV_WIDTH), _full((W_LORA, RWKV_WIDTH)), vec(RWKV_WIDTH),
                  _full((A_LORA, RWKV_WIDTH)), _full((G_LORA, RWKV_WIDTH)),
                  vec(RWKV_WIDTH), vec(RWKV_WIDTH), vec(RWKV_WIDTH), vec(RWKV_WIDTH), vec(RWKV_WIDTH)],
        out_specs=[pl.BlockSpec((1, 1, RWKV_WIDTH), lambda i: (i, 0, 0)),
                   pl.BlockSpec((1, hn, RWKV_HEAD), lambda i: (i, 0, 0))],
        out_shape=[jax.ShapeDtypeStruct((bd, 1, RWKV_WIDTH), F32),
                   jax.ShapeDtypeStruct((bd, hn, RWKV_HEAD), F32)],
        compiler_params=_cparams("parallel"),
    )(rw, prev_row, s0, p["mu_shift"], p["w0"], p["w_up"], p["a0"], p["a_up"], p["g_up"],
      p["k_k"], p["k_a"], p["r_k"], p["gn_g"], p["gn_b"])


def _row(x):
    return x.reshape(1, -1)


def kernel(x_prompt, x_sample, mem_prompt, cache_k_pool, cache_v_pool, page_table, cache_mem_k, cache_mem_v, state_wkv, state_shift, w_in, mu_shift, w0, w_up, a0, a_up, g_up, k_k, k_a, r_k, gn_g, gn_b, w_out, ln1_g, ln1_b, w_q_mem, w_kv_mem, w_o_mem, ln2_g, ln2_b, w_ff1, w_ff2, ln3_g, ln3_b):
    assert w_in.shape[0] == DEPTH == 1
    bsz, t, _ = x_prompt.shape
    bd = x_sample.shape[0]
    hn = H_RWKV * RWKV_HEAD

    w_in_b = w_in[0].astype(BF16)
    w_out_b = w_out[0].astype(BF16)
    wq_b = w_q_mem[0].astype(BF16)
    wkv_b = w_kv_mem[0].astype(BF16)
    wo_b = w_o_mem[0].astype(BF16)
    w1_b = w_ff1[0].astype(BF16)
    w2_b = w_ff2[0].astype(BF16)
    ln1 = (_row(ln1_g[0]), _row(ln1_b[0]))
    ln2 = (_row(ln2_g[0]), _row(ln2_b[0]))
    ln3 = (_row(ln3_g[0]), _row(ln3_b[0]))
    rwkv_p = dict(mu_shift=_row(mu_shift[0]), w0=_row(w0[0]), w_up=w_up[0], a0=_row(a0[0]), a_up=a_up[0],
                  g_up=g_up[0], k_k=_row(k_k[0]), k_a=_row(k_a[0]), r_k=_row(r_k[0]),
                  gn_g=_row(gn_g[0]), gn_b=_row(gn_b[0]))

    qt_p, kt_p, vt_p, kb_p, rw_p3, kmean_p = _proj_prompt(x_prompt, w_in_b, tm=512)
    y_att_p = _moba_prompt(qt_p, kb_p, vt_p, kmean_p)
    y_rw_p, s_p = _rwkv_prompt(rw_p3, jnp.zeros((bsz, 1, RWKV_COLS), F32), jnp.zeros((bsz, hn, RWKV_HEAD), F32), rwkv_p)
    mk_p, mv_p = _mm_multi(mem_prompt.reshape(bsz * N_MEM, D_MODEL), wkv_b, (D_MODEL, D_MODEL), tm=512)
    x2_p = _mix_memx_prompt(x_prompt, y_att_p, y_rw_p, mk_p.reshape(bsz, N_MEM, D_MODEL),
                            mv_p.reshape(bsz, N_MEM, D_MODEL), w_out_b[:ATT_WIDTH], w_out_b[ATT_WIDTH:],
                            wq_b, wo_b, ln1, ln2, tm=256)
    y_p = _mlp(x2_p.reshape(bsz * t, D_MODEL), w1_b, w2_b, *ln3, tm=256)

    kt_pool = cache_k_pool[0].transpose(0, 2, 3, 1)
    vt_pool = cache_v_pool[0].transpose(0, 2, 3, 1)
    xs = x_sample.reshape(bd, D_MODEL)
    q_s = _mm_f32(xs, w_in[0], ATT_WIDTH)
    k_s, v_s, rw_s = _mm_multi(xs, w_in_b, (ATT_WIDTH, ATT_WIDTH, RWKV_COLS), tm=bd, first_col=ATT_WIDTH)
    q_s3, k_s3, v_s3 = (z.reshape(bd, H_ATT, HEAD_DIM) for z in (q_s, k_s, v_s))
    scores_s, top = _moba_scores(q_s3, kt_pool, page_table)
    y_att_s = _moba_sample(q_s3, k_s3, v_s3, scores_s, vt_pool, page_table, top[:, :, :MOBA_TOPK])
    y_rw_s, s_s = _rwkv_step(rw_s.reshape(bd, 1, RWKV_COLS), state_shift[0].reshape(bd, 1, RWKV_COLS),
                             state_wkv[0].reshape(bd, hn, RWKV_HEAD), rwkv_p)
    x1_s = _mm_res_ln([y_att_s.reshape(bd, ATT_WIDTH), y_rw_s.reshape(bd, RWKV_WIDTH)],
                      [w_out_b[:ATT_WIDTH], w_out_b[ATT_WIDTH:]], xs, *ln1, tm=bd)
    (qm_s,) = _mm_multi(x1_s, wq_b, (D_MODEL,), tm=bd)
    att_s = _memx_sample(qm_s.reshape(bd, 1, D_MODEL), cache_mem_k[0], cache_mem_v[0])
    x2_s = _mm_res_ln([att_s.reshape(bd, D_MODEL)], [wo_b], x1_s, *ln2, tm=bd)
    y_s = _mlp(x2_s, w1_b, w2_b, *ln3, tm=bd)

    return (y_p.reshape(bsz, t, D_MODEL), y_s.reshape(bd, 1, D_MODEL),
            kt_p.reshape(bsz, H_ATT, HEAD_DIM, t).transpose(0, 3, 1, 2)[None],
            vt_p.reshape(bsz, H_ATT, HEAD_DIM, t).transpose(0, 3, 1, 2)[None],
            k_s.reshape(1, bd, 1, H_ATT, HEAD_DIM), v_s.reshape(1, bd, 1, H_ATT, HEAD_DIM),
            mk_p.reshape(1, bsz, N_MEM, MEM_HEADS, MEM_HEAD_DIM), mv_p.reshape(1, bsz, N_MEM, MEM_HEADS, MEM_HEAD_DIM),
            s_p.reshape(1, bsz, H_RWKV, RWKV_HEAD, RWKV_HEAD), rw_p3[:, -1][None],
            s_s.reshape(1, bd, H_RWKV, RWKV_HEAD, RWKV_HEAD), rw_s[None])
```

```python
import functools

import jax
import jax.numpy as jnp
from jax import lax
from jax.experimental import pallas as pl
from jax.experimental.pallas import tpu as pltpu

F32 = jnp.float32
BF16 = jnp.bfloat16
HIGHEST = lax.Precision.HIGHEST

D_MODEL = 1024
PAGE_SIZE = 128
ATT_WIDTH = 512
RWKV_WIDTH = 512
HEAD_DIM = 64
H_ATT = 8
MOBA_BLOCK = 256
MOBA_TOPK = 3
RWKV_HEAD = 64
H_RWKV = 8
W_LORA = 64
A_LORA = 64
G_LORA = 128
RWKV_COLS = 3 * RWKV_WIDTH + W_LORA + A_LORA + G_LORA
PROJ_COLS = 3 * ATT_WIDTH + RWKV_COLS
N_MEM = 256
MEM_HEADS = 4
MEM_HEAD_DIM = 256
D_FF = 4 * D_MODEL
LN_EPS = 1e-5
GN_EPS = 64e-5
DEPTH = 1
ALPHA = (2.0 * DEPTH) ** 0.25

RWKV_TILE = 256
RWKV_CHUNK = 64
NEG = -1e30
VMEM_LIMIT = 56 * 1024 * 1024

_NT = (((1,), (1,)), ((), ()))
_TN = (((0,), (0,)), ((), ()))


def _cparams(*sem):
    return pltpu.CompilerParams(dimension_semantics=sem, vmem_limit_bytes=VMEM_LIMIT)


def _dot(a, b, dims=None, precision=None):
    if dims is None:
        return jnp.dot(a, b, preferred_element_type=F32, precision=precision)
    return lax.dot_general(a, b, dims, preferred_element_type=F32, precision=precision)


def _bdot(a, b, dims=None):
    return _dot(a.astype(BF16), b.astype(BF16), dims)


def _layer_norm(z, g, b):
    mu = jnp.mean(z, axis=-1, keepdims=True)
    d = z - mu
    var = jnp.mean(d * d, axis=-1, keepdims=True)
    return d * lax.rsqrt(var + LN_EPS) * g + b


def _full(shape):
    n = len(shape)
    return pl.BlockSpec(shape, lambda *_: (0,) * n)


def _col_chunks(n, width=512):
    out, c = [], 0
    while c < n:
        w = min(width, n - c)
        out.append((c, w))
        c += w
    return out


def _mm_multi_body(x_ref, w_ref, *o_refs, col0s):
    xb = x_ref[...].astype(BF16)
    for o_ref, c0 in zip(o_refs, col0s):
        for c, w in _col_chunks(o_ref.shape[1]):
            o_ref[:, c:c + w] = _dot(xb, w_ref[:, c0 + c:c0 + c + w])


def _mm_f32_body(x_ref, w_ref, o_ref):
    o_ref[...] = _dot(x_ref[...], w_ref[...], precision=HIGHEST)


def _mm_f32(x, w, n_cols):
    m, k = x.shape
    return pl.pallas_call(
        _mm_f32_body,
        grid=(1,),
        in_specs=[_full(x.shape), pl.BlockSpec((k, n_cols), lambda i: (0, 0))],
        out_specs=_full((m, n_cols)),
        out_shape=jax.ShapeDtypeStruct((m, n_cols), F32),
        compiler_params=_cparams("arbitrary"),
    )(x, w)


def _mm_multi(x, w_bf16, widths, tm, first_col=0):
    m, k = x.shape
    col0s, c = [], first_col
    for wd in widths:
        col0s.append(c)
        c += wd
    assert c == w_bf16.shape[1] and m % tm == 0
    return pl.pallas_call(
        functools.partial(_mm_multi_body, col0s=tuple(col0s)),
        grid=(m // tm,),
        in_specs=[pl.BlockSpec((tm, k), lambda i: (i, 0)), _full(w_bf16.shape)],
        out_specs=[pl.BlockSpec((tm, wd), lambda i: (i, 0)) for wd in widths],
        out_shape=[jax.ShapeDtypeStruct((m, wd), F32) for wd in widths],
        compiler_params=_cparams("parallel"),
    )(x, w_bf16)


def _proj_prompt_body(x_ref, w_ref, qt_ref, kt_ref, vt_ref, kb_ref, rw_ref, kmean_ref):
    j = pl.program_id(1)
    tm = x_ref.shape[1]
    xb = x_ref[0].astype(BF16)
    aw = ATT_WIDTH
    q = _dot(xb, w_ref[:, 0:aw])
    k = _dot(xb, w_ref[:, aw:2 * aw])
    v = _dot(xb, w_ref[:, 2 * aw:3 * aw])
    qt_ref[0] = q.T
    kt_ref[0] = k.T
    vt_ref[0] = v.T
    kb_ref[0] = k.astype(BF16)
    for c, w in _col_chunks(RWKV_COLS):
        rw_ref[0, :, c:c + w] = _dot(xb, w_ref[:, 3 * aw + c:3 * aw + c + w])
    blk_per_tile = tm // MOBA_BLOCK
    for i in range(blk_per_tile):
        kmean_ref[0, pl.ds(j * blk_per_tile + i, 1), :] = jnp.mean(
            k[i * MOBA_BLOCK:(i + 1) * MOBA_BLOCK], axis=0, keepdims=True)


def _proj_prompt(x, w_bf16, tm):
    bsz, t, _ = x.shape
    assert t % tm == 0 and tm % MOBA_BLOCK == 0
    n_blk = t // MOBA_BLOCK
    tr_spec = pl.BlockSpec((1, ATT_WIDTH, tm), lambda i, j: (i, 0, j))
    tr_shape = jax.ShapeDtypeStruct((bsz, ATT_WIDTH, t), F32)
    return pl.pallas_call(
        _proj_prompt_body,
        grid=(bsz, t // tm),
        in_specs=[pl.BlockSpec((1, tm, D_MODEL), lambda i, j: (i, j, 0)), _full(w_bf16.shape)],
        out_specs=[tr_spec, tr_spec, tr_spec,
                   pl.BlockSpec((1, tm, ATT_WIDTH), lambda i, j: (i, j, 0)),
                   pl.BlockSpec((1, tm, RWKV_COLS), lambda i, j: (i, j, 0)),
                   pl.BlockSpec((1, n_blk, ATT_WIDTH), lambda i, j: (i, 0, 0))],
        out_shape=[tr_shape, tr_shape, tr_shape,
                   jax.ShapeDtypeStruct((bsz, t, ATT_WIDTH), BF16),
                   jax.ShapeDtypeStruct((bsz, t, RWKV_COLS), F32),
                   jax.ShapeDtypeStruct((bsz, n_blk, ATT_WIDTH), F32)],
        compiler_params=_cparams("parallel", "arbitrary"),
    )(x, w_bf16)


def _mm_res_ln_body(*refs, n_in):
    x_refs = refs[:n_in]
    w_refs = refs[n_in:2 * n_in]
    res_ref, g_ref, b_ref, o_ref = refs[2 * n_in:]
    acc = _dot(x_refs[0][...].astype(BF16), w_refs[0][...])
    for x_ref, w_ref in zip(x_refs[1:], w_refs[1:]):
        acc = acc + _dot(x_ref[...].astype(BF16), w_ref[...])
    o_ref[...] = _layer_norm(ALPHA * res_ref[...] + acc, g_ref[...], b_ref[...])


def _mm_res_ln(xs, ws, res, g, b, tm):
    m = res.shape[0]
    n_in = len(xs)
    assert m % tm == 0
    in_specs = [pl.BlockSpec((tm, x.shape[1]), lambda i: (i, 0)) for x in xs]
    in_specs += [_full(w.shape) for w in ws]
    in_specs += [pl.BlockSpec((tm, D_MODEL), lambda i: (i, 0)), _full((1, D_MODEL)), _full((1, D_MODEL))]
    return pl.pallas_call(
        functools.partial(_mm_res_ln_body, n_in=n_in),
        grid=(m // tm,),
        in_specs=in_specs,
        out_specs=pl.BlockSpec((tm, D_MODEL), lambda i: (i, 0)),
        out_shape=jax.ShapeDtypeStruct((m, D_MODEL), F32),
        compiler_params=_cparams("parallel"),
    )(*xs, *ws, res, g, b)


def _mlp_body(x_ref, w1_ref, w2_ref, g_ref, b_ref, o_ref):
    x = x_ref[...]
    xb = x.astype(BF16)
    acc = jnp.zeros(x.shape, F32)
    for c, w in _col_chunks(D_FF):
        h = jnp.maximum(_dot(xb, w1_ref[:, c:c + w]), 0.0)
        acc = acc + _dot((h * h).astype(BF16), w2_ref[c:c + w, :])
    o_ref[...] = _layer_norm(ALPHA * x + acc, g_ref[...], b_ref[...])


def _mlp(x, w1, w2, g, b, tm):
    m = x.shape[0]
    assert m % tm == 0
    return pl.pallas_call(
        _mlp_body,
        grid=(m // tm,),
        in_specs=[pl.BlockSpec((tm, D_MODEL), lambda i: (i, 0)), _full(w1.shape), _full(w2.shape),
                  _full((1, D_MODEL)), _full((1, D_MODEL))],
        out_specs=pl.BlockSpec((tm, D_MODEL), lambda i: (i, 0)),
        out_shape=jax.ShapeDtypeStruct((m, D_MODEL), F32),
        compiler_params=_cparams("parallel"),
    )(x, w1, w2, g, b)


def _mem_attend(q, k_head, v_head):
    heads = range(MEM_HEADS)
    s = [_bdot(q[:, h * MEM_HEAD_DIM:(h + 1) * MEM_HEAD_DIM], k_head(h), _NT) for h in heads]
    p = [jnp.exp(x - jnp.max(x, axis=-1, keepdims=True)) for x in s]
    l = [jnp.sum(x, axis=-1, keepdims=True) for x in p]
    return jnp.concatenate([_bdot(x, v_head(h)) / z for h, x, z in zip(heads, p, l)], axis=-1)


def _mix_memx_prompt_body(x_ref, ya_ref, yr_ref, mk_ref, mv_ref, wa_ref, wr_ref, wq_ref, wo_ref,
                          g1_ref, b1_ref, g2_ref, b2_ref, o_ref):
    mix = _dot(ya_ref[0].astype(BF16), wa_ref[...]) + _dot(yr_ref[0].astype(BF16), wr_ref[...])
    x1 = _layer_norm(ALPHA * x_ref[0] + mix, g1_ref[...], b1_ref[...])
    q = _dot(x1.astype(BF16), wq_ref[...]) * (MEM_HEAD_DIM ** -0.5)
    cols = lambda h: slice(h * MEM_HEAD_DIM, (h + 1) * MEM_HEAD_DIM)
    att = _mem_attend(q, lambda h: mk_ref[0, :, cols(h)], lambda h: mv_ref[0, :, cols(h)])
    acc = _dot(att.astype(BF16), wo_ref[...])
    o_ref[0] = _layer_norm(ALPHA * x1 + acc, g2_ref[...], b2_ref[...])


def _mix_memx_prompt(x, y_att, y_rw, mk, mv, w_att, w_rw, wq, wo, ln1, ln2, tm):
    bsz, t, _ = x.shape
    assert t % tm == 0
    row = lambda w: pl.BlockSpec((1, tm, w), lambda i, j: (i, j, 0))
    mem = pl.BlockSpec((1, N_MEM, D_MODEL), lambda i, j: (i, 0, 0))
    vec = _full((1, D_MODEL))
    return pl.pallas_call(
        _mix_memx_prompt_body,
        grid=(bsz, t // tm),
        in_specs=[row(D_MODEL), row(ATT_WIDTH), row(RWKV_WIDTH), mem, mem,
                  _full(w_att.shape), _full(w_rw.shape), _full(wq.shape), _full(wo.shape), vec, vec, vec, vec],
        out_specs=row(D_MODEL),
        out_shape=jax.ShapeDtypeStruct(x.shape, F32),
        compiler_params=_cparams("parallel", "parallel"),
    )(x, y_att, y_rw, mk, mv, w_att, w_rw, wq, wo, *ln1, *ln2)


def _memx_sample_body(q_ref, mk_ref, mv_ref, o_ref):
    n_rows = N_MEM * MEM_HEADS
    k_all = mk_ref[0].reshape(n_rows, MEM_HEAD_DIM)
    v_all = mv_ref[0].reshape(n_rows, MEM_HEAD_DIM)
    q = q_ref[0] * (MEM_HEAD_DIM ** -0.5)
    q_rows = [q[:, h * MEM_HEAD_DIM:(h + 1) * MEM_HEAD_DIM] for h in range(MEM_HEADS)]
    q8 = jnp.concatenate(q_rows + [jnp.zeros((8 - MEM_HEADS, MEM_HEAD_DIM), F32)], axis=0)
    s = _bdot(q8, k_all, _NT)
    row = lax.broadcasted_iota(jnp.int32, s.shape, 0)
    own = (lax.broadcasted_iota(jnp.int32, s.shape, 1) % MEM_HEADS) == row
    s = jnp.where(own, s, NEG)
    p = jnp.where(own, jnp.exp(s - jnp.max(s, axis=-1, keepdims=True)), 0.0)
    l = jnp.sum(p, axis=-1, keepdims=True)
    o8 = _bdot(p, v_all) / jnp.where(l > 0.0, l, 1.0)
    o_ref[0] = jnp.concatenate([o8[h:h + 1] for h in range(MEM_HEADS)], axis=-1)


def _memx_sample(q, mk, mv):
    bd = q.shape[0]
    mem_spec = pl.BlockSpec((1, N_MEM, MEM_HEADS, MEM_HEAD_DIM), lambda i: (i, 0, 0, 0))
    return pl.pallas_call(
        _memx_sample_body,
        grid=(bd,),
        in_specs=[pl.BlockSpec((1, 1, D_MODEL), lambda i: (i, 0, 0)), mem_spec, mem_spec],
        out_specs=pl.BlockSpec((1, 1, D_MODEL), lambda i: (i, 0, 0)),
        out_shape=jax.ShapeDtypeStruct((bd, 1, D_MODEL), F32),
        compiler_params=_cparams("parallel"),
    )(q, mk, mv)


def _top3_rows(g, n_iota, n_lim):
    g = jnp.where(n_iota < n_lim, g, -jnp.inf)
    n_rows = g.shape[0]
    sel = n_iota < 0
    for _ in range(MOBA_TOPK):
        mx = jnp.max(g, axis=0, keepdims=True)
        idx = jnp.min(jnp.where(g == mx, n_iota, n_rows), axis=0, keepdims=True)
        hit = n_iota == idx
        sel = sel | hit
        g = jnp.where(hit, -jnp.inf, g)
    return sel & (n_iota < n_lim)


MOBA_QT = 256
MASK_BIG = 1e30
FEAT_ROWS = 8


def _moba_prompt_body(qt_ref, kb_ref, vt_ref, kmean_ref, o_ref, vt_s, fe_s, fo_s, acc_s, *, n_blk):
    c = pl.program_id(1)
    half = HEAD_DIM
    lane_k = lax.broadcasted_iota(jnp.int32, (MOBA_BLOCK, 2 * half), 1)
    row_k = lax.broadcasted_iota(jnp.int32, (MOBA_BLOCK, 2 * half), 0)

    @pl.when(c == 0)
    def _():
        for n in range(n_blk):
            vt_s[n] = vt_ref[0, :, n * MOBA_BLOCK:(n + 1) * MOBA_BLOCK].astype(BF16)
            for tbl, f in ((fe_s, lane_k - half), (fo_s, lane_k)):
                feat = jnp.where((f == 0) | (f == 1) | (f == FEAT_ROWS + n), 1.0, 0.0)
                feat = jnp.where(f == 2, row_k.astype(F32), feat)
                feat = jnp.where(f == 3, float(n * MOBA_BLOCK), feat)
                tbl[n] = feat.astype(BF16)

    heads = [slice(h * HEAD_DIM, (h + 1) * HEAD_DIM) for h in range(H_ATT)]
    q_blk = (c * MOBA_QT) // MOBA_BLOCK
    qt = qt_ref[0]
    n_iota = lax.broadcasted_iota(jnp.int32, (n_blk, MOBA_QT), 0)
    gates = [_dot(kmean_ref[0, :, hs], qt[hs, :], precision=HIGHEST) for hs in heads]
    sel = [_top3_rows(g, n_iota, q_blk) | (n_iota == q_blk) for g in gates]
    acc_s[...] = jnp.zeros(acc_s.shape, F32)

    r8 = lax.broadcasted_iota(jnp.int32, (FEAT_ROWS, MOBA_QT), 0)
    q_rel = lax.broadcasted_iota(jnp.int32, (FEAT_ROWS, MOBA_QT), 1).astype(F32)
    chunk_start = (c * MOBA_QT).astype(F32)
    pad = jnp.zeros((half - FEAT_ROWS - n_blk, MOBA_QT), F32)
    q_aug = []
    for h, hs in enumerate(heads):
        slope = 2.0 ** -(h + 1)
        base = jnp.where(r8 == 0, -slope * chunk_start, 0.0)
        base = jnp.where(r8 == 1, -slope * q_rel, base)
        base = jnp.where((r8 == 2) | (r8 == 3), slope, base)
        feat = jnp.concatenate([base, (sel[h].astype(F32) - 1.0) * MASK_BIG, pad], axis=0)
        q_h = qt[hs, :] * (HEAD_DIM ** -0.5)
        q_aug.append(jnp.concatenate([q_h, feat] if h % 2 == 0 else [feat, q_h], axis=0).astype(BF16))

    left = lane_k < half
    causal = (lax.broadcasted_iota(jnp.int32, (MOBA_BLOCK, MOBA_QT), 0)
              <= lax.broadcasted_iota(jnp.int32, (MOBA_BLOCK, MOBA_QT), 1) + (c * MOBA_QT - q_blk * MOBA_BLOCK))
    head_row = lax.broadcasted_iota(jnp.int32, (H_ATT, MOBA_QT), 0)

    def attend(n, m_all, l_all, own):
        rows = pl.ds(pl.multiple_of(n * MOBA_BLOCK, MOBA_BLOCK), MOBA_BLOCK)
        f_even, f_odd = fe_s[n], fo_s[n]
        vblk = vt_s[n]
        k_aug = []
        for j in range(H_ATT // 2):
            k_pair = kb_ref[0, rows, j * 2 * half:(j + 1) * 2 * half]
            k_aug += [jnp.where(left, k_pair, f_even), jnp.where(left, f_odd, k_pair)]
        s = [_dot(k, q) for k, q in zip(k_aug, q_aug)]
        if own:
            s = [jnp.where(causal, x, NEG) for x in s]
        m_old = [m_all[h:h + 1, :] for h in range(H_ATT)]
        m_new = [jnp.maximum(mo, jnp.max(x, axis=0, keepdims=True)) for mo, x in zip(m_old, s)]
        p = [jnp.exp(x - mn) for x, mn in zip(s, m_new)]
        corr = [jnp.exp(mo - mn) for mo, mn in zip(m_old, m_new)]
        pv = [_dot(vblk[hs, :], x.astype(BF16)) for hs, x in zip(heads, p)]
        for h, hs in enumerate(heads):
            acc_s[hs, :] = acc_s[hs, :] * corr[h] + pv[h]
            l_h = l_all[h:h + 1, :] * corr[h] + jnp.sum(p[h], axis=0, keepdims=True)
            l_all = jnp.where(head_row == h, l_h, l_all)
            m_all = jnp.where(head_row == h, m_new[h], m_all)
        return m_all, l_all

    m0 = jnp.full((H_ATT, MOBA_QT), NEG, F32)
    l0 = jnp.zeros((H_ATT, MOBA_QT), F32)
    m_all, l_all = lax.fori_loop(0, q_blk, lambda n, carry: attend(n, *carry, False), (m0, l0))
    m_all, l_all = attend(q_blk, m_all, l_all, True)
    for h, hs in enumerate(heads):
        acc_s[hs, :] = acc_s[hs, :] / l_all[h:h + 1, :]
    o_ref[0] = acc_s[...].T


def _moba_prompt(qt, kb, vt, kmean):
    bsz, t, _ = kb.shape
    assert t % MOBA_BLOCK == 0 and MOBA_BLOCK % MOBA_QT == 0
    n_blk = t // MOBA_BLOCK
    assert n_blk % 8 == 0 and FEAT_ROWS + n_blk <= HEAD_DIM and H_ATT % 2 == 0
    return pl.pallas_call(
        functools.partial(_moba_prompt_body, n_blk=n_blk),
        grid=(bsz, t // MOBA_QT),
        in_specs=[pl.BlockSpec((1, ATT_WIDTH, MOBA_QT), lambda i, j: (i, 0, j)),
                  pl.BlockSpec((1, t, ATT_WIDTH), lambda i, j: (i, 0, 0)),
                  pl.BlockSpec((1, ATT_WIDTH, t), lambda i, j: (i, 0, 0)),
                  pl.BlockSpec((1, n_blk, ATT_WIDTH), lambda i, j: (i, 0, 0))],
        out_specs=pl.BlockSpec((1, MOBA_QT, ATT_WIDTH), lambda i, j: (i, j, 0)),
        out_shape=jax.ShapeDtypeStruct((bsz, t, ATT_WIDTH), F32),
        scratch_shapes=[pltpu.VMEM((n_blk, ATT_WIDTH, MOBA_BLOCK), BF16),
                        pltpu.VMEM((n_blk, MOBA_BLOCK, 2 * HEAD_DIM), BF16),
                        pltpu.VMEM((n_blk, MOBA_BLOCK, 2 * HEAD_DIM), BF16),
                        pltpu.VMEM((ATT_WIDTH, MOBA_QT), F32)],
        compiler_params=_cparams("parallel", "arbitrary"),
    )(qt, kb, vt, kmean)


PAGES_PER_STEP = 32
PAGES_PER_BLOCK = MOBA_BLOCK // PAGE_SIZE
N_SEL_PAGES = MOBA_TOPK * PAGES_PER_BLOCK


def _moba_scores_body(pt_ref, q_ref, *refs, n_pages):
    page_refs = refs[:PAGES_PER_STEP]
    s_ref, top_ref, qcol_s, gate_s = refs[PAGES_PER_STEP:]
    g = pl.program_id(1)
    n_blk = n_pages // PAGES_PER_BLOCK
    blk_per_step = PAGES_PER_STEP // PAGES_PER_BLOCK

    @pl.when(g == 0)
    def _():
        for h in range(H_ATT):
            qcol_s[h] = jnp.broadcast_to(q_ref[0, h:h + 1, :], (PAGE_SIZE, HEAD_DIM)).T
        gate_s[...] = jnp.zeros(gate_s.shape, F32)

    lane = lax.broadcasted_iota(jnp.int32, (H_ATT, 128), 1)
    for i in range(blk_per_step):
        blk_sum = jnp.zeros((H_ATT, PAGE_SIZE), F32)
        for jj in range(PAGES_PER_BLOCK):
            p = i * PAGES_PER_BLOCK + jj
            s = jnp.sum(page_refs[p][0] * qcol_s[...], axis=1)
            s_ref[0, p] = s
            blk_sum = blk_sum + s
        gate = jnp.sum(blk_sum, axis=1, keepdims=True) * (1.0 / MOBA_BLOCK)
        gate_s[...] = jnp.where(lane == g * blk_per_step + i, gate, gate_s[...])

    @pl.when(g == pl.num_programs(1) - 1)
    def _():
        g_work = jnp.where(lane < n_blk, gate_s[...], -jnp.inf)
        out = jnp.zeros((H_ATT, 128), jnp.int32)
        for r in range(MOBA_TOPK):
            mx = jnp.max(g_work, axis=1, keepdims=True)
            idx = jnp.min(jnp.where(g_work == mx, lane, 128), axis=1, keepdims=True)
            out = jnp.where(lane == r, idx, out)
            g_work = jnp.where(lane == idx, -jnp.inf, g_work)
        top_ref[0] = out


def _moba_scores(q, kt_pool, page_table):
    bd, n_pages = page_table.shape
    assert n_pages % PAGES_PER_STEP == 0 and PAGES_PER_STEP % PAGES_PER_BLOCK == 0
    n_blk = n_pages // PAGES_PER_BLOCK
    assert MOBA_TOPK <= n_blk <= 128

    def page_spec(i):
        return pl.BlockSpec((1, H_ATT, HEAD_DIM, PAGE_SIZE),
                            lambda b, g, pt: (pt[b * n_pages + g * PAGES_PER_STEP + i], 0, 0, 0))

    grid_spec = pltpu.PrefetchScalarGridSpec(
        num_scalar_prefetch=1,
        grid=(bd, n_pages // PAGES_PER_STEP),
        in_specs=[pl.BlockSpec((1, H_ATT, HEAD_DIM), lambda b, g, pt: (b, 0, 0))]
        + [page_spec(i) for i in range(PAGES_PER_STEP)],
        out_specs=[pl.BlockSpec((1, PAGES_PER_STEP, H_ATT, PAGE_SIZE), lambda b, g, pt: (b, g, 0, 0)),
                   pl.BlockSpec((1, H_ATT, 128), lambda b, g, pt: (b, 0, 0))],
        scratch_shapes=[pltpu.VMEM((H_ATT, HEAD_DIM, PAGE_SIZE), F32), pltpu.VMEM((H_ATT, 128), F32)],
    )
    return pl.pallas_call(
        functools.partial(_moba_scores_body, n_pages=n_pages),
        grid_spec=grid_spec,
        out_shape=[jax.ShapeDtypeStruct((bd, n_pages, H_ATT, PAGE_SIZE), F32),
                   jax.ShapeDtypeStruct((bd, H_ATT, 128), jnp.int32)],
        compiler_params=_cparams("parallel", "arbitrary"),
    )(page_table.reshape(-1), q, *([kt_pool] * PAGES_PER_STEP))


def _moba_sample_body(pt_ref, top_ref, q_ref, kn_ref, vn_ref, s_ref, vt_hbm, o_ref, vbuf, sem, *, past, n_pages):
    b = pl.program_id(0)
    units = [(h, j) for h in range(H_ATT) for j in range(N_SEL_PAGES)]

    def seq_page(seq, h, j):
        return top_ref[(seq * H_ATT + h) * MOBA_TOPK + j // PAGES_PER_BLOCK] * PAGES_PER_BLOCK + j % PAGES_PER_BLOCK

    def tile_copies(seq, slot):
        return [pltpu.make_async_copy(vt_hbm.at[pt_ref[seq * n_pages + seq_page(seq, h, j)], h],
                                      vbuf.at[slot, h * N_SEL_PAGES + j], sem.at[slot]) for h, j in units]

    @pl.when(b == 0)
    def _():
        for cp in tile_copies(0, 0):
            cp.start()

    @pl.when(b + 1 < pl.num_programs(0))
    def _():
        for cp in tile_copies(b + 1, (b + 1) % 2):
            cp.start()

    slot = b % 2
    scale = HEAD_DIM ** -0.5
    lane = lax.broadcasted_iota(jnp.int32, (1, PAGE_SIZE), 1)
    logits = []
    for h, j in units:
        page = seq_page(b, h, j)
        dist = (past - (page * PAGE_SIZE + lane)).astype(F32)
        logits.append(s_ref[0, page, h:h + 1, :] * scale - (2.0 ** -(h + 1)) * dist)
    s_self = jnp.sum(q_ref[0] * kn_ref[0], axis=-1, keepdims=True) * scale
    m, p_self = [], []
    for h in range(H_ATT):
        m_h = s_self[h:h + 1]
        for s in logits[h * N_SEL_PAGES:(h + 1) * N_SEL_PAGES]:
            m_h = jnp.maximum(m_h, jnp.max(s, axis=-1, keepdims=True))
        m.append(m_h)
        p_self.append(jnp.exp(s_self[h:h + 1] - m_h))
    p = [jnp.exp(s - m[h]) for s, (h, j) in zip(logits, units)]
    for cp in tile_copies(b, slot):
        cp.wait()
    pv = [_bdot(jnp.broadcast_to(x, (8, PAGE_SIZE)), vbuf[slot, h * N_SEL_PAGES + j], _NT)[0:1]
          for x, (h, j) in zip(p, units)]
    for h in range(H_ATT):
        sel = range(h * N_SEL_PAGES, (h + 1) * N_SEL_PAGES)
        l = p_self[h] + sum(jnp.sum(p[i], axis=-1, keepdims=True) for i in sel)
        acc = p_self[h] * vn_ref[0, h:h + 1, :] + sum(pv[i] for i in sel)
        o_ref[0, h] = acc / l


def _moba_sample(q, k_new, v_new, scores, vt_pool, page_table, top):
    bd, n_pages = page_table.shape
    past = n_pages * PAGE_SIZE
    assert past % MOBA_BLOCK == 0

    def tok_spec():
        return pl.BlockSpec((1, H_ATT, HEAD_DIM), lambda b, pt, tp: (b, 0, 0))

    grid_spec = pltpu.PrefetchScalarGridSpec(
        num_scalar_prefetch=2,
        grid=(bd,),
        in_specs=[tok_spec(), tok_spec(), tok_spec(),
                  pl.BlockSpec((1, n_pages, H_ATT, PAGE_SIZE), lambda b, pt, tp: (b, 0, 0, 0)),
                  pl.BlockSpec(memory_space=pl.ANY)],
        out_specs=pl.BlockSpec((1, H_ATT, 1, HEAD_DIM), lambda b, pt, tp: (b, 0, 0, 0)),
        scratch_shapes=[pltpu.VMEM((2, H_ATT * N_SEL_PAGES, HEAD_DIM, PAGE_SIZE), F32),
                        pltpu.SemaphoreType.DMA((2,))],
    )
    return pl.pallas_call(
        functools.partial(_moba_sample_body, past=past, n_pages=n_pages),
        grid_spec=grid_spec,
        out_shape=jax.ShapeDtypeStruct((bd, H_ATT, 1, HEAD_DIM), F32),
        compiler_params=_cparams("arbitrary"),
    )(page_table.reshape(-1), top.reshape(-1), q, k_new, v_new, scores, vt_pool)


def _softplus(z):
    return jnp.maximum(z, 0.0) + jnp.log1p(jnp.exp(-jnp.abs(z)))


def _sigmoid(z):
    return 1.0 / (1.0 + jnp.exp(-z))


def _rwkv_token_terms(m, w0, w_up, a0, a_up, g_up, k_k, k_a):
    rw = RWKV_WIDTH
    r, k, v = m[:, 0:rw], m[:, rw:2 * rw], m[:, 2 * rw:3 * rw]
    xw = m[:, 3 * rw:3 * rw + W_LORA]
    xa = m[:, 3 * rw + W_LORA:3 * rw + W_LORA + A_LORA]
    xg = m[:, 3 * rw + W_LORA + A_LORA:]
    w_log = -_softplus(-(w0 + _dot(jnp.tanh(xw), w_up, precision=HIGHEST))) - 0.5
    log_decay = -jnp.exp(w_log)
    a = _sigmoid(a0 + _dot(xa, a_up, precision=HIGHEST))
    g = _bdot(_sigmoid(xg), g_up)
    kk = k * k_k
    k2 = k * (1.0 + (a - 1.0) * k_a)
    return r, log_decay, k2, v, kk, a, g


def _head_norm(kk_h):
    return kk_h / jnp.maximum(jnp.sqrt(jnp.sum(kk_h * kk_h, axis=-1, keepdims=True)), 1e-12)


def _rwkv_head_out(y, r_h, k_h, v_h, g_h, rk_h, gng_h, gnb_h):
    mu = jnp.mean(y, axis=-1, keepdims=True)
    d = y - mu
    var = jnp.mean(d * d, axis=-1, keepdims=True)
    yn = d * lax.rsqrt(var + GN_EPS) * gng_h + gnb_h
    bonus = jnp.sum(r_h * k_h * rk_h, axis=-1, keepdims=True) * v_h
    return (yn + bonus) * g_h


PAIR = 2 * RWKV_HEAD


def _pair_blockdiag(y):
    left = lax.broadcasted_iota(jnp.int32, y.shape, 1) < RWKV_HEAD
    zero = jnp.zeros_like(y)
    return jnp.concatenate([jnp.where(left, y, zero), jnp.where(left, zero, y)], axis=0)


def _pair_nn(x, y):
    return _dot(x.astype(BF16), _pair_blockdiag(y.astype(BF16)))


def _pair_nt(x, y):
    return _dot(x.astype(BF16), _pair_blockdiag(y.astype(BF16)), _NT)


def _pair_tn(x, y):
    full = _dot(x.astype(BF16), y.astype(BF16), _TN)
    left = lax.broadcasted_iota(jnp.int32, (RWKV_HEAD, PAIR), 1) < RWKV_HEAD
    return jnp.where(left, full[:RWKV_HEAD], full[RWKV_HEAD:])


def _pair_sum(x):
    left = lax.broadcasted_iota(jnp.int32, x.shape, 1) < RWKV_HEAD
    s_a = jnp.sum(jnp.where(left, x, 0.0), axis=-1, keepdims=True)
    s_b = jnp.sum(jnp.where(left, 0.0, x), axis=-1, keepdims=True)
    return jnp.where(left, s_a, s_b)


def _rwkv_pair_out(y, r_p, k_p, v_p, g_p, rk_p, gng_p, gnb_p):
    inv_n = 1.0 / RWKV_HEAD
    d = y - _pair_sum(y) * inv_n
    var = _pair_sum(d * d) * inv_n
    yn = d * lax.rsqrt(var + GN_EPS) * gng_p + gnb_p
    return (yn + _pair_sum(r_p * k_p * rk_p) * v_p) * g_p


def _rwkv_prompt_body(rw_ref, prev_ref, s0_ref, mu_ref, w0_ref, wup_ref, a0_ref, aup_ref, gup_ref,
                      kk_ref, ka_ref, rk_ref, gng_ref, gnb_ref, y_ref, sfin_ref, state_s, prev_s):
    c = pl.program_id(1)
    ch = RWKV_CHUNK
    tile = rw_ref.shape[1]
    n_ch = tile // ch
    n_pairs = H_RWKV // 2
    pairs = [slice(p * PAIR, (p + 1) * PAIR) for p in range(n_pairs)]

    @pl.when(c == 0)
    def _():
        for p in range(n_pairs):
            state_s[p] = jnp.concatenate([s0_ref[0, (2 * p) * RWKV_HEAD:(2 * p + 1) * RWKV_HEAD, :],
                                          s0_ref[0, (2 * p + 1) * RWKV_HEAD:(2 * p + 2) * RWKV_HEAD, :]], axis=1)
        prev_s[...] = prev_ref[0]

    rw = rw_ref[0]
    row = lax.broadcasted_iota(jnp.int32, rw.shape, 0)
    rw_prev = jnp.where(row == 0, prev_s[...], pltpu.roll(rw, 1, 0))
    prev_s[...] = rw[tile - 1:tile, :]
    m = rw + (rw_prev - rw) * mu_ref[...]
    r, log_decay, k2, v, kk, a, g = _rwkv_token_terms(
        m, w0_ref[...], wup_ref[...], a0_ref[...], aup_ref[...], gup_ref[...], kk_ref[...], ka_ref[...])

    ti = lax.broadcasted_iota(jnp.int32, (tile, tile), 0)
    si = lax.broadcasted_iota(jnp.int32, (tile, tile), 1)
    in_chunk_lower = (si <= ti) & (si // ch == ti // ch)
    cs = _dot(in_chunk_lower.astype(F32), log_decay, precision=HIGHEST)
    cs_end = cs[ch - 1:ch, :]
    rows = lax.broadcasted_iota(jnp.int32, cs.shape, 0)
    for j in range(1, n_ch):
        cs_end = jnp.where(rows >= j * ch, cs[(j + 1) * ch - 1:(j + 1) * ch, :], cs_end)
    gam = jnp.exp(cs)
    gam_prev = jnp.exp(cs - log_decay)
    gam_inv = jnp.exp(-cs)
    gam_tail = jnp.exp(cs_end - cs)

    t_p = lax.broadcasted_iota(jnp.int32, (ch, PAIR), 0)
    s_p = lax.broadcasted_iota(jnp.int32, (ch, PAIR), 1) % RWKV_HEAD
    lower = s_p <= t_p
    strict = s_p < t_p
    eye = (s_p == t_p).astype(F32)

    units = [(slice(j * ch, (j + 1) * ch), ps) for j in range(n_ch) for ps in pairs]
    kk_n = [kk[ts, ps] for ts, ps in units]
    kk_n = [x / jnp.maximum(jnp.sqrt(_pair_sum(x * x)), 1e-12) for x in kk_n]
    a_t = [-n * gam_prev[ts, ps] for n, (ts, ps) in zip(kk_n, units)]
    b_h = [n * a[ts, ps] for n, (ts, ps) in zip(kk_n, units)]
    b_t = [x * gam_inv[ts, ps] for x, (ts, ps) in zip(b_h, units)]
    k_t = [k2[ts, ps] * gam_inv[ts, ps] for ts, ps in units]
    r_t = [r[ts, ps] * gam[ts, ps] for ts, ps in units]
    v_h = [v[ts, ps] for ts, ps in units]
    l_ab = [jnp.where(strict, _pair_nt(x, z), 0.0) for x, z in zip(a_t, b_t)]
    l_ak = [jnp.where(strict, _pair_nt(x, z), 0.0) for x, z in zip(a_t, k_t)]
    m_rb = [jnp.where(lower, _pair_nt(x, z), 0.0) for x, z in zip(r_t, b_t)]
    m_rk = [jnp.where(lower, _pair_nt(x, z), 0.0) for x, z in zip(r_t, k_t)]
    inv = [eye + x for x in l_ab]
    pw = list(l_ab)
    span = 2
    while span < ch:
        pw = [_pair_nn(x, x) for x in pw]
        inv = [i + _pair_nn(i, x) for i, x in zip(inv, pw)]
        span *= 2
    s_cur = [state_s[p] for p in range(n_pairs)]
    for j in range(n_ch):
        sel = range(j * n_pairs, (j + 1) * n_pairs)
        w_ = [_pair_nt(a_t[i], s) + _pair_nn(l_ak[i], v_h[i]) for i, s in zip(sel, s_cur)]
        u = [_pair_nn(inv[i], x) for i, x in zip(sel, w_)]
        y = [_pair_nt(r_t[i], s) + _pair_nn(m_rb[i], uu) + _pair_nn(m_rk[i], v_h[i])
             for i, s, uu in zip(sel, s_cur, u)]
        s_cur = [s * gam[(j + 1) * ch - 1:(j + 1) * ch, units[i][1]] + _pair_tn(uu, b_h[i] * gam_tail[units[i]])
                 + _pair_tn(v_h[i], k2[units[i]] * gam_tail[units[i]]) for i, s, uu in zip(sel, s_cur, u)]
        for i, yy in zip(sel, y):
            ts, ps = units[i]
            y_ref[0, ts, ps] = _rwkv_pair_out(yy, r[ts, ps], k2[ts, ps], v_h[i], g[ts, ps],
                                              rk_ref[:, ps], gng_ref[:, ps], gnb_ref[:, ps])
    for p in range(n_pairs):
        state_s[p] = s_cur[p]

    @pl.when(c == pl.num_programs(1) - 1)
    def _():
        for p in range(n_pairs):
            sfin_ref[0, (2 * p) * RWKV_HEAD:(2 * p + 1) * RWKV_HEAD, :] = state_s[p][:, :RWKV_HEAD]
            sfin_ref[0, (2 * p + 1) * RWKV_HEAD:(2 * p + 2) * RWKV_HEAD, :] = state_s[p][:, RWKV_HEAD:]


def _rwkv_prompt(rw, prev_row, s0, p, tile):
    bsz, t, _ = rw.shape
    assert t % tile == 0 and tile % RWKV_CHUNK == 0 and H_RWKV % 2 == 0
    hn = H_RWKV * RWKV_HEAD
    vec = lambda n: _full((1, n))
    return pl.pallas_call(
        _rwkv_prompt_body,
        grid=(bsz, t // tile),
        in_specs=[pl.BlockSpec((1, tile, RWKV_COLS), lambda i, j: (i, j, 0)),
                  pl.BlockSpec((1, 1, RWKV_COLS), lambda i, j: (i, 0, 0)),
                  pl.BlockSpec((1, hn, RWKV_HEAD), lambda i, j: (i, 0, 0)),
                  vec(RWKV_COLS), vec(RWKV_WIDTH), _full((W_LORA, RWKV_WIDTH)), vec(RWKV_WIDTH),
                  _full((A_LORA, RWKV_WIDTH)), _full((G_LORA, RWKV_WIDTH)),
                  vec(RWKV_WIDTH), vec(RWKV_WIDTH), vec(RWKV_WIDTH), vec(RWKV_WIDTH), vec(RWKV_WIDTH)],
        out_specs=[pl.BlockSpec((1, tile, RWKV_WIDTH), lambda i, j: (i, j, 0)),
                   pl.BlockSpec((1, hn, RWKV_HEAD), lambda i, j: (i, 0, 0))],
        out_shape=[jax.ShapeDtypeStruct((bsz, t, RWKV_WIDTH), F32),
                   jax.ShapeDtypeStruct((bsz, hn, RWKV_HEAD), F32)],
        scratch_shapes=[pltpu.VMEM((H_RWKV // 2, RWKV_HEAD, PAIR), F32), pltpu.VMEM((1, RWKV_COLS), F32)],
        compiler_params=_cparams("parallel", "arbitrary"),
    )(rw, prev_row, s0, p["mu_shift"], p["w0"], p["w_up"], p["a0"], p["a_up"], p["g_up"],
      p["k_k"], p["k_a"], p["r_k"], p["gn_g"], p["gn_b"])


def _rows8(*rows):
    ri = lax.broadcasted_iota(jnp.int32, (8, rows[0].shape[1]), 0)
    out = jnp.zeros((8, rows[0].shape[1]), F32)
    for i, x in enumerate(rows):
        out = jnp.where(ri == i, x, out)
    return out


def _rwkv_step_body(rw_ref, prev_ref, s0_ref, mu_ref, w0_ref, wup_ref, a0_ref, aup_ref, gup_ref,
                    kk_ref, ka_ref, rk_ref, gng_ref, gnb_ref, y_ref, sfin_ref):
    rw = rw_ref[0]
    m = rw + (prev_ref[0] - rw) * mu_ref[...]
    m8 = jnp.broadcast_to(m, (8, RWKV_COLS))
    r, log_decay, k2, v, kk, a, g = _rwkv_token_terms(
        m8, w0_ref[...], wup_ref[...], a0_ref[...], aup_ref[...], gup_ref[...], kk_ref[...], ka_ref[...])
    decay = jnp.exp(log_decay)
    heads = [slice(h * RWKV_HEAD, (h + 1) * RWKV_HEAD) for h in range(H_RWKV)]
    kkn = [_head_norm(kk[0:1, hs]) for hs in heads]
    b_h = [n * a[0:1, hs] for n, hs in zip(kkn, heads)]
    s_0 = [s0_ref[0, hs, :] for hs in heads]
    sa = [_dot(_rows8(-n), s, _NT, HIGHEST)[0:1] for n, s in zip(kkn, s_0)]
    s_1 = [s * decay[0:1, hs] + _dot(_rows8(x, v[0:1, hs]), _rows8(bb, k2[0:1, hs]), _TN, HIGHEST)
           for s, hs, x, bb in zip(s_0, heads, sa, b_h)]
    y = [_dot(_rows8(r[0:1, hs]), s, _NT, HIGHEST)[0:1] for hs, s in zip(heads, s_1)]
    for hs, s, yy in zip(heads, s_1, y):
        sfin_ref[0, hs, :] = s
        y_ref[0, :, hs] = _rwkv_head_out(yy, r[0:1, hs], k2[0:1, hs], v[0:1, hs], g[0:1, hs],
                                         rk_ref[:, hs], gng_ref[:, hs], gnb_ref[:, hs])


def _rwkv_step(rw, prev_row, s0, p):
    bd = rw.shape[0]
    hn = H_RWKV * RWKV_HEAD
    vec = lambda n: _full((1, n))
    return pl.pallas_call(
        _rwkv_step_body,
        grid=(bd,),
        in_specs=[pl.BlockSpec((1, 1, RWKV_COLS), lambda i: (i, 0, 0)),
                  pl.BlockSpec((1, 1, RWKV_COLS), lambda i: (i, 0, 0)),
                  pl.BlockSpec((1, hn, RWKV_HEAD), lambda i: (i, 0, 0)),
                  vec(RWKV_COLS), vec(RWKV_WIDTH), _full((W_LORA, RWKV_WIDTH)), vec(RWKV_WIDTH),
                  _full((A_LORA, RWKV_WIDTH)), _full((G_LORA, RWKV_WIDTH)),
                  vec(RWKV_WIDTH), vec(RWKV_WIDTH), vec(RWKV_WIDTH), vec(RWKV_WIDTH), vec(RWKV_WIDTH)],
        out_specs=[pl.BlockSpec((1, 1, RWKV_WIDTH), lambda i: (i, 0, 0)),
                   pl.BlockSpec((1, hn, RWKV_HEAD), lambda i: (i, 0, 0))],
        out_shape=[jax.ShapeDtypeStruct((bd, 1, RWKV_WIDTH), F32),
                   jax.ShapeDtypeStruct((bd, hn, RWKV_HEAD), F32)],
        compiler_params=_cparams("parallel"),
    )(rw, prev_row, s0, p["mu_shift"], p["w0"], p["w_up"], p["a0"], p["a_up"], p["g_up"],
      p["k_k"], p["k_a"], p["r_k"], p["gn_g"], p["gn_b"])


def _row(x):
    return x.reshape(1, -1)


def kernel(x_prompt, x_sample, mem_prompt, cache_k_pool, cache_v_pool, page_table, cache_mem_k, cache_mem_v, state_wkv, state_shift, w_in, mu_shift, w0, w_up, a0, a_up, g_up, k_k, k_a, r_k, gn_g, gn_b, w_out, ln1_g, ln1_b, w_q_mem, w_kv_mem, w_o_mem, ln2_g, ln2_b, w_ff1, w_ff2, ln3_g, ln3_b):
    assert w_in.shape[0] == DEPTH == 1
    bsz, t, _ = x_prompt.shape
    bd = x_sample.shape[0]
    hn = H_RWKV * RWKV_HEAD

    w_in_b = w_in[0].astype(BF16)
    w_out_b = w_out[0].astype(BF16)
    wq_b = w_q_mem[0].astype(BF16)
    wkv_b = w_kv_mem[0].astype(BF16)
    wo_b = w_o_mem[0].astype(BF16)
    w1_b = w_ff1[0].astype(BF16)
    w2_b = w_ff2[0].astype(BF16)
    ln1 = (_row(ln1_g[0]), _row(ln1_b[0]))
    ln2 = (_row(ln2_g[0]), _row(ln2_b[0]))
    ln3 = (_row(ln3_g[0]), _row(ln3_b[0]))
    rwkv_p = dict(mu_shift=_row(mu_shift[0]), w0=_row(w0[0]), w_up=w_up[0], a0=_row(a0[0]), a_up=a_up[0],
                  g_up=g_up[0], k_k=_row(k_k[0]), k_a=_row(k_a[0]), r_k=_row(r_k[0]),
                  gn_g=_row(gn_g[0]), gn_b=_row(gn_b[0]))

    qt_p, kt_p, vt_p, kb_p, rw_p3, kmean_p = _proj_prompt(x_prompt, w_in_b, tm=512)
    y_att_p = _moba_prompt(qt_p, kb_p, vt_p, kmean_p)
    y_rw_p, s_p = _rwkv_prompt(rw_p3, jnp.zeros((bsz, 1, RWKV_COLS), F32), jnp.zeros((bsz, hn, RWKV_HEAD), F32), rwkv_p,
                               tile=RWKV_TILE)
    mk_p, mv_p = _mm_multi(mem_prompt.reshape(bsz * N_MEM, D_MODEL), wkv_b, (D_MODEL, D_MODEL), tm=512)
    x2_p = _mix_memx_prompt(x_prompt, y_att_p, y_rw_p, mk_p.reshape(bsz, N_MEM, D_MODEL),
                            mv_p.reshape(bsz, N_MEM, D_MODEL), w_out_b[:ATT_WIDTH], w_out_b[ATT_WIDTH:],
                            wq_b, wo_b, ln1, ln2, tm=512)
    y_p = _mlp(x2_p.reshape(bsz * t, D_MODEL), w1_b, w2_b, *ln3, tm=512)

    kt_pool = cache_k_pool[0].transpose(0, 2, 3, 1)
    vt_pool = cache_v_pool[0].transpose(0, 2, 3, 1)
    xs = x_sample.reshape(bd, D_MODEL)
    q_s = _mm_f32(xs, w_in[0], ATT_WIDTH)
    k_s, v_s, rw_s = _mm_multi(xs, w_in_b, (ATT_WIDTH, ATT_WIDTH, RWKV_COLS), tm=bd, first_col=ATT_WIDTH)
    q_s3, k_s3, v_s3 = (z.reshape(bd, H_ATT, HEAD_DIM) for z in (q_s, k_s, v_s))
    scores_s, top = _moba_scores(q_s3, kt_pool, page_table)
    y_att_s = _moba_sample(q_s3, k_s3, v_s3, scores_s, vt_pool, page_table, top[:, :, :MOBA_TOPK])
    y_rw_s, s_s = _rwkv_step(rw_s.reshape(bd, 1, RWKV_COLS), state_shift[0].reshape(bd, 1, RWKV_COLS),
                             state_wkv[0].reshape(bd, hn, RWKV_HEAD), rwkv_p)
    x1_s = _mm_res_ln([y_att_s.reshape(bd, ATT_WIDTH), y_rw_s.reshape(bd, RWKV_WIDTH)],
                      [w_out_b[:ATT_WIDTH], w_out_b[ATT_WIDTH:]], xs, *ln1, tm=bd)
    (qm_s,) = _mm_multi(x1_s, wq_b, (D_MODEL,), tm=bd)
    att_s = _memx_sample(qm_s.reshape(bd, 1, D_MODEL), cache_mem_k[0], cache_mem_v[0])
    x2_s = _mm_res_ln([att_s.reshape(bd, D_MODEL)], [wo_b], x1_s, *ln2, tm=bd)
    y_s = _mlp(x2_s, w1_b, w2_b, *ln3, tm=bd)

    return (y_p.reshape(bsz, t, D_MODEL), y_s.reshape(bd, 1, D_MODEL),
            kt_p.reshape(bsz, H_ATT, HEAD_DIM, t).transpose(0, 3, 1, 2)[None],
            vt_p.reshape(bsz, H_ATT, HEAD_DIM, t).transpose(0, 3, 1, 2)[None],
            k_s.reshape(1, bd, 1, H_ATT, HEAD_DIM), v_s.reshape(1, bd, 1, H_ATT, HEAD_DIM),
            mk_p.reshape(1, bsz, N_MEM, MEM_HEADS, MEM_HEAD_DIM), mv_p.reshape(1, bsz, N_MEM, MEM_HEADS, MEM_HEAD_DIM),
            s_p.reshape(1, bsz, H_RWKV, RWKV_HEAD, RWKV_HEAD), rw_p3[:, -1][None],
            s_s.reshape(1, bd, H_RWKV, RWKV_HEAD, RWKV_HEAD), rw_s[None])
```

```python
import functools

import jax
import jax.numpy as jnp
from jax import lax
from jax.experimental import pallas as pl
from jax.experimental.pallas import tpu as pltpu

F32 = jnp.float32
BF16 = jnp.bfloat16
HIGHEST = lax.Precision.HIGHEST

D_MODEL = 1024
PAGE_SIZE = 128
ATT_WIDTH = 512
RWKV_WIDTH = 512
HEAD_DIM = 64
H_ATT = 8
MOBA_BLOCK = 256
MOBA_TOPK = 3
RWKV_HEAD = 64
H_RWKV = 8
W_LORA = 64
A_LORA = 64
G_LORA = 128
RWKV_COLS = 3 * RWKV_WIDTH + W_LORA + A_LORA + G_LORA
PROJ_COLS = 3 * ATT_WIDTH + RWKV_COLS
N_MEM = 256
MEM_HEADS = 4
MEM_HEAD_DIM = 256
D_FF = 4 * D_MODEL
LN_EPS = 1e-5
GN_EPS = 64e-5
DEPTH = 1
ALPHA = (2.0 * DEPTH) ** 0.25

RWKV_TILE = 256
RWKV_CHUNK = 64
NEG = -1e30
VMEM_LIMIT = 56 * 1024 * 1024

_NT = (((1,), (1,)), ((), ()))
_TN = (((0,), (0,)), ((), ()))


def _cparams(*sem):
    return pltpu.CompilerParams(dimension_semantics=sem, vmem_limit_bytes=VMEM_LIMIT)


def _dot(a, b, dims=None, precision=None):
    if dims is None:
        return jnp.dot(a, b, preferred_element_type=F32, precision=precision)
    return lax.dot_general(a, b, dims, preferred_element_type=F32, precision=precision)


def _bdot(a, b, dims=None):
    return _dot(a.astype(BF16), b.astype(BF16), dims)


def _layer_norm(z, g, b):
    mu = jnp.mean(z, axis=-1, keepdims=True)
    d = z - mu
    var = jnp.mean(d * d, axis=-1, keepdims=True)
    return d * lax.rsqrt(var + LN_EPS) * g + b


def _full(shape):
    n = len(shape)
    return pl.BlockSpec(shape, lambda *_: (0,) * n)


def _col_chunks(n, width=512):
    out, c = [], 0
    while c < n:
        w = min(width, n - c)
        out.append((c, w))
        c += w
    return out


def _mm_multi_body(x_ref, w_ref, *o_refs, col0s):
    xb = x_ref[...].astype(BF16)
    for o_ref, c0 in zip(o_refs, col0s):
        for c, w in _col_chunks(o_ref.shape[1]):
            o_ref[:, c:c + w] = _dot(xb, w_ref[:, c0 + c:c0 + c + w])


def _mm_f32_body(x_ref, w_ref, o_ref):
    o_ref[...] = _dot(x_ref[...], w_ref[...], precision=HIGHEST)


def _mm_f32(x, w, n_cols):
    m, k = x.shape
    return pl.pallas_call(
        _mm_f32_body,
        grid=(1,),
        in_specs=[_full(x.shape), pl.BlockSpec((k, n_cols), lambda i: (0, 0))],
        out_specs=_full((m, n_cols)),
        out_shape=jax.ShapeDtypeStruct((m, n_cols), F32),
        compiler_params=_cparams("arbitrary"),
    )(x, w)


def _mm_multi(x, w_bf16, widths, tm, first_col=0):
    m, k = x.shape
    col0s, c = [], first_col
    for wd in widths:
        col0s.append(c)
        c += wd
    assert c == w_bf16.shape[1] and m % tm == 0
    return pl.pallas_call(
        functools.partial(_mm_multi_body, col0s=tuple(col0s)),
        grid=(m // tm,),
        in_specs=[pl.BlockSpec((tm, k), lambda i: (i, 0)), _full(w_bf16.shape)],
        out_specs=[pl.BlockSpec((tm, wd), lambda i: (i, 0)) for wd in widths],
        out_shape=[jax.ShapeDtypeStruct((m, wd), F32) for wd in widths],
        compiler_params=_cparams("parallel"),
    )(x, w_bf16)


def _proj_prompt_body(x_ref, w_ref, qt_ref, kt_ref, vt_ref, kb_ref, rw_ref, kmean_ref):
    j = pl.program_id(1)
    tm = x_ref.shape[1]
    xb = x_ref[0].astype(BF16)
    aw = ATT_WIDTH
    q = _dot(xb, w_ref[:, 0:aw])
    k = _dot(xb, w_ref[:, aw:2 * aw])
    v = _dot(xb, w_ref[:, 2 * aw:3 * aw])
    qt_ref[0] = q.T
    kt_ref[0] = k.T
    vt_ref[0] = v.T
    kb_ref[0] = k.astype(BF16)
    for c, w in _col_chunks(RWKV_COLS):
        rw_ref[0, :, c:c + w] = _dot(xb, w_ref[:, 3 * aw + c:3 * aw + c + w])
    blk_per_tile = tm // MOBA_BLOCK
    for i in range(blk_per_tile):
        kmean_ref[0, pl.ds(j * blk_per_tile + i, 1), :] = jnp.mean(
            k[i * MOBA_BLOCK:(i + 1) * MOBA_BLOCK], axis=0, keepdims=True)


def _proj_prompt(x, w_bf16, tm):
    bsz, t, _ = x.shape
    assert t % tm == 0 and tm % MOBA_BLOCK == 0
    n_blk = t // MOBA_BLOCK
    tr_spec = pl.BlockSpec((1, ATT_WIDTH, tm), lambda i, j: (i, 0, j))
    tr_shape = jax.ShapeDtypeStruct((bsz, ATT_WIDTH, t), F32)
    return pl.pallas_call(
        _proj_prompt_body,
        grid=(bsz, t // tm),
        in_specs=[pl.BlockSpec((1, tm, D_MODEL), lambda i, j: (i, j, 0)), _full(w_bf16.shape)],
        out_specs=[tr_spec, tr_spec, tr_spec,
                   pl.BlockSpec((1, tm, ATT_WIDTH), lambda i, j: (i, j, 0)),
                   pl.BlockSpec((1, tm, RWKV_COLS), lambda i, j: (i, j, 0)),
                   pl.BlockSpec((1, n_blk, ATT_WIDTH), lambda i, j: (i, 0, 0))],
        out_shape=[tr_shape, tr_shape, tr_shape,
                   jax.ShapeDtypeStruct((bsz, t, ATT_WIDTH), BF16),
                   jax.ShapeDtypeStruct((bsz, t, RWKV_COLS), F32),
                   jax.ShapeDtypeStruct((bsz, n_blk, ATT_WIDTH), F32)],
        compiler_params=_cparams("parallel", "arbitrary"),
    )(x, w_bf16)


def _mm_res_ln_body(*refs, n_in):
    x_refs = refs[:n_in]
    w_refs = refs[n_in:2 * n_in]
    res_ref, g_ref, b_ref, o_ref = refs[2 * n_in:]
    acc = _dot(x_refs[0][...].astype(BF16), w_refs[0][...])
    for x_ref, w_ref in zip(x_refs[1:], w_refs[1:]):
        acc = acc + _dot(x_ref[...].astype(BF16), w_ref[...])
    o_ref[...] = _layer_norm(ALPHA * res_ref[...] + acc, g_ref[...], b_ref[...])


def _mm_res_ln(xs, ws, res, g, b, tm):
    m = res.shape[0]
    n_in = len(xs)
    assert m % tm == 0
    in_specs = [pl.BlockSpec((tm, x.shape[1]), lambda i: (i, 0)) for x in xs]
    in_specs += [_full(w.shape) for w in ws]
    in_specs += [pl.BlockSpec((tm, D_MODEL), lambda i: (i, 0)), _full((1, D_MODEL)), _full((1, D_MODEL))]
    return pl.pallas_call(
        functools.partial(_mm_res_ln_body, n_in=n_in),
        grid=(m // tm,),
        in_specs=in_specs,
        out_specs=pl.BlockSpec((tm, D_MODEL), lambda i: (i, 0)),
        out_shape=jax.ShapeDtypeStruct((m, D_MODEL), F32),
        compiler_params=_cparams("parallel"),
    )(*xs, *ws, res, g, b)


def _mlp_body(x_ref, w1_ref, w2_ref, g_ref, b_ref, o_ref):
    x = x_ref[...]
    xb = x.astype(BF16)
    acc = jnp.zeros(x.shape, F32)
    for c, w in _col_chunks(D_FF):
        h = jnp.maximum(_dot(xb, w1_ref[:, c:c + w]), 0.0)
        acc = acc + _dot((h * h).astype(BF16), w2_ref[c:c + w, :])
    o_ref[...] = _layer_norm(ALPHA * x + acc, g_ref[...], b_ref[...])


def _mlp(x, w1, w2, g, b, tm):
    m = x.shape[0]
    assert m % tm == 0
    return pl.pallas_call(
        _mlp_body,
        grid=(m // tm,),
        in_specs=[pl.BlockSpec((tm, D_MODEL), lambda i: (i, 0)), _full(w1.shape), _full(w2.shape),
                  _full((1, D_MODEL)), _full((1, D_MODEL))],
        out_specs=pl.BlockSpec((tm, D_MODEL), lambda i: (i, 0)),
        out_shape=jax.ShapeDtypeStruct((m, D_MODEL), F32),
        compiler_params=_cparams("parallel"),
    )(x, w1, w2, g, b)


def _mem_attend(q, k_head, v_head):
    heads = range(MEM_HEADS)
    s = [_bdot(q[:, h * MEM_HEAD_DIM:(h + 1) * MEM_HEAD_DIM], k_head(h), _NT) for h in heads]
    p = [jnp.exp(x - jnp.max(x, axis=-1, keepdims=True)) for x in s]
    l = [jnp.sum(x, axis=-1, keepdims=True) for x in p]
    return jnp.concatenate([_bdot(x, v_head(h)) / z for h, x, z in zip(heads, p, l)], axis=-1)


def _mix_memx_prompt_body(x_ref, ya_ref, yr_ref, mk_ref, mv_ref, wa_ref, wr_ref, wq_ref, wo_ref,
                          g1_ref, b1_ref, g2_ref, b2_ref, o_ref):
    mix = _dot(ya_ref[0].astype(BF16), wa_ref[...]) + _dot(yr_ref[0].astype(BF16), wr_ref[...])
    x1 = _layer_norm(ALPHA * x_ref[0] + mix, g1_ref[...], b1_ref[...])
    q = _dot(x1.astype(BF16), wq_ref[...]) * (MEM_HEAD_DIM ** -0.5)
    cols = lambda h: slice(h * MEM_HEAD_DIM, (h + 1) * MEM_HEAD_DIM)
    att = _mem_attend(q, lambda h: mk_ref[0, :, cols(h)], lambda h: mv_ref[0, :, cols(h)])
    acc = _dot(att.astype(BF16), wo_ref[...])
    o_ref[0] = _layer_norm(ALPHA * x1 + acc, g2_ref[...], b2_ref[...])


def _mix_memx_prompt(x, y_att, y_rw, mk, mv, w_att, w_rw, wq, wo, ln1, ln2, tm):
    bsz, t, _ = x.shape
    assert t % tm == 0
    row = lambda w: pl.BlockSpec((1, tm, w), lambda i, j: (i, j, 0))
    mem = pl.BlockSpec((1, N_MEM, D_MODEL), lambda i, j: (i, 0, 0))
    vec = _full((1, D_MODEL))
    return pl.pallas_call(
        _mix_memx_prompt_body,
        grid=(bsz, t // tm),
        in_specs=[row(D_MODEL), row(ATT_WIDTH), row(RWKV_WIDTH), mem, mem,
                  _full(w_att.shape), _full(w_rw.shape), _full(wq.shape), _full(wo.shape), vec, vec, vec, vec],
        out_specs=row(D_MODEL),
        out_shape=jax.ShapeDtypeStruct(x.shape, F32),
        compiler_params=_cparams("parallel", "parallel"),
    )(x, y_att, y_rw, mk, mv, w_att, w_rw, wq, wo, *ln1, *ln2)


def _memx_sample_body(q_ref, mk_ref, mv_ref, o_ref):
    n_rows = N_MEM * MEM_HEADS
    k_all = mk_ref[0].reshape(n_rows, MEM_HEAD_DIM)
    v_all = mv_ref[0].reshape(n_rows, MEM_HEAD_DIM)
    q = q_ref[0] * (MEM_HEAD_DIM ** -0.5)
    q_rows = [q[:, h * MEM_HEAD_DIM:(h + 1) * MEM_HEAD_DIM] for h in range(MEM_HEADS)]
    q8 = jnp.concatenate(q_rows + [jnp.zeros((8 - MEM_HEADS, MEM_HEAD_DIM), F32)], axis=0)
    s = _bdot(q8, k_all, _NT)
    row = lax.broadcasted_iota(jnp.int32, s.shape, 0)
    own = (lax.broadcasted_iota(jnp.int32, s.shape, 1) % MEM_HEADS) == row
    s = jnp.where(own, s, NEG)
    p = jnp.where(own, jnp.exp(s - jnp.max(s, axis=-1, keepdims=True)), 0.0)
    l = jnp.sum(p, axis=-1, keepdims=True)
    o8 = _bdot(p, v_all) / jnp.where(l > 0.0, l, 1.0)
    o_ref[0] = jnp.concatenate([o8[h:h + 1] for h in range(MEM_HEADS)], axis=-1)


def _memx_sample(q, mk, mv):
    bd = q.shape[0]
    mem_spec = pl.BlockSpec((1, N_MEM, MEM_HEADS, MEM_HEAD_DIM), lambda i: (i, 0, 0, 0))
    return pl.pallas_call(
        _memx_sample_body,
        grid=(bd,),
        in_specs=[pl.BlockSpec((1, 1, D_MODEL), lambda i: (i, 0, 0)), mem_spec, mem_spec],
        out_specs=pl.BlockSpec((1, 1, D_MODEL), lambda i: (i, 0, 0)),
        out_shape=jax.ShapeDtypeStruct((bd, 1, D_MODEL), F32),
        compiler_params=_cparams("parallel"),
    )(q, mk, mv)


def _top3_rows(g, n_iota, n_lim):
    g = jnp.where(n_iota < n_lim, g, -jnp.inf)
    n_rows = g.shape[0]
    sel = n_iota < 0
    for _ in range(MOBA_TOPK):
        mx = jnp.max(g, axis=0, keepdims=True)
        idx = jnp.min(jnp.where(g == mx, n_iota, n_rows), axis=0, keepdims=True)
        hit = n_iota == idx
        sel = sel | hit
        g = jnp.where(hit, -jnp.inf, g)
    return sel & (n_iota < n_lim)


MOBA_QT = 256
MASK_BIG = 1e30
FEAT_ROWS = 8


def _moba_prompt_body(qt_ref, kb_ref, vt_ref, kmean_ref, o_ref, vt_s, fe_s, fo_s, acc_s, *, n_blk):
    c = pl.program_id(1)
    half = HEAD_DIM
    lane_k = lax.broadcasted_iota(jnp.int32, (MOBA_BLOCK, 2 * half), 1)
    row_k = lax.broadcasted_iota(jnp.int32, (MOBA_BLOCK, 2 * half), 0)

    @pl.when(c == 0)
    def _():
        for n in range(n_blk):
            vt_s[n] = vt_ref[0, :, n * MOBA_BLOCK:(n + 1) * MOBA_BLOCK].astype(BF16)
            for tbl, f in ((fe_s, lane_k - half), (fo_s, lane_k)):
                feat = jnp.where((f == 0) | (f == 1) | (f == FEAT_ROWS + n), 1.0, 0.0)
                feat = jnp.where(f == 2, row_k.astype(F32), feat)
                feat = jnp.where(f == 3, float(n * MOBA_BLOCK), feat)
                tbl[n] = feat.astype(BF16)

    heads = [slice(h * HEAD_DIM, (h + 1) * HEAD_DIM) for h in range(H_ATT)]
    q_blk = (c * MOBA_QT) // MOBA_BLOCK
    qt = qt_ref[0]
    n_iota = lax.broadcasted_iota(jnp.int32, (n_blk, MOBA_QT), 0)
    gates = [_dot(kmean_ref[0, :, hs], qt[hs, :], precision=HIGHEST) for hs in heads]
    sel = [_top3_rows(g, n_iota, q_blk) | (n_iota == q_blk) for g in gates]
    acc_s[...] = jnp.zeros(acc_s.shape, F32)

    r8 = lax.broadcasted_iota(jnp.int32, (FEAT_ROWS, MOBA_QT), 0)
    q_rel = lax.broadcasted_iota(jnp.int32, (FEAT_ROWS, MOBA_QT), 1).astype(F32)
    chunk_start = (c * MOBA_QT).astype(F32)
    pad = jnp.zeros((half - FEAT_ROWS - n_blk, MOBA_QT), F32)
    q_aug = []
    for h, hs in enumerate(heads):
        slope = 2.0 ** -(h + 1)
        base = jnp.where(r8 == 0, -slope * chunk_start, 0.0)
        base = jnp.where(r8 == 1, -slope * q_rel, base)
        base = jnp.where((r8 == 2) | (r8 == 3), slope, base)
        feat = jnp.concatenate([base, (sel[h].astype(F32) - 1.0) * MASK_BIG, pad], axis=0)
        q_h = qt[hs, :] * (HEAD_DIM ** -0.5)
        q_aug.append(jnp.concatenate([q_h, feat] if h % 2 == 0 else [feat, q_h], axis=0).astype(BF16))

    left = lane_k < half
    causal = (lax.broadcasted_iota(jnp.int32, (MOBA_BLOCK, MOBA_QT), 0)
              <= lax.broadcasted_iota(jnp.int32, (MOBA_BLOCK, MOBA_QT), 1) + (c * MOBA_QT - q_blk * MOBA_BLOCK))
    head_row = lax.broadcasted_iota(jnp.int32, (H_ATT, MOBA_QT), 0)

    def attend(n, m_all, l_all, own):
        rows = pl.ds(pl.multiple_of(n * MOBA_BLOCK, MOBA_BLOCK), MOBA_BLOCK)
        f_even, f_odd = fe_s[n], fo_s[n]
        vblk = vt_s[n]
        k_aug = []
        for j in range(H_ATT // 2):
            k_pair = kb_ref[0, rows, j * 2 * half:(j + 1) * 2 * half]
            k_aug += [jnp.where(left, k_pair, f_even), jnp.where(left, f_odd, k_pair)]
        s = [_dot(k, q) for k, q in zip(k_aug, q_aug)]
        if own:
            s = [jnp.where(causal, x, NEG) for x in s]
        m_old = [m_all[h:h + 1, :] for h in range(H_ATT)]
        m_new = [jnp.maximum(mo, jnp.max(x, axis=0, keepdims=True)) for mo, x in zip(m_old, s)]
        p = [jnp.exp(x - mn) for x, mn in zip(s, m_new)]
        corr = [jnp.exp(mo - mn) for mo, mn in zip(m_old, m_new)]
        pv = [_dot(vblk[hs, :], x.astype(BF16)) for hs, x in zip(heads, p)]
        for h, hs in enumerate(heads):
            acc_s[hs, :] = acc_s[hs, :] * corr[h] + pv[h]
            l_h = l_all[h:h + 1, :] * corr[h] + jnp.sum(p[h], axis=0, keepdims=True)
            l_all = jnp.where(head_row == h, l_h, l_all)
            m_all = jnp.where(head_row == h, m_new[h], m_all)
        return m_all, l_all

    m0 = jnp.full((H_ATT, MOBA_QT), NEG, F32)
    l0 = jnp.zeros((H_ATT, MOBA_QT), F32)
    m_all, l_all = lax.fori_loop(0, q_blk, lambda n, carry: attend(n, *carry, False), (m0, l0))
    m_all, l_all = attend(q_blk, m_all, l_all, True)
    for h, hs in enumerate(heads):
        acc_s[hs, :] = acc_s[hs, :] / l_all[h:h + 1, :]
    o_ref[0] = acc_s[...].T


def _moba_prompt(qt, kb, vt, kmean):
    bsz, t, _ = kb.shape
    assert t % MOBA_BLOCK == 0 and MOBA_BLOCK % MOBA_QT == 0
    n_blk = t // MOBA_BLOCK
    assert n_blk % 8 == 0 and FEAT_ROWS + n_blk <= HEAD_DIM and H_ATT % 2 == 0
    return pl.pallas_call(
        functools.partial(_moba_prompt_body, n_blk=n_blk),
        grid=(bsz, t // MOBA_QT),
        in_specs=[pl.BlockSpec((1, ATT_WIDTH, MOBA_QT), lambda i, j: (i, 0, j)),
                  pl.BlockSpec((1, t, ATT_WIDTH), lambda i, j: (i, 0, 0)),
                  pl.BlockSpec((1, ATT_WIDTH, t), lambda i, j: (i, 0, 0)),
                  pl.BlockSpec((1, n_blk, ATT_WIDTH), lambda i, j: (i, 0, 0))],
        out_specs=pl.BlockSpec((1, MOBA_QT, ATT_WIDTH), lambda i, j: (i, j, 0)),
        out_shape=jax.ShapeDtypeStruct((bsz, t, ATT_WIDTH), F32),
        scratch_shapes=[pltpu.VMEM((n_blk, ATT_WIDTH, MOBA_BLOCK), BF16),
                        pltpu.VMEM((n_blk, MOBA_BLOCK, 2 * HEAD_DIM), BF16),
                        pltpu.VMEM((n_blk, MOBA_BLOCK, 2 * HEAD_DIM), BF16),
                        pltpu.VMEM((ATT_WIDTH, MOBA_QT), F32)],
        compiler_params=_cparams("parallel", "arbitrary"),
    )(qt, kb, vt, kmean)


PAGES_PER_STEP = 32
PAGES_PER_BLOCK = MOBA_BLOCK // PAGE_SIZE
N_SEL_PAGES = MOBA_TOPK * PAGES_PER_BLOCK


def _moba_scores_body(pt_ref, q_ref, kt_hbm, s_ref, top_ref, kbuf, sem, qcol_s, gate_s, *, n_pages):
    b = pl.program_id(0)
    g = pl.program_id(1)
    n_groups = pl.num_programs(1)
    step = b * n_groups + g
    n_blk = n_pages // PAGES_PER_BLOCK
    blk_per_step = PAGES_PER_STEP // PAGES_PER_BLOCK

    def page_copies(stp, slot):
        return [pltpu.make_async_copy(kt_hbm.at[pt_ref[stp * PAGES_PER_STEP + i]], kbuf.at[slot, i], sem.at[slot])
                for i in range(PAGES_PER_STEP)]

    @pl.when(step == 0)
    def _():
        for cp in page_copies(0, 0):
            cp.start()

    @pl.when(step + 1 < pl.num_programs(0) * n_groups)
    def _():
        for cp in page_copies(step + 1, (step + 1) % 2):
            cp.start()

    slot = step % 2
    for cp in page_copies(step, slot):
        cp.wait()

    @pl.when(g == 0)
    def _():
        for h in range(H_ATT):
            qcol_s[h] = jnp.broadcast_to(q_ref[0, h:h + 1, :], (PAGE_SIZE, HEAD_DIM)).T
        gate_s[...] = jnp.zeros(gate_s.shape, F32)

    lane = lax.broadcasted_iota(jnp.int32, (H_ATT, 128), 1)
    for i in range(blk_per_step):
        blk_sum = jnp.zeros((H_ATT, PAGE_SIZE), F32)
        for jj in range(PAGES_PER_BLOCK):
            p = i * PAGES_PER_BLOCK + jj
            s = jnp.sum(kbuf[slot, p] * qcol_s[...], axis=1)
            s_ref[0, p] = s
            blk_sum = blk_sum + s
        gate = jnp.sum(blk_sum, axis=1, keepdims=True) * (1.0 / MOBA_BLOCK)
        gate_s[...] = jnp.where(lane == g * blk_per_step + i, gate, gate_s[...])

    @pl.when(g == pl.num_programs(1) - 1)
    def _():
        g_work = jnp.where(lane < n_blk, gate_s[...], -jnp.inf)
        out = jnp.zeros((H_ATT, 128), jnp.int32)
        for r in range(MOBA_TOPK):
            mx = jnp.max(g_work, axis=1, keepdims=True)
            idx = jnp.min(jnp.where(g_work == mx, lane, 128), axis=1, keepdims=True)
            out = jnp.where(lane == r, idx, out)
            g_work = jnp.where(lane == idx, -jnp.inf, g_work)
        top_ref[0] = out


def _moba_scores(q, kt_pool, page_table):
    bd, n_pages = page_table.shape
    assert n_pages % PAGES_PER_STEP == 0 and PAGES_PER_STEP % PAGES_PER_BLOCK == 0
    n_blk = n_pages // PAGES_PER_BLOCK
    assert MOBA_TOPK <= n_blk <= 128

    grid_spec = pltpu.PrefetchScalarGridSpec(
        num_scalar_prefetch=1,
        grid=(bd, n_pages // PAGES_PER_STEP),
        in_specs=[pl.BlockSpec((1, H_ATT, HEAD_DIM), lambda b, g, pt: (b, 0, 0)),
                  pl.BlockSpec(memory_space=pl.ANY)],
        out_specs=[pl.BlockSpec((1, PAGES_PER_STEP, H_ATT, PAGE_SIZE), lambda b, g, pt: (b, g, 0, 0)),
                   pl.BlockSpec((1, H_ATT, 128), lambda b, g, pt: (b, 0, 0))],
        scratch_shapes=[pltpu.VMEM((2, PAGES_PER_STEP, H_ATT, HEAD_DIM, PAGE_SIZE), F32),
                        pltpu.SemaphoreType.DMA((2,)),
                        pltpu.VMEM((H_ATT, HEAD_DIM, PAGE_SIZE), F32), pltpu.VMEM((H_ATT, 128), F32)],
    )
    return pl.pallas_call(
        functools.partial(_moba_scores_body, n_pages=n_pages),
        grid_spec=grid_spec,
        out_shape=[jax.ShapeDtypeStruct((bd, n_pages, H_ATT, PAGE_SIZE), F32),
                   jax.ShapeDtypeStruct((bd, H_ATT, 128), jnp.int32)],
        compiler_params=_cparams("arbitrary", "arbitrary"),
    )(page_table.reshape(-1), q, kt_pool)


def _moba_sample_body(pt_ref, top_ref, q_ref, kn_ref, vn_ref, s_ref, vt_hbm, o_ref, vbuf, sem, *, past, n_pages):
    b = pl.program_id(0)
    units = [(h, j) for h in range(H_ATT) for j in range(N_SEL_PAGES)]

    def seq_page(seq, h, j):
        return top_ref[(seq * H_ATT + h) * MOBA_TOPK + j // PAGES_PER_BLOCK] * PAGES_PER_BLOCK + j % PAGES_PER_BLOCK

    def tile_copies(seq, slot):
        return [pltpu.make_async_copy(vt_hbm.at[pt_ref[seq * n_pages + seq_page(seq, h, j)], h],
                                      vbuf.at[slot, h * N_SEL_PAGES + j], sem.at[slot]) for h, j in units]

    @pl.when(b == 0)
    def _():
        for cp in tile_copies(0, 0):
            cp.start()

    @pl.when(b + 1 < pl.num_programs(0))
    def _():
        for cp in tile_copies(b + 1, (b + 1) % 2):
            cp.start()

    slot = b % 2
    scale = HEAD_DIM ** -0.5
    lane = lax.broadcasted_iota(jnp.int32, (1, PAGE_SIZE), 1)
    logits = []
    for h, j in units:
        page = seq_page(b, h, j)
        dist = (past - (page * PAGE_SIZE + lane)).astype(F32)
        logits.append(s_ref[0, page, h:h + 1, :] * scale - (2.0 ** -(h + 1)) * dist)
    s_self = jnp.sum(q_ref[0] * kn_ref[0], axis=-1, keepdims=True) * scale
    m, p_self = [], []
    for h in range(H_ATT):
        m_h = s_self[h:h + 1]
        for s in logits[h * N_SEL_PAGES:(h + 1) * N_SEL_PAGES]:
            m_h = jnp.maximum(m_h, jnp.max(s, axis=-1, keepdims=True))
        m.append(m_h)
        p_self.append(jnp.exp(s_self[h:h + 1] - m_h))
    p = [jnp.exp(s - m[h]) for s, (h, j) in zip(logits, units)]
    for cp in tile_copies(b, slot):
        cp.wait()
    pv = [_bdot(jnp.broadcast_to(x, (8, PAGE_SIZE)), vbuf[slot, h * N_SEL_PAGES + j], _NT)[0:1]
          for x, (h, j) in zip(p, units)]
    for h in range(H_ATT):
        sel = range(h * N_SEL_PAGES, (h + 1) * N_SEL_PAGES)
        l = p_self[h] + sum(jnp.sum(p[i], axis=-1, keepdims=True) for i in sel)
        acc = p_self[h] * vn_ref[0, h:h + 1, :] + sum(pv[i] for i in sel)
        o_ref[0, h] = acc / l


def _moba_sample(q, k_new, v_new, scores, vt_pool, page_table, top):
    bd, n_pages = page_table.shape
    past = n_pages * PAGE_SIZE
    assert past % MOBA_BLOCK == 0

    def tok_spec():
        return pl.BlockSpec((1, H_ATT, HEAD_DIM), lambda b, pt, tp: (b, 0, 0))

    grid_spec = pltpu.PrefetchScalarGridSpec(
        num_scalar_prefetch=2,
        grid=(bd,),
        in_specs=[tok_spec(), tok_spec(), tok_spec(),
                  pl.BlockSpec((1, n_pages, H_ATT, PAGE_SIZE), lambda b, pt, tp: (b, 0, 0, 0)),
                  pl.BlockSpec(memory_space=pl.ANY)],
        out_specs=pl.BlockSpec((1, H_ATT, 1, HEAD_DIM), lambda b, pt, tp: (b, 0, 0, 0)),
        scratch_shapes=[pltpu.VMEM((2, H_ATT * N_SEL_PAGES, HEAD_DIM, PAGE_SIZE), F32),
                        pltpu.SemaphoreType.DMA((2,))],
    )
    return pl.pallas_call(
        functools.partial(_moba_sample_body, past=past, n_pages=n_pages),
        grid_spec=grid_spec,
        out_shape=jax.ShapeDtypeStruct((bd, H_ATT, 1, HEAD_DIM), F32),
        compiler_params=_cparams("arbitrary"),
    )(page_table.reshape(-1), top.reshape(-1), q, k_new, v_new, scores, vt_pool)


def _softplus(z):
    return jnp.maximum(z, 0.0) + jnp.log(1.0 + jnp.exp(-jnp.abs(z)))


def _sigmoid(z):
    return 1.0 / (1.0 + jnp.exp(-z))


def _rwkv_token_terms(m, w0, w_up, a0, a_up, g_up, k_k, k_a):
    rw = RWKV_WIDTH
    r, k, v = m[:, 0:rw], m[:, rw:2 * rw], m[:, 2 * rw:3 * rw]
    xw = m[:, 3 * rw:3 * rw + W_LORA]
    xa = m[:, 3 * rw + W_LORA:3 * rw + W_LORA + A_LORA]
    xg = m[:, 3 * rw + W_LORA + A_LORA:]
    w_log = -_softplus(-(w0 + _dot(jnp.tanh(xw), w_up, precision=HIGHEST))) - 0.5
    log_decay = -jnp.exp(w_log)
    a = _sigmoid(a0 + _dot(xa, a_up, precision=HIGHEST))
    g = _bdot(_sigmoid(xg), g_up)
    kk = k * k_k
    k2 = k * (1.0 + (a - 1.0) * k_a)
    return r, log_decay, k2, v, kk, a, g


def _head_norm(kk_h):
    return kk_h * lax.rsqrt(jnp.maximum(jnp.sum(kk_h * kk_h, axis=-1, keepdims=True), 1e-24))


def _rwkv_head_out(y, r_h, k_h, v_h, g_h, rk_h, gng_h, gnb_h):
    mu = jnp.mean(y, axis=-1, keepdims=True)
    d = y - mu
    var = jnp.mean(d * d, axis=-1, keepdims=True)
    yn = d * lax.rsqrt(var + GN_EPS) * gng_h + gnb_h
    bonus = jnp.sum(r_h * k_h * rk_h, axis=-1, keepdims=True) * v_h
    return (yn + bonus) * g_h


PAIR = 2 * RWKV_HEAD


def _pair_blockdiag(y):
    left = lax.broadcasted_iota(jnp.int32, y.shape, 1) < RWKV_HEAD
    zero = jnp.zeros_like(y)
    return jnp.concatenate([jnp.where(left, y, zero), jnp.where(left, zero, y)], axis=0)


def _pair_nn(x, y):
    return _dot(x.astype(BF16), _pair_blockdiag(y.astype(BF16)))


def _pair_nt(x, y):
    return _dot(x.astype(BF16), _pair_blockdiag(y.astype(BF16)), _NT)


def _pair_tn(x, y):
    full = _dot(x.astype(BF16), y.astype(BF16), _TN)
    left = lax.broadcasted_iota(jnp.int32, (RWKV_HEAD, PAIR), 1) < RWKV_HEAD
    return jnp.where(left, full[:RWKV_HEAD], full[RWKV_HEAD:])


def _pair_sum(x):
    left = lax.broadcasted_iota(jnp.int32, x.shape, 1) < RWKV_HEAD
    s_a = jnp.sum(jnp.where(left, x, 0.0), axis=-1, keepdims=True)
    s_b = jnp.sum(jnp.where(left, 0.0, x), axis=-1, keepdims=True)
    return jnp.where(left, s_a, s_b)


def _rwkv_pair_out(y, r_p, k_p, v_p, g_p, rk_p, gng_p, gnb_p):
    inv_n = 1.0 / RWKV_HEAD
    d = y - _pair_sum(y) * inv_n
    var = _pair_sum(d * d) * inv_n
    yn = d * lax.rsqrt(var + GN_EPS) * gng_p + gnb_p
    return (yn + _pair_sum(r_p * k_p * rk_p) * v_p) * g_p


def _rwkv_prompt_body(rw_ref, prev_ref, s0_ref, mu_ref, w0_ref, wup_ref, a0_ref, aup_ref, gup_ref,
                      kk_ref, ka_ref, rk_ref, gng_ref, gnb_ref, y_ref, sfin_ref, state_s, prev_s):
    c = pl.program_id(1)
    ch = RWKV_CHUNK
    tile = rw_ref.shape[1]
    n_ch = tile // ch
    n_pairs = H_RWKV // 2
    pairs = [slice(p * PAIR, (p + 1) * PAIR) for p in range(n_pairs)]

    @pl.when(c == 0)
    def _():
        for p in range(n_pairs):
            state_s[p] = jnp.concatenate([s0_ref[0, (2 * p) * RWKV_HEAD:(2 * p + 1) * RWKV_HEAD, :],
                                          s0_ref[0, (2 * p + 1) * RWKV_HEAD:(2 * p + 2) * RWKV_HEAD, :]], axis=1)
        prev_s[...] = prev_ref[0]

    rw = rw_ref[0]
    row = lax.broadcasted_iota(jnp.int32, rw.shape, 0)
    rw_prev = jnp.where(row == 0, prev_s[...], pltpu.roll(rw, 1, 0))
    prev_s[...] = rw[tile - 1:tile, :]
    m = rw + (rw_prev - rw) * mu_ref[...]
    r, log_decay, k2, v, kk, a, g = _rwkv_token_terms(
        m, w0_ref[...], wup_ref[...], a0_ref[...], aup_ref[...], gup_ref[...], kk_ref[...], ka_ref[...])

    ti = lax.broadcasted_iota(jnp.int32, (tile, tile), 0)
    si = lax.broadcasted_iota(jnp.int32, (tile, tile), 1)
    in_chunk_lower = (si <= ti) & (si // ch == ti // ch)
    cs = _dot(in_chunk_lower.astype(F32), log_decay, precision=HIGHEST)
    cs_end = cs[ch - 1:ch, :]
    rows = lax.broadcasted_iota(jnp.int32, cs.shape, 0)
    for j in range(1, n_ch):
        cs_end = jnp.where(rows >= j * ch, cs[(j + 1) * ch - 1:(j + 1) * ch, :], cs_end)
    gam = jnp.exp(cs)
    gam_prev = jnp.exp(cs - log_decay)
    gam_inv = jnp.exp(-cs)
    gam_tail = jnp.exp(cs_end - cs)

    t_p = lax.broadcasted_iota(jnp.int32, (ch, PAIR), 0)
    s_p = lax.broadcasted_iota(jnp.int32, (ch, PAIR), 1) % RWKV_HEAD
    lower = s_p <= t_p
    strict = s_p < t_p
    eye = (s_p == t_p).astype(F32)

    units = [(slice(j * ch, (j + 1) * ch), ps) for j in range(n_ch) for ps in pairs]
    kk_n = [kk[ts, ps] for ts, ps in units]
    kk_n = [x * lax.rsqrt(jnp.maximum(_pair_sum(x * x), 1e-24)) for x in kk_n]
    a_t = [-n * gam_prev[ts, ps] for n, (ts, ps) in zip(kk_n, units)]
    b_h = [n * a[ts, ps] for n, (ts, ps) in zip(kk_n, units)]
    b_t = [x * gam_inv[ts, ps] for x, (ts, ps) in zip(b_h, units)]
    k_t = [k2[ts, ps] * gam_inv[ts, ps] for ts, ps in units]
    r_t = [r[ts, ps] * gam[ts, ps] for ts, ps in units]
    v_h = [v[ts, ps] for ts, ps in units]
    l_ab = [jnp.where(strict, _pair_nt(x, z), 0.0) for x, z in zip(a_t, b_t)]
    l_ak = [jnp.where(strict, _pair_nt(x, z), 0.0) for x, z in zip(a_t, k_t)]
    m_rb = [jnp.where(lower, _pair_nt(x, z), 0.0) for x, z in zip(r_t, b_t)]
    m_rk = [jnp.where(lower, _pair_nt(x, z), 0.0) for x, z in zip(r_t, k_t)]
    inv = [eye + x for x in l_ab]
    pw = list(l_ab)
    span = 2
    while span < ch:
        pw = [_pair_nn(x, x) for x in pw]
        inv = [i + _pair_nn(i, x) for i, x in zip(inv, pw)]
        span *= 2
    lv = [_pair_nn(l, vv) for l, vv in zip(l_ak, v_h)]
    t_a = [_pair_nn(i, x) for i, x in zip(inv, a_t)]
    t_l = [_pair_nn(i, x) for i, x in zip(inv, lv)]
    b_e = [x * gam_tail[u_] for x, u_ in zip(b_h, units)]
    k_e = [k2[u_] * gam_tail[u_] for u_ in units]
    r_a = [x + _pair_nn(mb, ta) for x, mb, ta in zip(r_t, m_rb, t_a)]
    y_0 = [_pair_nn(mb, tl) + _pair_nn(mk, vv) for mb, tl, mk, vv in zip(m_rb, t_l, m_rk, v_h)]
    m_s = [_pair_tn(ta, be) for ta, be in zip(t_a, b_e)]
    c_s = [_pair_tn(tl, be) + _pair_tn(vv, ke) for tl, be, vv, ke in zip(t_l, b_e, v_h, k_e)]
    s_cur = [state_s[p] for p in range(n_pairs)]
    for j in range(n_ch):
        sel = range(j * n_pairs, (j + 1) * n_pairs)
        y = [_pair_nt(r_a[i], s) + y_0[i] for i, s in zip(sel, s_cur)]
        s_cur = [s * gam[(j + 1) * ch - 1:(j + 1) * ch, units[i][1]] + _pair_nn(s, m_s[i]) + c_s[i]
                 for i, s in zip(sel, s_cur)]
        for i, yy in zip(sel, y):
            ts, ps = units[i]
            y_ref[0, ts, ps] = _rwkv_pair_out(yy, r[ts, ps], k2[ts, ps], v_h[i], g[ts, ps],
                                              rk_ref[:, ps], gng_ref[:, ps], gnb_ref[:, ps])
    for p in range(n_pairs):
        state_s[p] = s_cur[p]

    @pl.when(c == pl.num_programs(1) - 1)
    def _():
        for p in range(n_pairs):
            sfin_ref[0, (2 * p) * RWKV_HEAD:(2 * p + 1) * RWKV_HEAD, :] = state_s[p][:, :RWKV_HEAD]
            sfin_ref[0, (2 * p + 1) * RWKV_HEAD:(2 * p + 2) * RWKV_HEAD, :] = state_s[p][:, RWKV_HEAD:]


def _rwkv_prompt(rw, prev_row, s0, p, tile):
    bsz, t, _ = rw.shape
    assert t % tile == 0 and tile % RWKV_CHUNK == 0 and H_RWKV % 2 == 0
    hn = H_RWKV * RWKV_HEAD
    vec = lambda n: _full((1, n))
    return pl.pallas_call(
        _rwkv_prompt_body,
        grid=(bsz, t // tile),
        in_specs=[pl.BlockSpec((1, tile, RWKV_COLS), lambda i, j: (i, j, 0)),
                  pl.BlockSpec((1, 1, RWKV_COLS), lambda i, j: (i, 0, 0)),
                  pl.BlockSpec((1, hn, RWKV_HEAD), lambda i, j: (i, 0, 0)),
                  vec(RWKV_COLS), vec(RWKV_WIDTH), _full((W_LORA, RWKV_WIDTH)), vec(RWKV_WIDTH),
                  _full((A_LORA, RWKV_WIDTH)), _full((G_LORA, RWKV_WIDTH)),
                  vec(RWKV_WIDTH), vec(RWKV_WIDTH), vec(RWKV_WIDTH), vec(RWKV_WIDTH), vec(RWKV_WIDTH)],
        out_specs=[pl.BlockSpec((1, tile, RWKV_WIDTH), lambda i, j: (i, j, 0)),
                   pl.BlockSpec((1, hn, RWKV_HEAD), lambda i, j: (i, 0, 0))],
        out_shape=[jax.ShapeDtypeStruct((bsz, t, RWKV_WIDTH), F32),
                   jax.ShapeDtypeStruct((bsz, hn, RWKV_HEAD), F32)],
        scratch_shapes=[pltpu.VMEM((H_RWKV // 2, RWKV_HEAD, PAIR), F32), pltpu.VMEM((1, RWKV_COLS), F32)],
        compiler_params=_cparams("parallel", "arbitrary"),
    )(rw, prev_row, s0, p["mu_shift"], p["w0"], p["w_up"], p["a0"], p["a_up"], p["g_up"],
      p["k_k"], p["k_a"], p["r_k"], p["gn_g"], p["gn_b"])


def _rows8(*rows):
    ri = lax.broadcasted_iota(jnp.int32, (8, rows[0].shape[1]), 0)
    out = jnp.zeros((8, rows[0].shape[1]), F32)
    for i, x in enumerate(rows):
        out = jnp.where(ri == i, x, out)
    return out


STEP_ROWS = 8


def _rwkv_step_body(rw_ref, prev_ref, s0_ref, mu_ref, w0_ref, wup_ref, a0_ref, aup_ref, gup_ref,
                    kk_ref, ka_ref, rk_ref, gng_ref, gnb_ref, y_ref, sfin_ref):
    rw = rw_ref[...]
    m = rw + (prev_ref[...] - rw) * mu_ref[...]
    r, log_decay, k2, v, kk, a, g = _rwkv_token_terms(
        m, w0_ref[...], wup_ref[...], a0_ref[...], aup_ref[...], gup_ref[...], kk_ref[...], ka_ref[...])
    decay = jnp.exp(log_decay)
    heads = [slice(h * RWKV_HEAD, (h + 1) * RWKV_HEAD) for h in range(H_RWKV)]
    kkn = [_head_norm(kk[:, hs]) for hs in heads]
    b_h = [n * a[:, hs] for n, hs in zip(kkn, heads)]
    units = [(i, h) for i in range(STEP_ROWS) for h in range(H_RWKV)]
    row = lambda x, i: x[i:i + 1, :]
    s_0 = [s0_ref[i, heads[h], :] for i, h in units]
    sa = [_dot(_rows8(-row(kkn[h], i)), s, _NT, HIGHEST)[0:1] for (i, h), s in zip(units, s_0)]
    s_1 = [s * row(decay[:, heads[h]], i)
           + _dot(_rows8(x, row(v[:, heads[h]], i)), _rows8(row(b_h[h], i), row(k2[:, heads[h]], i)), _TN, HIGHEST)
           for (i, h), s, x in zip(units, s_0, sa)]
    y = [_dot(_rows8(row(r[:, heads[h]], i)), s, _NT, HIGHEST)[0:1] for (i, h), s in zip(units, s_1)]
    for (i, h), s in zip(units, s_1):
        sfin_ref[i, heads[h], :] = s
    seq = lax.broadcasted_iota(jnp.int32, (STEP_ROWS, RWKV_HEAD), 0)
    for h, hs in enumerate(heads):
        y_h = jnp.zeros((STEP_ROWS, RWKV_HEAD), F32)
        for i in range(STEP_ROWS):
            y_h = jnp.where(seq == i, y[i * H_RWKV + h], y_h)
        y_ref[:, hs] = _rwkv_head_out(y_h, r[:, hs], k2[:, hs], v[:, hs], g[:, hs],
                                      rk_ref[:, hs], gng_ref[:, hs], gnb_ref[:, hs])


def _rwkv_step(rw, prev_row, s0, p):
    bd = rw.shape[0]
    assert bd % STEP_ROWS == 0
    hn = H_RWKV * RWKV_HEAD
    vec = lambda n: _full((1, n))
    rows = lambda n: pl.BlockSpec((STEP_ROWS, n), lambda i: (i, 0))
    state = pl.BlockSpec((STEP_ROWS, hn, RWKV_HEAD), lambda i: (i, 0, 0))
    return pl.pallas_call(
        _rwkv_step_body,
        grid=(bd // STEP_ROWS,),
        in_specs=[rows(RWKV_COLS), rows(RWKV_COLS), state,
                  vec(RWKV_COLS), vec(RWKV_WIDTH), _full((W_LORA, RWKV_WIDTH)), vec(RWKV_WIDTH),
                  _full((A_LORA, RWKV_WIDTH)), _full((G_LORA, RWKV_WIDTH)),
                  vec(RWKV_WIDTH), vec(RWKV_WIDTH), vec(RWKV_WIDTH), vec(RWKV_WIDTH), vec(RWKV_WIDTH)],
        out_specs=[rows(RWKV_WIDTH), state],
        out_shape=[jax.ShapeDtypeStruct((bd, RWKV_WIDTH), F32),
                   jax.ShapeDtypeStruct((bd, hn, RWKV_HEAD), F32)],
        compiler_params=_cparams("parallel"),
    )(rw, prev_row, s0, p["mu_shift"], p["w0"], p["w_up"], p["a0"], p["a_up"], p["g_up"],
      p["k_k"], p["k_a"], p["r_k"], p["gn_g"], p["gn_b"])


def _row(x):
    return x.reshape(1, -1)


def kernel(x_prompt, x_sample, mem_prompt, cache_k_pool, cache_v_pool, page_table, cache_mem_k, cache_mem_v, state_wkv, state_shift, w_in, mu_shift, w0, w_up, a0, a_up, g_up, k_k, k_a, r_k, gn_g, gn_b, w_out, ln1_g, ln1_b, w_q_mem, w_kv_mem, w_o_mem, ln2_g, ln2_b, w_ff1, w_ff2, ln3_g, ln3_b):
    assert w_in.shape[0] == DEPTH == 1
    bsz, t, _ = x_prompt.shape
    bd = x_sample.shape[0]
    hn = H_RWKV * RWKV_HEAD

    w_in_b = w_in[0].astype(BF16)
    w_out_b = w_out[0].astype(BF16)
    wq_b = w_q_mem[0].astype(BF16)
    wkv_b = w_kv_mem[0].astype(BF16)
    wo_b = w_o_mem[0].astype(BF16)
    w1_b = w_ff1[0].astype(BF16)
    w2_b = w_ff2[0].astype(BF16)
    ln1 = (_row(ln1_g[0]), _row(ln1_b[0]))
    ln2 = (_row(ln2_g[0]), _row(ln2_b[0]))
    ln3 = (_row(ln3_g[0]), _row(ln3_b[0]))
    rwkv_p = dict(mu_shift=_row(mu_shift[0]), w0=_row(w0[0]), w_up=w_up[0], a0=_row(a0[0]), a_up=a_up[0],
                  g_up=g_up[0], k_k=_row(k_k[0]), k_a=_row(k_a[0]), r_k=_row(r_k[0]),
                  gn_g=_row(gn_g[0]), gn_b=_row(gn_b[0]))

    qt_p, kt_p, vt_p, kb_p, rw_p3, kmean_p = _proj_prompt(x_prompt, w_in_b, tm=512)
    y_att_p = _moba_prompt(qt_p, kb_p, vt_p, kmean_p)
    y_rw_p, s_p = _rwkv_prompt(rw_p3, jnp.zeros((bsz, 1, RWKV_COLS), F32), jnp.zeros((bsz, hn, RWKV_HEAD), F32), rwkv_p,
                               tile=RWKV_TILE)
    mk_p, mv_p = _mm_multi(mem_prompt.reshape(bsz * N_MEM, D_MODEL), wkv_b, (D_MODEL, D_MODEL), tm=512)
    x2_p = _mix_memx_prompt(x_prompt, y_att_p, y_rw_p, mk_p.reshape(bsz, N_MEM, D_MODEL),
                            mv_p.reshape(bsz, N_MEM, D_MODEL), w_out_b[:ATT_WIDTH], w_out_b[ATT_WIDTH:],
                            wq_b, wo_b, ln1, ln2, tm=512)
    y_p = _mlp(x2_p.reshape(bsz * t, D_MODEL), w1_b, w2_b, *ln3, tm=512)

    kt_pool = cache_k_pool[0].transpose(0, 2, 3, 1)
    vt_pool = cache_v_pool[0].transpose(0, 2, 3, 1)
    xs = x_sample.reshape(bd, D_MODEL)
    q_s = _mm_f32(xs, w_in[0], ATT_WIDTH)
    k_s, v_s, rw_s = _mm_multi(xs, w_in_b, (ATT_WIDTH, ATT_WIDTH, RWKV_COLS), tm=bd, first_col=ATT_WIDTH)
    q_s3, k_s3, v_s3 = (z.reshape(bd, H_ATT, HEAD_DIM) for z in (q_s, k_s, v_s))
    scores_s, top = _moba_scores(q_s3, kt_pool, page_table)
    y_att_s = _moba_sample(q_s3, k_s3, v_s3, scores_s, vt_pool, page_table, top[:, :, :MOBA_TOPK])
    y_rw_s, s_s = _rwkv_step(rw_s, state_shift[0], state_wkv[0].reshape(bd, hn, RWKV_HEAD), rwkv_p)
    x1_s = _mm_res_ln([y_att_s.reshape(bd, ATT_WIDTH), y_rw_s],
                      [w_out_b[:ATT_WIDTH], w_out_b[ATT_WIDTH:]], xs, *ln1, tm=bd)
    (qm_s,) = _mm_multi(x1_s, wq_b, (D_MODEL,), tm=bd)
    att_s = _memx_sample(qm_s.reshape(bd, 1, D_MODEL), cache_mem_k[0], cache_mem_v[0])
    x2_s = _mm_res_ln([att_s.reshape(bd, D_MODEL)], [wo_b], x1_s, *ln2, tm=bd)
    y_s = _mlp(x2_s, w1_b, w2_b, *ln3, tm=bd)

    return (y_p.reshape(bsz, t, D_MODEL), y_s.reshape(bd, 1, D_MODEL),
            kt_p.reshape(bsz, H_ATT, HEAD_DIM, t).transpose(0, 3, 1, 2)[None],
            vt_p.reshape(bsz, H_ATT, HEAD_DIM, t).transpose(0, 3, 1, 2)[None],
            k_s.reshape(1, bd, 1, H_ATT, HEAD_DIM), v_s.reshape(1, bd, 1, H_ATT, HEAD_DIM),
            mk_p.reshape(1, bsz, N_MEM, MEM_HEADS, MEM_HEAD_DIM), mv_p.reshape(1, bsz, N_MEM, MEM_HEADS, MEM_HEAD_DIM),
            s_p.reshape(1, bsz, H_RWKV, RWKV_HEAD, RWKV_HEAD), rw_p3[:, -1][None],
            s_s.reshape(1, bd, H_RWKV, RWKV_HEAD, RWKV_HEAD), rw_s[None])
```

```python
import functools

import jax
import jax.numpy as jnp
from jax import lax
from jax.experimental import pallas as pl
from jax.experimental.pallas import tpu as pltpu

F32 = jnp.float32
BF16 = jnp.bfloat16
HIGHEST = lax.Precision.HIGHEST

D_MODEL = 1024
PAGE_SIZE = 128
ATT_WIDTH = 512
RWKV_WIDTH = 512
HEAD_DIM = 64
H_ATT = 8
MOBA_BLOCK = 256
MOBA_TOPK = 3
RWKV_HEAD = 64
H_RWKV = 8
W_LORA = 64
A_LORA = 64
G_LORA = 128
RWKV_COLS = 3 * RWKV_WIDTH + W_LORA + A_LORA + G_LORA
PROJ_COLS = 3 * ATT_WIDTH + RWKV_COLS
N_MEM = 256
MEM_HEADS = 4
MEM_HEAD_DIM = 256
D_FF = 4 * D_MODEL
LN_EPS = 1e-5
GN_EPS = 64e-5
DEPTH = 1
ALPHA = (2.0 * DEPTH) ** 0.25

MLP_TM = 512
RWKV_TILE = 256
RWKV_CHUNK = 64
NEG = -1e30
VMEM_LIMIT = 56 * 1024 * 1024

_NT = (((1,), (1,)), ((), ()))
_TN = (((0,), (0,)), ((), ()))


def _cparams(*sem):
    return pltpu.CompilerParams(dimension_semantics=sem, vmem_limit_bytes=VMEM_LIMIT)


def _dot(a, b, dims=None, precision=None):
    if dims is None:
        return jnp.dot(a, b, preferred_element_type=F32, precision=precision)
    return lax.dot_general(a, b, dims, preferred_element_type=F32, precision=precision)


def _bdot(a, b, dims=None):
    return _dot(a.astype(BF16), b.astype(BF16), dims)


def _layer_norm(z, g, b):
    mu = jnp.mean(z, axis=-1, keepdims=True)
    d = z - mu
    var = jnp.mean(d * d, axis=-1, keepdims=True)
    return d * lax.rsqrt(var + LN_EPS) * g + b


def _full(shape):
    n = len(shape)
    return pl.BlockSpec(shape, lambda *_: (0,) * n)


def _col_chunks(n, width=512):
    out, c = [], 0
    while c < n:
        w = min(width, n - c)
        out.append((c, w))
        c += w
    return out


def _mm_multi_body(x_ref, w_ref, *o_refs, col0s):
    xb = x_ref[...].astype(BF16)
    for o_ref, c0 in zip(o_refs, col0s):
        for c, w in _col_chunks(o_ref.shape[1]):
            o_ref[:, c:c + w] = _dot(xb, w_ref[:, c0 + c:c0 + c + w])


def _mm_f32_body(x_ref, w_ref, o_ref):
    o_ref[...] = _dot(x_ref[...], w_ref[...], precision=HIGHEST)


def _mm_f32(x, w, n_cols):
    m, k = x.shape
    return pl.pallas_call(
        _mm_f32_body,
        grid=(1,),
        in_specs=[_full(x.shape), pl.BlockSpec((k, n_cols), lambda i: (0, 0))],
        out_specs=_full((m, n_cols)),
        out_shape=jax.ShapeDtypeStruct((m, n_cols), F32),
        compiler_params=_cparams("arbitrary"),
    )(x, w)


def _mm_multi(x, w_bf16, widths, tm, first_col=0):
    m, k = x.shape
    col0s, c = [], first_col
    for wd in widths:
        col0s.append(c)
        c += wd
    assert c == w_bf16.shape[1] and m % tm == 0
    return pl.pallas_call(
        functools.partial(_mm_multi_body, col0s=tuple(col0s)),
        grid=(m // tm,),
        in_specs=[pl.BlockSpec((tm, k), lambda i: (i, 0)), _full(w_bf16.shape)],
        out_specs=[pl.BlockSpec((tm, wd), lambda i: (i, 0)) for wd in widths],
        out_shape=[jax.ShapeDtypeStruct((m, wd), F32) for wd in widths],
        compiler_params=_cparams("parallel"),
    )(x, w_bf16)


def _proj_prompt_body(x_ref, w_ref, qt_ref, kt_ref, vt_ref, kb_ref, rw_ref, kmean_ref):
    j = pl.program_id(1)
    tm = x_ref.shape[1]
    xb = x_ref[0].astype(BF16)
    aw = ATT_WIDTH
    q = _dot(xb, w_ref[:, 0:aw])
    k = _dot(xb, w_ref[:, aw:2 * aw])
    v = _dot(xb, w_ref[:, 2 * aw:3 * aw])
    qt_ref[0] = q.T
    kt_ref[0] = k.T
    vt_ref[0] = v.T
    kb_ref[0] = k.astype(BF16)
    for c, w in _col_chunks(RWKV_COLS):
        rw_ref[0, :, c:c + w] = _dot(xb, w_ref[:, 3 * aw + c:3 * aw + c + w])
    blk_per_tile = tm // MOBA_BLOCK
    for i in range(blk_per_tile):
        kmean_ref[0, pl.ds(j * blk_per_tile + i, 1), :] = jnp.mean(
            k[i * MOBA_BLOCK:(i + 1) * MOBA_BLOCK], axis=0, keepdims=True)


def _proj_prompt(x, w_bf16, tm):
    bsz, t, _ = x.shape
    assert t % tm == 0 and tm % MOBA_BLOCK == 0
    n_blk = t // MOBA_BLOCK
    tr_spec = pl.BlockSpec((1, ATT_WIDTH, tm), lambda i, j: (i, 0, j))
    tr_shape = jax.ShapeDtypeStruct((bsz, ATT_WIDTH, t), F32)
    return pl.pallas_call(
        _proj_prompt_body,
        grid=(bsz, t // tm),
        in_specs=[pl.BlockSpec((1, tm, D_MODEL), lambda i, j: (i, j, 0)), _full(w_bf16.shape)],
        out_specs=[tr_spec, tr_spec, tr_spec,
                   pl.BlockSpec((1, tm, ATT_WIDTH), lambda i, j: (i, j, 0)),
                   pl.BlockSpec((1, tm, RWKV_COLS), lambda i, j: (i, j, 0)),
                   pl.BlockSpec((1, n_blk, ATT_WIDTH), lambda i, j: (i, 0, 0))],
        out_shape=[tr_shape, tr_shape, tr_shape,
                   jax.ShapeDtypeStruct((bsz, t, ATT_WIDTH), BF16),
                   jax.ShapeDtypeStruct((bsz, t, RWKV_COLS), F32),
                   jax.ShapeDtypeStruct((bsz, n_blk, ATT_WIDTH), F32)],
        compiler_params=_cparams("parallel", "arbitrary"),
    )(x, w_bf16)


def _mm_res_ln_body(*refs, n_in):
    x_refs = refs[:n_in]
    w_refs = refs[n_in:2 * n_in]
    res_ref, g_ref, b_ref, o_ref = refs[2 * n_in:]
    acc = _dot(x_refs[0][...].astype(BF16), w_refs[0][...])
    for x_ref, w_ref in zip(x_refs[1:], w_refs[1:]):
        acc = acc + _dot(x_ref[...].astype(BF16), w_ref[...])
    o_ref[...] = _layer_norm(ALPHA * res_ref[...] + acc, g_ref[...], b_ref[...])


def _mm_res_ln(xs, ws, res, g, b, tm):
    m = res.shape[0]
    n_in = len(xs)
    assert m % tm == 0
    in_specs = [pl.BlockSpec((tm, x.shape[1]), lambda i: (i, 0)) for x in xs]
    in_specs += [_full(w.shape) for w in ws]
    in_specs += [pl.BlockSpec((tm, D_MODEL), lambda i: (i, 0)), _full((1, D_MODEL)), _full((1, D_MODEL))]
    return pl.pallas_call(
        functools.partial(_mm_res_ln_body, n_in=n_in),
        grid=(m // tm,),
        in_specs=in_specs,
        out_specs=pl.BlockSpec((tm, D_MODEL), lambda i: (i, 0)),
        out_shape=jax.ShapeDtypeStruct((m, D_MODEL), F32),
        compiler_params=_cparams("parallel"),
    )(*xs, *ws, res, g, b)


def _mlp_body(x_ref, w1_ref, w2_ref, g_ref, b_ref, o_ref):
    x = x_ref[...]
    xb = x.astype(BF16)
    acc = jnp.zeros(x.shape, F32)
    for c, w in _col_chunks(D_FF):
        h = jnp.maximum(_dot(xb, w1_ref[:, c:c + w]), 0.0)
        acc = acc + _dot((h * h).astype(BF16), w2_ref[c:c + w, :])
    o_ref[...] = _layer_norm(ALPHA * x + acc, g_ref[...], b_ref[...])


def _mlp(x, w1, w2, g, b, tm):
    m = x.shape[0]
    assert m % tm == 0
    return pl.pallas_call(
        _mlp_body,
        grid=(m // tm,),
        in_specs=[pl.BlockSpec((tm, D_MODEL), lambda i: (i, 0)), _full(w1.shape), _full(w2.shape),
                  _full((1, D_MODEL)), _full((1, D_MODEL))],
        out_specs=pl.BlockSpec((tm, D_MODEL), lambda i: (i, 0)),
        out_shape=jax.ShapeDtypeStruct((m, D_MODEL), F32),
        compiler_params=_cparams("parallel"),
    )(x, w1, w2, g, b)


def _mem_attend(q, k_head, v_head):
    heads = range(MEM_HEADS)
    s = [_bdot(q[:, h * MEM_HEAD_DIM:(h + 1) * MEM_HEAD_DIM], k_head(h), _NT) for h in heads]
    p = [jnp.exp(x - jnp.max(x, axis=-1, keepdims=True)) for x in s]
    l = [jnp.sum(x, axis=-1, keepdims=True) for x in p]
    return jnp.concatenate([_bdot(x, v_head(h)) / z for h, x, z in zip(heads, p, l)], axis=-1)


def _mix_memx_prompt_body(x_ref, ya_ref, yr_ref, mk_ref, mv_ref, wa_ref, wr_ref, wq_ref, wo_ref,
                          g1_ref, b1_ref, g2_ref, b2_ref, o_ref):
    mix = _dot(ya_ref[0].astype(BF16), wa_ref[...]) + _dot(yr_ref[0].astype(BF16), wr_ref[...])
    x1 = _layer_norm(ALPHA * x_ref[0] + mix, g1_ref[...], b1_ref[...])
    q = _dot(x1.astype(BF16), wq_ref[...]) * (MEM_HEAD_DIM ** -0.5)
    cols = lambda h: slice(h * MEM_HEAD_DIM, (h + 1) * MEM_HEAD_DIM)
    att = _mem_attend(q, lambda h: mk_ref[0, :, cols(h)], lambda h: mv_ref[0, :, cols(h)])
    acc = _dot(att.astype(BF16), wo_ref[...])
    o_ref[0] = _layer_norm(ALPHA * x1 + acc, g2_ref[...], b2_ref[...])


def _mix_memx_prompt(x, y_att, y_rw, mk, mv, w_att, w_rw, wq, wo, ln1, ln2, tm):
    bsz, t, _ = x.shape
    assert t % tm == 0
    row = lambda w: pl.BlockSpec((1, tm, w), lambda i, j: (i, j, 0))
    mem = pl.BlockSpec((1, N_MEM, D_MODEL), lambda i, j: (i, 0, 0))
    vec = _full((1, D_MODEL))
    return pl.pallas_call(
        _mix_memx_prompt_body,
        grid=(bsz, t // tm),
        in_specs=[row(D_MODEL), row(ATT_WIDTH), row(RWKV_WIDTH), mem, mem,
                  _full(w_att.shape), _full(w_rw.shape), _full(wq.shape), _full(wo.shape), vec, vec, vec, vec],
        out_specs=row(D_MODEL),
        out_shape=jax.ShapeDtypeStruct(x.shape, F32),
        compiler_params=_cparams("parallel", "parallel"),
    )(x, y_att, y_rw, mk, mv, w_att, w_rw, wq, wo, *ln1, *ln2)


def _memx_sample_body(q_ref, mk_ref, mv_ref, o_ref):
    n_rows = N_MEM * MEM_HEADS
    k_all = mk_ref[0].reshape(n_rows, MEM_HEAD_DIM)
    v_all = mv_ref[0].reshape(n_rows, MEM_HEAD_DIM)
    q = q_ref[0] * (MEM_HEAD_DIM ** -0.5)
    q_rows = [q[:, h * MEM_HEAD_DIM:(h + 1) * MEM_HEAD_DIM] for h in range(MEM_HEADS)]
    q8 = jnp.concatenate(q_rows + [jnp.zeros((8 - MEM_HEADS, MEM_HEAD_DIM), F32)], axis=0)
    s = _bdot(q8, k_all, _NT)
    row = lax.broadcasted_iota(jnp.int32, s.shape, 0)
    own = (lax.broadcasted_iota(jnp.int32, s.shape, 1) % MEM_HEADS) == row
    s = jnp.where(own, s, NEG)
    p = jnp.where(own, jnp.exp(s - jnp.max(s, axis=-1, keepdims=True)), 0.0)
    l = jnp.sum(p, axis=-1, keepdims=True)
    o8 = _bdot(p, v_all) / jnp.where(l > 0.0, l, 1.0)
    o_ref[0] = jnp.concatenate([o8[h:h + 1] for h in range(MEM_HEADS)], axis=-1)


def _memx_sample(q, mk, mv):
    bd = q.shape[0]
    mem_spec = pl.BlockSpec((1, N_MEM, MEM_HEADS, MEM_HEAD_DIM), lambda i: (i, 0, 0, 0))
    return pl.pallas_call(
        _memx_sample_body,
        grid=(bd,),
        in_specs=[pl.BlockSpec((1, 1, D_MODEL), lambda i: (i, 0, 0)), mem_spec, mem_spec],
        out_specs=pl.BlockSpec((1, 1, D_MODEL), lambda i: (i, 0, 0)),
        out_shape=jax.ShapeDtypeStruct((bd, 1, D_MODEL), F32),
        compiler_params=_cparams("parallel"),
    )(q, mk, mv)


def _top3_rows(g, n_iota, n_lim):
    g = jnp.where(n_iota < n_lim, g, -jnp.inf)
    n_rows = g.shape[0]
    sel = n_iota < 0
    for _ in range(MOBA_TOPK):
        mx = jnp.max(g, axis=0, keepdims=True)
        idx = jnp.min(jnp.where(g == mx, n_iota, n_rows), axis=0, keepdims=True)
        hit = n_iota == idx
        sel = sel | hit
        g = jnp.where(hit, -jnp.inf, g)
    return sel & (n_iota < n_lim)


MOBA_QT = 256
MASK_BIG = 1e30
FEAT_ROWS = 8


def _moba_prompt_body(qt_ref, kb_ref, vt_ref, kmean_ref, o_ref, vt_s, fe_s, fo_s, acc_s, *, n_blk):
    c = pl.program_id(1)
    half = HEAD_DIM
    lane_k = lax.broadcasted_iota(jnp.int32, (MOBA_BLOCK, 2 * half), 1)
    row_k = lax.broadcasted_iota(jnp.int32, (MOBA_BLOCK, 2 * half), 0)

    @pl.when(c == 0)
    def _():
        for n in range(n_blk):
            vt_s[n] = vt_ref[0, :, n * MOBA_BLOCK:(n + 1) * MOBA_BLOCK].astype(BF16)
            for tbl, f in ((fe_s, lane_k - half), (fo_s, lane_k)):
                feat = jnp.where((f == 0) | (f == 1) | (f == FEAT_ROWS + n), 1.0, 0.0)
                feat = jnp.where(f == 2, row_k.astype(F32), feat)
                feat = jnp.where(f == 3, float(n * MOBA_BLOCK), feat)
                tbl[n] = feat.astype(BF16)

    heads = [slice(h * HEAD_DIM, (h + 1) * HEAD_DIM) for h in range(H_ATT)]
    q_blk = (c * MOBA_QT) // MOBA_BLOCK
    qt = qt_ref[0]
    n_iota = lax.broadcasted_iota(jnp.int32, (n_blk, MOBA_QT), 0)
    gates = [_dot(kmean_ref[0, :, hs], qt[hs, :], precision=HIGHEST) for hs in heads]
    sel = [_top3_rows(g, n_iota, q_blk) | (n_iota == q_blk) for g in gates]
    acc_s[...] = jnp.zeros(acc_s.shape, F32)

    r8 = lax.broadcasted_iota(jnp.int32, (FEAT_ROWS, MOBA_QT), 0)
    q_rel = lax.broadcasted_iota(jnp.int32, (FEAT_ROWS, MOBA_QT), 1).astype(F32)
    chunk_start = (c * MOBA_QT).astype(F32)
    pad = jnp.zeros((half - FEAT_ROWS - n_blk, MOBA_QT), F32)
    q_aug = []
    for h, hs in enumerate(heads):
        slope = 2.0 ** -(h + 1)
        base = jnp.where(r8 == 0, -slope * chunk_start, 0.0)
        base = jnp.where(r8 == 1, -slope * q_rel, base)
        base = jnp.where((r8 == 2) | (r8 == 3), slope, base)
        feat = jnp.concatenate([base, (sel[h].astype(F32) - 1.0) * MASK_BIG, pad], axis=0)
        q_h = qt[hs, :] * (HEAD_DIM ** -0.5)
        q_aug.append(jnp.concatenate([q_h, feat] if h % 2 == 0 else [feat, q_h], axis=0).astype(BF16))

    left = lane_k < half
    causal = (lax.broadcasted_iota(jnp.int32, (MOBA_BLOCK, MOBA_QT), 0)
              <= lax.broadcasted_iota(jnp.int32, (MOBA_BLOCK, MOBA_QT), 1) + (c * MOBA_QT - q_blk * MOBA_BLOCK))
    head_row = lax.broadcasted_iota(jnp.int32, (H_ATT, MOBA_QT), 0)

    def attend(n, m_all, l_all, own):
        rows = pl.ds(pl.multiple_of(n * MOBA_BLOCK, MOBA_BLOCK), MOBA_BLOCK)
        f_even, f_odd = fe_s[n], fo_s[n]
        vblk = vt_s[n]
        k_aug = []
        for j in range(H_ATT // 2):
            k_pair = kb_ref[0, rows, j * 2 * half:(j + 1) * 2 * half]
            k_aug += [jnp.where(left, k_pair, f_even), jnp.where(left, f_odd, k_pair)]
        s = [_dot(k, q) for k, q in zip(k_aug, q_aug)]
        if own:
            s = [jnp.where(causal, x, NEG) for x in s]
        m_old = [m_all[h:h + 1, :] for h in range(H_ATT)]
        m_new = [jnp.maximum(mo, jnp.max(x, axis=0, keepdims=True)) for mo, x in zip(m_old, s)]
        p = [jnp.exp(x - mn) for x, mn in zip(s, m_new)]
        corr = [jnp.exp(mo - mn) for mo, mn in zip(m_old, m_new)]
        pv = [_dot(vblk[hs, :], x.astype(BF16)) for hs, x in zip(heads, p)]
        for h, hs in enumerate(heads):
            acc_s[hs, :] = acc_s[hs, :] * corr[h] + pv[h]
            l_h = l_all[h:h + 1, :] * corr[h] + jnp.sum(p[h], axis=0, keepdims=True)
            l_all = jnp.where(head_row == h, l_h, l_all)
            m_all = jnp.where(head_row == h, m_new[h], m_all)
        return m_all, l_all

    m0 = jnp.full((H_ATT, MOBA_QT), NEG, F32)
    l0 = jnp.zeros((H_ATT, MOBA_QT), F32)
    m_all, l_all = lax.fori_loop(0, q_blk, lambda n, carry: attend(n, *carry, False), (m0, l0))
    m_all, l_all = attend(q_blk, m_all, l_all, True)
    for h, hs in enumerate(heads):
        acc_s[hs, :] = acc_s[hs, :] / l_all[h:h + 1, :]
    o_ref[0] = acc_s[...].T


def _moba_prompt(qt, kb, vt, kmean):
    bsz, t, _ = kb.shape
    assert t % MOBA_BLOCK == 0 and MOBA_BLOCK % MOBA_QT == 0
    n_blk = t // MOBA_BLOCK
    assert n_blk % 8 == 0 and FEAT_ROWS + n_blk <= HEAD_DIM and H_ATT % 2 == 0
    return pl.pallas_call(
        functools.partial(_moba_prompt_body, n_blk=n_blk),
        grid=(bsz, t // MOBA_QT),
        in_specs=[pl.BlockSpec((1, ATT_WIDTH, MOBA_QT), lambda i, j: (i, 0, j)),
                  pl.BlockSpec((1, t, ATT_WIDTH), lambda i, j: (i, 0, 0)),
                  pl.BlockSpec((1, ATT_WIDTH, t), lambda i, j: (i, 0, 0)),
                  pl.BlockSpec((1, n_blk, ATT_WIDTH), lambda i, j: (i, 0, 0))],
        out_specs=pl.BlockSpec((1, MOBA_QT, ATT_WIDTH), lambda i, j: (i, j, 0)),
        out_shape=jax.ShapeDtypeStruct((bsz, t, ATT_WIDTH), F32),
        scratch_shapes=[pltpu.VMEM((n_blk, ATT_WIDTH, MOBA_BLOCK), BF16),
                        pltpu.VMEM((n_blk, MOBA_BLOCK, 2 * HEAD_DIM), BF16),
                        pltpu.VMEM((n_blk, MOBA_BLOCK, 2 * HEAD_DIM), BF16),
                        pltpu.VMEM((ATT_WIDTH, MOBA_QT), F32)],
        compiler_params=_cparams("parallel", "arbitrary"),
    )(qt, kb, vt, kmean)


PAGES_PER_STEP = 32
PAGES_PER_BLOCK = MOBA_BLOCK // PAGE_SIZE
N_SEL_PAGES = MOBA_TOPK * PAGES_PER_BLOCK


def _mlp_scores_body(pt_ref, x_ref, w1_ref, w2_ref, g_ref, b_ref, q_ref, kt_hbm, o_ref, s_ref, top_ref,
                     kbuf, sem, qcol_s, gate_s, *, n_pages, steps_per_seq):
    i = pl.program_id(0)
    part = i % steps_per_seq
    n_groups = n_pages // PAGES_PER_STEP // steps_per_seq
    n_blk = n_pages // PAGES_PER_BLOCK
    blk_per_group = PAGES_PER_STEP // PAGES_PER_BLOCK
    ff_chunks = _col_chunks(D_FF)
    ff_per_group = len(ff_chunks) // n_groups
    blk_per_ff = blk_per_group // ff_per_group

    def page_copies(group, slot):
        return [pltpu.make_async_copy(kt_hbm.at[pt_ref[group * PAGES_PER_STEP + j]], kbuf.at[slot, j], sem.at[slot])
                for j in range(PAGES_PER_STEP)]

    @pl.when(i == 0)
    def _():
        for cp in page_copies(0, 0):
            cp.start()

    @pl.when(part == 0)
    def _():
        for h in range(H_ATT):
            qcol_s[h] = jnp.broadcast_to(q_ref[0, h:h + 1, :], (PAGE_SIZE, HEAD_DIM)).T
        gate_s[...] = jnp.zeros(gate_s.shape, F32)

    lane = lax.broadcasted_iota(jnp.int32, (H_ATT, 128), 1)
    gates = gate_s[...]
    x = x_ref[...]
    xb = x.astype(BF16)
    acc = jnp.zeros(x.shape, F32)
    for g in range(n_groups):
        group = i * n_groups + g

        @pl.when(group + 1 < pl.num_programs(0) * n_groups)
        def _():
            for cp in page_copies(group + 1, (g + 1) % 2):
                cp.start()

        for cp in page_copies(group, g % 2):
            cp.wait()
        for f, (c, w) in enumerate(ff_chunks[g * ff_per_group:(g + 1) * ff_per_group]):
            h = jnp.maximum(_dot(xb, w1_ref[:, c:c + w]), 0.0)
            acc = acc + _dot((h * h).astype(BF16), w2_ref[c:c + w, :])
            for blk in range(f * blk_per_ff, (f + 1) * blk_per_ff):
                blk_sum = jnp.zeros((H_ATT, PAGE_SIZE), F32)
                for jj in range(PAGES_PER_BLOCK):
                    p = blk * PAGES_PER_BLOCK + jj
                    s = jnp.sum(kbuf[g % 2, p] * qcol_s[...], axis=1)
                    s_ref[0, g * PAGES_PER_STEP + p] = s
                    blk_sum = blk_sum + s
                gate = jnp.sum(blk_sum, axis=1, keepdims=True) * (1.0 / MOBA_BLOCK)
                gates = jnp.where(lane == (part * n_groups + g) * blk_per_group + blk, gate, gates)
    o_ref[...] = _layer_norm(ALPHA * x + acc, g_ref[...], b_ref[...])
    gate_s[...] = gates

    @pl.when(part == steps_per_seq - 1)
    def _():
        g_work = jnp.where(lane < n_blk, gates, -jnp.inf)
        out = jnp.zeros((H_ATT, 128), jnp.int32)
        for r in range(MOBA_TOPK):
            mx = jnp.max(g_work, axis=1, keepdims=True)
            idx = jnp.min(jnp.where(g_work == mx, lane, 128), axis=1, keepdims=True)
            out = jnp.where(lane == r, idx, out)
            g_work = jnp.where(lane == idx, -jnp.inf, g_work)
        top_ref[0] = out


def _mlp_scores(x, w1, w2, g, b, q, kt_pool, page_table, tm):
    m = x.shape[0]
    bd, n_pages = page_table.shape
    assert m % (bd * tm) == 0 and n_pages % PAGES_PER_STEP == 0 and PAGES_PER_STEP % PAGES_PER_BLOCK == 0
    sps = m // (bd * tm)
    n_groups = n_pages // PAGES_PER_STEP
    assert n_groups % sps == 0 and (n_groups // sps) % 2 == 0 and len(_col_chunks(D_FF)) % (n_groups // sps) == 0
    assert (PAGES_PER_STEP // PAGES_PER_BLOCK) % (len(_col_chunks(D_FF)) // (n_groups // sps)) == 0
    assert MOBA_TOPK <= n_pages // PAGES_PER_BLOCK <= 128
    resident = lambda shape: pl.BlockSpec(shape, lambda i, pt: (0,) * len(shape), pipeline_mode=pl.Buffered(1))
    grid_spec = pltpu.PrefetchScalarGridSpec(
        num_scalar_prefetch=1,
        grid=(bd * sps,),
        in_specs=[pl.BlockSpec((tm, D_MODEL), lambda i, pt: (i, 0)), resident(w1.shape), resident(w2.shape),
                  resident((1, D_MODEL)), resident((1, D_MODEL)),
                  pl.BlockSpec((1, H_ATT, HEAD_DIM), lambda i, pt: (i // sps, 0, 0)),
                  pl.BlockSpec(memory_space=pl.ANY)],
        out_specs=[pl.BlockSpec((tm, D_MODEL), lambda i, pt: (i, 0)),
                   pl.BlockSpec((1, n_pages // sps, H_ATT, PAGE_SIZE), lambda i, pt: (i // sps, i % sps, 0, 0)),
                   pl.BlockSpec((1, H_ATT, 128), lambda i, pt: (i // sps, 0, 0))],
        scratch_shapes=[pltpu.VMEM((2, PAGES_PER_STEP, H_ATT, HEAD_DIM, PAGE_SIZE), F32),
                        pltpu.SemaphoreType.DMA((2,)),
                        pltpu.VMEM((H_ATT, HEAD_DIM, PAGE_SIZE), F32), pltpu.VMEM((H_ATT, 128), F32)],
    )
    return pl.pallas_call(
        functools.partial(_mlp_scores_body, n_pages=n_pages, steps_per_seq=sps),
        grid_spec=grid_spec,
        out_shape=[jax.ShapeDtypeStruct((m, D_MODEL), F32),
                   jax.ShapeDtypeStruct((bd, n_pages, H_ATT, PAGE_SIZE), F32),
                   jax.ShapeDtypeStruct((bd, H_ATT, 128), jnp.int32)],
        compiler_params=_cparams("arbitrary"),
    )(page_table.reshape(-1), x, w1, w2, g, b, q, kt_pool)


def _moba_sample_body(pt_ref, top_ref, q_ref, kn_ref, vn_ref, s_ref, vt_hbm, o_ref, vbuf, sem, *, past, n_pages):
    b = pl.program_id(0)
    units = [(h, j) for h in range(H_ATT) for j in range(N_SEL_PAGES)]

    def seq_page(seq, h, j):
        return top_ref[(seq * H_ATT + h) * MOBA_TOPK + j // PAGES_PER_BLOCK] * PAGES_PER_BLOCK + j % PAGES_PER_BLOCK

    def tile_copies(seq, slot):
        return [pltpu.make_async_copy(vt_hbm.at[pt_ref[seq * n_pages + seq_page(seq, h, j)], h],
                                      vbuf.at[slot, h * N_SEL_PAGES + j], sem.at[slot]) for h, j in units]

    @pl.when(b == 0)
    def _():
        for cp in tile_copies(0, 0):
            cp.start()

    @pl.when(b + 1 < pl.num_programs(0))
    def _():
        for cp in tile_copies(b + 1, (b + 1) % 2):
            cp.start()

    slot = b % 2
    scale = HEAD_DIM ** -0.5
    lane = lax.broadcasted_iota(jnp.int32, (1, PAGE_SIZE), 1)
    logits = []
    for h, j in units:
        page = seq_page(b, h, j)
        dist = (past - (page * PAGE_SIZE + lane)).astype(F32)
        logits.append(s_ref[0, page, h:h + 1, :] * scale - (2.0 ** -(h + 1)) * dist)
    s_self = jnp.sum(q_ref[0] * kn_ref[0], axis=-1, keepdims=True) * scale
    m, p_self = [], []
    for h in range(H_ATT):
        m_h = s_self[h:h + 1]
        for s in logits[h * N_SEL_PAGES:(h + 1) * N_SEL_PAGES]:
            m_h = jnp.maximum(m_h, jnp.max(s, axis=-1, keepdims=True))
        m.append(m_h)
        p_self.append(jnp.exp(s_self[h:h + 1] - m_h))
    p = [jnp.exp(s - m[h]) for s, (h, j) in zip(logits, units)]
    for cp in tile_copies(b, slot):
        cp.wait()
    pv = [_bdot(jnp.broadcast_to(x, (8, PAGE_SIZE)), vbuf[slot, h * N_SEL_PAGES + j], _NT)[0:1]
          for x, (h, j) in zip(p, units)]
    for h in range(H_ATT):
        sel = range(h * N_SEL_PAGES, (h + 1) * N_SEL_PAGES)
        l = p_self[h] + sum(jnp.sum(p[i], axis=-1, keepdims=True) for i in sel)
        acc = p_self[h] * vn_ref[0, h:h + 1, :] + sum(pv[i] for i in sel)
        o_ref[0, h] = acc / l


def _moba_sample(q, k_new, v_new, scores, vt_pool, page_table, top):
    bd, n_pages = page_table.shape
    past = n_pages * PAGE_SIZE
    assert past % MOBA_BLOCK == 0

    def tok_spec():
        return pl.BlockSpec((1, H_ATT, HEAD_DIM), lambda b, pt, tp: (b, 0, 0))

    grid_spec = pltpu.PrefetchScalarGridSpec(
        num_scalar_prefetch=2,
        grid=(bd,),
        in_specs=[tok_spec(), tok_spec(), tok_spec(),
                  pl.BlockSpec((1, n_pages, H_ATT, PAGE_SIZE), lambda b, pt, tp: (b, 0, 0, 0)),
                  pl.BlockSpec(memory_space=pl.ANY)],
        out_specs=pl.BlockSpec((1, H_ATT, 1, HEAD_DIM), lambda b, pt, tp: (b, 0, 0, 0)),
        scratch_shapes=[pltpu.VMEM((2, H_ATT * N_SEL_PAGES, HEAD_DIM, PAGE_SIZE), F32),
                        pltpu.SemaphoreType.DMA((2,))],
    )
    return pl.pallas_call(
        functools.partial(_moba_sample_body, past=past, n_pages=n_pages),
        grid_spec=grid_spec,
        out_shape=jax.ShapeDtypeStruct((bd, H_ATT, 1, HEAD_DIM), F32),
        compiler_params=_cparams("arbitrary"),
    )(page_table.reshape(-1), top.reshape(-1), q, k_new, v_new, scores, vt_pool)


def _softplus(z):
    return jnp.maximum(z, 0.0) + jnp.log(1.0 + jnp.exp(-jnp.abs(z)))


def _sigmoid(z):
    return 1.0 / (1.0 + jnp.exp(-z))


def _rwkv_token_terms(m, w0, w_up, a0, a_up, g_up, k_k, k_a):
    rw = RWKV_WIDTH
    r, k, v = m[:, 0:rw], m[:, rw:2 * rw], m[:, 2 * rw:3 * rw]
    xw = m[:, 3 * rw:3 * rw + W_LORA]
    xa = m[:, 3 * rw + W_LORA:3 * rw + W_LORA + A_LORA]
    xg = m[:, 3 * rw + W_LORA + A_LORA:]
    w_log = -_softplus(-(w0 + _dot(jnp.tanh(xw), w_up, precision=HIGHEST))) - 0.5
    log_decay = -jnp.exp(w_log)
    a = _sigmoid(a0 + _dot(xa, a_up, precision=HIGHEST))
    g = _bdot(_sigmoid(xg), g_up)
    kk = k * k_k
    k2 = k * (1.0 + (a - 1.0) * k_a)
    return r, log_decay, k2, v, kk, a, g


def _head_norm(kk_h):
    return kk_h * lax.rsqrt(jnp.maximum(jnp.sum(kk_h * kk_h, axis=-1, keepdims=True), 1e-24))


def _rwkv_head_out(y, r_h, k_h, v_h, g_h, rk_h, gng_h, gnb_h):
    mu = jnp.mean(y, axis=-1, keepdims=True)
    d = y - mu
    var = jnp.mean(d * d, axis=-1, keepdims=True)
    yn = d * lax.rsqrt(var + GN_EPS) * gng_h + gnb_h
    bonus = jnp.sum(r_h * k_h * rk_h, axis=-1, keepdims=True) * v_h
    return (yn + bonus) * g_h


PAIR = 2 * RWKV_HEAD


def _pair_blockdiag(y):
    left = lax.broadcasted_iota(jnp.int32, y.shape, 1) < RWKV_HEAD
    zero = jnp.zeros_like(y)
    return jnp.concatenate([jnp.where(left, y, zero), jnp.where(left, zero, y)], axis=0)


def _pair_nn(x, y):
    return _dot(x.astype(BF16), _pair_blockdiag(y.astype(BF16)))


def _pair_nt(x, y):
    return _dot(x.astype(BF16), _pair_blockdiag(y.astype(BF16)), _NT)


def _pair_tn(x, y):
    full = _dot(x.astype(BF16), y.astype(BF16), _TN)
    left = lax.broadcasted_iota(jnp.int32, (RWKV_HEAD, PAIR), 1) < RWKV_HEAD
    return jnp.where(left, full[:RWKV_HEAD], full[RWKV_HEAD:])


def _pair_sum(x):
    left = lax.broadcasted_iota(jnp.int32, x.shape, 1) < RWKV_HEAD
    s_a = jnp.sum(jnp.where(left, x, 0.0), axis=-1, keepdims=True)
    s_b = jnp.sum(jnp.where(left, 0.0, x), axis=-1, keepdims=True)
    return jnp.where(left, s_a, s_b)


def _rwkv_pair_out(y, r_p, k_p, v_p, g_p, rk_p, gng_p, gnb_p):
    inv_n = 1.0 / RWKV_HEAD
    d = y - _pair_sum(y) * inv_n
    var = _pair_sum(d * d) * inv_n
    yn = d * lax.rsqrt(var + GN_EPS) * gng_p + gnb_p
    return (yn + _pair_sum(r_p * k_p * rk_p) * v_p) * g_p


def _rwkv_prompt_body(rw_ref, prev_ref, s0_ref, mu_ref, w0_ref, wup_ref, a0_ref, aup_ref, gup_ref,
                      kk_ref, ka_ref, rk_ref, gng_ref, gnb_ref, y_ref, sfin_ref, state_s, prev_s):
    c = pl.program_id(1)
    ch = RWKV_CHUNK
    tile = rw_ref.shape[1]
    n_ch = tile // ch
    n_pairs = H_RWKV // 2
    pairs = [slice(p * PAIR, (p + 1) * PAIR) for p in range(n_pairs)]

    @pl.when(c == 0)
    def _():
        for p in range(n_pairs):
            state_s[p] = jnp.concatenate([s0_ref[0, (2 * p) * RWKV_HEAD:(2 * p + 1) * RWKV_HEAD, :],
                                          s0_ref[0, (2 * p + 1) * RWKV_HEAD:(2 * p + 2) * RWKV_HEAD, :]], axis=1)
        prev_s[...] = prev_ref[0]

    rw = rw_ref[0]
    row = lax.broadcasted_iota(jnp.int32, rw.shape, 0)
    rw_prev = jnp.where(row == 0, prev_s[...], pltpu.roll(rw, 1, 0))
    prev_s[...] = rw[tile - 1:tile, :]
    m = rw + (rw_prev - rw) * mu_ref[...]
    r, log_decay, k2, v, kk, a, g = _rwkv_token_terms(
        m, w0_ref[...], wup_ref[...], a0_ref[...], aup_ref[...], gup_ref[...], kk_ref[...], ka_ref[...])

    ti = lax.broadcasted_iota(jnp.int32, (tile, tile), 0)
    si = lax.broadcasted_iota(jnp.int32, (tile, tile), 1)
    in_chunk_lower = (si <= ti) & (si // ch == ti // ch)
    cs = _dot(in_chunk_lower.astype(F32), log_decay, precision=HIGHEST)
    cs_end = cs[ch - 1:ch, :]
    rows = lax.broadcasted_iota(jnp.int32, cs.shape, 0)
    for j in range(1, n_ch):
        cs_end = jnp.where(rows >= j * ch, cs[(j + 1) * ch - 1:(j + 1) * ch, :], cs_end)
    gam = jnp.exp(cs)
    gam_prev = jnp.exp(cs - log_decay)
    gam_inv = jnp.exp(-cs)
    gam_tail = jnp.exp(cs_end - cs)

    t_p = lax.broadcasted_iota(jnp.int32, (ch, PAIR), 0)
    s_p = lax.broadcasted_iota(jnp.int32, (ch, PAIR), 1) % RWKV_HEAD
    lower = s_p <= t_p
    strict = s_p < t_p
    eye = (s_p == t_p).astype(F32)

    units = [(slice(j * ch, (j + 1) * ch), ps) for j in range(n_ch) for ps in pairs]
    kk_n = [kk[ts, ps] for ts, ps in units]
    kk_n = [x * lax.rsqrt(jnp.maximum(_pair_sum(x * x), 1e-24)) for x in kk_n]
    a_t = [-n * gam_prev[ts, ps] for n, (ts, ps) in zip(kk_n, units)]
    b_h = [n * a[ts, ps] for n, (ts, ps) in zip(kk_n, units)]
    b_t = [x * gam_inv[ts, ps] for x, (ts, ps) in zip(b_h, units)]
    k_t = [k2[ts, ps] * gam_inv[ts, ps] for ts, ps in units]
    r_t = [r[ts, ps] * gam[ts, ps] for ts, ps in units]
    v_h = [v[ts, ps] for ts, ps in units]
    l_ab = [jnp.where(strict, _pair_nt(x, z), 0.0) for x, z in zip(a_t, b_t)]
    l_ak = [jnp.where(strict, _pair_nt(x, z), 0.0) for x, z in zip(a_t, k_t)]
    m_rb = [jnp.where(lower, _pair_nt(x, z), 0.0) for x, z in zip(r_t, b_t)]
    m_rk = [jnp.where(lower, _pair_nt(x, z), 0.0) for x, z in zip(r_t, k_t)]
    inv = [eye + x for x in l_ab]
    pw = list(l_ab)
    span = 2
    while span < ch:
        pw = [_pair_nn(x, x) for x in pw]
        inv = [i + _pair_nn(i, x) for i, x in zip(inv, pw)]
        span *= 2
    lv = [_pair_nn(l, vv) for l, vv in zip(l_ak, v_h)]
    t_a = [_pair_nn(i, x) for i, x in zip(inv, a_t)]
    t_l = [_pair_nn(i, x) for i, x in zip(inv, lv)]
    b_e = [x * gam_tail[u_] for x, u_ in zip(b_h, units)]
    k_e = [k2[u_] * gam_tail[u_] for u_ in units]
    r_a = [x + _pair_nn(mb, ta) for x, mb, ta in zip(r_t, m_rb, t_a)]
    y_0 = [_pair_nn(mb, tl) + _pair_nn(mk, vv) for mb, tl, mk, vv in zip(m_rb, t_l, m_rk, v_h)]
    m_s = [_pair_tn(ta, be) for ta, be in zip(t_a, b_e)]
    c_s = [_pair_tn(tl, be) + _pair_tn(vv, ke) for tl, be, vv, ke in zip(t_l, b_e, v_h, k_e)]
    s_cur = [state_s[p] for p in range(n_pairs)]
    for j in range(n_ch):
        sel = range(j * n_pairs, (j + 1) * n_pairs)
        y = [_pair_nt(r_a[i], s) + y_0[i] for i, s in zip(sel, s_cur)]
        s_cur = [s * gam[(j + 1) * ch - 1:(j + 1) * ch, units[i][1]] + _pair_nn(s, m_s[i]) + c_s[i]
                 for i, s in zip(sel, s_cur)]
        for i, yy in zip(sel, y):
            ts, ps = units[i]
            y_ref[0, ts, ps] = _rwkv_pair_out(yy, r[ts, ps], k2[ts, ps], v_h[i], g[ts, ps],
                                              rk_ref[:, ps], gng_ref[:, ps], gnb_ref[:, ps])
    for p in range(n_pairs):
        state_s[p] = s_cur[p]

    @pl.when(c == pl.num_programs(1) - 1)
    def _():
        for p in range(n_pairs):
            sfin_ref[0, (2 * p) * RWKV_HEAD:(2 * p + 1) * RWKV_HEAD, :] = state_s[p][:, :RWKV_HEAD]
            sfin_ref[0, (2 * p + 1) * RWKV_HEAD:(2 * p + 2) * RWKV_HEAD, :] = state_s[p][:, RWKV_HEAD:]


def _rwkv_prompt(rw, prev_row, s0, p, tile):
    bsz, t, _ = rw.shape
    assert t % tile == 0 and tile % RWKV_CHUNK == 0 and H_RWKV % 2 == 0
    hn = H_RWKV * RWKV_HEAD
    vec = lambda n: _full((1, n))
    return pl.pallas_call(
        _rwkv_prompt_body,
        grid=(bsz, t // tile),
        in_specs=[pl.BlockSpec((1, tile, RWKV_COLS), lambda i, j: (i, j, 0)),
                  pl.BlockSpec((1, 1, RWKV_COLS), lambda i, j: (i, 0, 0)),
                  pl.BlockSpec((1, hn, RWKV_HEAD), lambda i, j: (i, 0, 0)),
                  vec(RWKV_COLS), vec(RWKV_WIDTH), _full((W_LORA, RWKV_WIDTH)), vec(RWKV_WIDTH),
                  _full((A_LORA, RWKV_WIDTH)), _full((G_LORA, RWKV_WIDTH)),
                  vec(RWKV_WIDTH), vec(RWKV_WIDTH), vec(RWKV_WIDTH), vec(RWKV_WIDTH), vec(RWKV_WIDTH)],
        out_specs=[pl.BlockSpec((1, tile, RWKV_WIDTH), lambda i, j: (i, j, 0)),
                   pl.BlockSpec((1, hn, RWKV_HEAD), lambda i, j: (i, 0, 0))],
        out_shape=[jax.ShapeDtypeStruct((bsz, t, RWKV_WIDTH), F32),
                   jax.ShapeDtypeStruct((bsz, hn, RWKV_HEAD), F32)],
        scratch_shapes=[pltpu.VMEM((H_RWKV // 2, RWKV_HEAD, PAIR), F32), pltpu.VMEM((1, RWKV_COLS), F32)],
        compiler_params=_cparams("parallel", "arbitrary"),
    )(rw, prev_row, s0, p["mu_shift"], p["w0"], p["w_up"], p["a0"], p["a_up"], p["g_up"],
      p["k_k"], p["k_a"], p["r_k"], p["gn_g"], p["gn_b"])


def _rows8(*rows):
    ri = lax.broadcasted_iota(jnp.int32, (8, rows[0].shape[1]), 0)
    out = jnp.zeros((8, rows[0].shape[1]), F32)
    for i, x in enumerate(rows):
        out = jnp.where(ri == i, x, out)
    return out


STEP_ROWS = 8


def _rwkv_step_body(rw_ref, prev_ref, s0_ref, mu_ref, w0_ref, wup_ref, a0_ref, aup_ref, gup_ref,
                    kk_ref, ka_ref, rk_ref, gng_ref, gnb_ref, y_ref, sfin_ref):
    rw = rw_ref[...]
    m = rw + (prev_ref[...] - rw) * mu_ref[...]
    r, log_decay, k2, v, kk, a, g = _rwkv_token_terms(
        m, w0_ref[...], wup_ref[...], a0_ref[...], aup_ref[...], gup_ref[...], kk_ref[...], ka_ref[...])
    decay = jnp.exp(log_decay)
    heads = [slice(h * RWKV_HEAD, (h + 1) * RWKV_HEAD) for h in range(H_RWKV)]
    kkn = [_head_norm(kk[:, hs]) for hs in heads]
    b_h = [n * a[:, hs] for n, hs in zip(kkn, heads)]
    units = [(i, h) for i in range(STEP_ROWS) for h in range(H_RWKV)]
    row = lambda x, i: x[i:i + 1, :]
    s_0 = [s0_ref[i, heads[h], :] for i, h in units]
    sa = [_dot(_rows8(-row(kkn[h], i)), s, _NT, HIGHEST)[0:1] for (i, h), s in zip(units, s_0)]
    s_1 = [s * row(decay[:, heads[h]], i)
           + _dot(_rows8(x, row(v[:, heads[h]], i)), _rows8(row(b_h[h], i), row(k2[:, heads[h]], i)), _TN, HIGHEST)
           for (i, h), s, x in zip(units, s_0, sa)]
    y = [_dot(_rows8(row(r[:, heads[h]], i)), s, _NT, HIGHEST)[0:1] for (i, h), s in zip(units, s_1)]
    for (i, h), s in zip(units, s_1):
        sfin_ref[i, heads[h], :] = s
    seq = lax.broadcasted_iota(jnp.int32, (STEP_ROWS, RWKV_HEAD), 0)
    for h, hs in enumerate(heads):
        y_h = jnp.zeros((STEP_ROWS, RWKV_HEAD), F32)
        for i in range(STEP_ROWS):
            y_h = jnp.where(seq == i, y[i * H_RWKV + h], y_h)
        y_ref[:, hs] = _rwkv_head_out(y_h, r[:, hs], k2[:, hs], v[:, hs], g[:, hs],
                                      rk_ref[:, hs], gng_ref[:, hs], gnb_ref[:, hs])


def _rwkv_step(rw, prev_row, s0, p):
    bd = rw.shape[0]
    assert bd % STEP_ROWS == 0
    hn = H_RWKV * RWKV_HEAD
    vec = lambda n: _full((1, n))
    rows = lambda n: pl.BlockSpec((STEP_ROWS, n), lambda i: (i, 0))
    state = pl.BlockSpec((STEP_ROWS, hn, RWKV_HEAD), lambda i: (i, 0, 0))
    return pl.pallas_call(
        _rwkv_step_body,
        grid=(bd // STEP_ROWS,),
        in_specs=[rows(RWKV_COLS), rows(RWKV_COLS), state,
                  vec(RWKV_COLS), vec(RWKV_WIDTH), _full((W_LORA, RWKV_WIDTH)), vec(RWKV_WIDTH),
                  _full((A_LORA, RWKV_WIDTH)), _full((G_LORA, RWKV_WIDTH)),
                  vec(RWKV_WIDTH), vec(RWKV_WIDTH), vec(RWKV_WIDTH), vec(RWKV_WIDTH), vec(RWKV_WIDTH)],
        out_specs=[rows(RWKV_WIDTH), state],
        out_shape=[jax.ShapeDtypeStruct((bd, RWKV_WIDTH), F32),
                   jax.ShapeDtypeStruct((bd, hn, RWKV_HEAD), F32)],
        compiler_params=_cparams("parallel"),
    )(rw, prev_row, s0, p["mu_shift"], p["w0"], p["w_up"], p["a0"], p["a_up"], p["g_up"],
      p["k_k"], p["k_a"], p["r_k"], p["gn_g"], p["gn_b"])


def _row(x):
    return x.reshape(1, -1)


def kernel(x_prompt, x_sample, mem_prompt, cache_k_pool, cache_v_pool, page_table, cache_mem_k, cache_mem_v, state_wkv, state_shift, w_in, mu_shift, w0, w_up, a0, a_up, g_up, k_k, k_a, r_k, gn_g, gn_b, w_out, ln1_g, ln1_b, w_q_mem, w_kv_mem, w_o_mem, ln2_g, ln2_b, w_ff1, w_ff2, ln3_g, ln3_b):
    assert w_in.shape[0] == DEPTH == 1
    bsz, t, _ = x_prompt.shape
    bd = x_sample.shape[0]
    hn = H_RWKV * RWKV_HEAD

    w_in_b = w_in[0].astype(BF16)
    w_out_b = w_out[0].astype(BF16)
    wq_b = w_q_mem[0].astype(BF16)
    wkv_b = w_kv_mem[0].astype(BF16)
    wo_b = w_o_mem[0].astype(BF16)
    w1_b = w_ff1[0].astype(BF16)
    w2_b = w_ff2[0].astype(BF16)
    ln1 = (_row(ln1_g[0]), _row(ln1_b[0]))
    ln2 = (_row(ln2_g[0]), _row(ln2_b[0]))
    ln3 = (_row(ln3_g[0]), _row(ln3_b[0]))
    rwkv_p = dict(mu_shift=_row(mu_shift[0]), w0=_row(w0[0]), w_up=w_up[0], a0=_row(a0[0]), a_up=a_up[0],
                  g_up=g_up[0], k_k=_row(k_k[0]), k_a=_row(k_a[0]), r_k=_row(r_k[0]),
                  gn_g=_row(gn_g[0]), gn_b=_row(gn_b[0]))

    qt_p, kt_p, vt_p, kb_p, rw_p3, kmean_p = _proj_prompt(x_prompt, w_in_b, tm=512)
    y_att_p = _moba_prompt(qt_p, kb_p, vt_p, kmean_p)
    y_rw_p, s_p = _rwkv_prompt(rw_p3, jnp.zeros((bsz, 1, RWKV_COLS), F32), jnp.zeros((bsz, hn, RWKV_HEAD), F32), rwkv_p,
                               tile=RWKV_TILE)
    mk_p, mv_p = _mm_multi(mem_prompt.reshape(bsz * N_MEM, D_MODEL), wkv_b, (D_MODEL, D_MODEL), tm=512)
    x2_p = _mix_memx_prompt(x_prompt, y_att_p, y_rw_p, mk_p.reshape(bsz, N_MEM, D_MODEL),
                            mv_p.reshape(bsz, N_MEM, D_MODEL), w_out_b[:ATT_WIDTH], w_out_b[ATT_WIDTH:],
                            wq_b, wo_b, ln1, ln2, tm=512)

    kt_pool = cache_k_pool[0].transpose(0, 2, 3, 1)
    vt_pool = cache_v_pool[0].transpose(0, 2, 3, 1)
    xs = x_sample.reshape(bd, D_MODEL)
    q_s = _mm_f32(xs, w_in[0], ATT_WIDTH)
    k_s, v_s, rw_s = _mm_multi(xs, w_in_b, (ATT_WIDTH, ATT_WIDTH, RWKV_COLS), tm=bd, first_col=ATT_WIDTH)
    q_s3, k_s3, v_s3 = (z.reshape(bd, H_ATT, HEAD_DIM) for z in (q_s, k_s, v_s))
    y_p, scores_s, top = _mlp_scores(x2_p.reshape(bsz * t, D_MODEL), w1_b, w2_b, *ln3, q_s3, kt_pool, page_table,
                                     tm=MLP_TM)
    y_att_s = _moba_sample(q_s3, k_s3, v_s3, scores_s, vt_pool, page_table, top[:, :, :MOBA_TOPK])
    y_rw_s, s_s = _rwkv_step(rw_s, state_shift[0], state_wkv[0].reshape(bd, hn, RWKV_HEAD), rwkv_p)
    x1_s = _mm_res_ln([y_att_s.reshape(bd, ATT_WIDTH), y_rw_s],
                      [w_out_b[:ATT_WIDTH], w_out_b[ATT_WIDTH:]], xs, *ln1, tm=bd)
    (qm_s,) = _mm_multi(x1_s, wq_b, (D_MODEL,), tm=bd)
    att_s = _memx_sample(qm_s.reshape(bd, 1, D_MODEL), cache_mem_k[0], cache_mem_v[0])
    x2_s = _mm_res_ln([att_s.reshape(bd, D_MODEL)], [wo_b], x1_s, *ln2, tm=bd)
    y_s = _mlp(x2_s, w1_b, w2_b, *ln3, tm=bd)

    return (y_p.reshape(bsz, t, D_MODEL), y_s.reshape(bd, 1, D_MODEL),
            kt_p.reshape(bsz, H_ATT, HEAD_DIM, t).transpose(0, 3, 1, 2)[None],
            vt_p.reshape(bsz, H_ATT, HEAD_DIM, t).transpose(0, 3, 1, 2)[None],
            k_s.reshape(1, bd, 1, H_ATT, HEAD_DIM), v_s.reshape(1, bd, 1, H_ATT, HEAD_DIM),
            mk_p.reshape(1, bsz, N_MEM, MEM_HEADS, MEM_HEAD_DIM), mv_p.reshape(1, bsz, N_MEM, MEM_HEADS, MEM_HEAD_DIM),
            s_p.reshape(1, bsz, H_RWKV, RWKV_HEAD, RWKV_HEAD), rw_p3[:, -1][None],
            s_s.reshape(1, bd, H_RWKV, RWKV_HEAD, RWKV_HEAD), rw_s[None])
```

```python
import functools

import jax
import jax.numpy as jnp
from jax import lax
from jax.experimental import pallas as pl
from jax.experimental.pallas import tpu as pltpu

F32 = jnp.float32
BF16 = jnp.bfloat16
HIGHEST = lax.Precision.HIGHEST

D_MODEL = 1024
PAGE_SIZE = 128
ATT_WIDTH = 512
RWKV_WIDTH = 512
HEAD_DIM = 64
H_ATT = 8
MOBA_BLOCK = 256
MOBA_TOPK = 3
RWKV_HEAD = 64
H_RWKV = 8
W_LORA = 64
A_LORA = 64
G_LORA = 128
RWKV_COLS = 3 * RWKV_WIDTH + W_LORA + A_LORA + G_LORA
PROJ_COLS = 3 * ATT_WIDTH + RWKV_COLS
N_MEM = 256
MEM_HEADS = 4
MEM_HEAD_DIM = 256
D_FF = 4 * D_MODEL
LN_EPS = 1e-5
GN_EPS = 64e-5
DEPTH = 1
ALPHA = (2.0 * DEPTH) ** 0.25

MLP_TM = 512
RWKV_TILE = 256
RWKV_CHUNK = 64
NEG = -1e30
VMEM_LIMIT = 56 * 1024 * 1024

_NT = (((1,), (1,)), ((), ()))
_TN = (((0,), (0,)), ((), ()))


def _cparams(*sem):
    return pltpu.CompilerParams(dimension_semantics=sem, vmem_limit_bytes=VMEM_LIMIT)


def _dot(a, b, dims=None, precision=None):
    if dims is None:
        return jnp.dot(a, b, preferred_element_type=F32, precision=precision)
    return lax.dot_general(a, b, dims, preferred_element_type=F32, precision=precision)


def _bdot(a, b, dims=None):
    return _dot(a.astype(BF16), b.astype(BF16), dims)


def _layer_norm(z, g, b):
    mu = jnp.mean(z, axis=-1, keepdims=True)
    d = z - mu
    var = jnp.mean(d * d, axis=-1, keepdims=True)
    return d * lax.rsqrt(var + LN_EPS) * g + b


def _full(shape):
    n = len(shape)
    return pl.BlockSpec(shape, lambda *_: (0,) * n)


def _col_chunks(n, width=512):
    out, c = [], 0
    while c < n:
        w = min(width, n - c)
        out.append((c, w))
        c += w
    return out


def _mm_multi_body(x_ref, w_ref, *o_refs, col0s):
    xb = x_ref[...].astype(BF16)
    for o_ref, c0 in zip(o_refs, col0s):
        for c, w in _col_chunks(o_ref.shape[1]):
            o_ref[:, c:c + w] = _dot(xb, w_ref[:, c0 + c:c0 + c + w])


def _mm_f32_body(x_ref, w_ref, o_ref):
    o_ref[...] = _dot(x_ref[...], w_ref[...], precision=HIGHEST)


def _mm_f32(x, w, n_cols):
    m, k = x.shape
    return pl.pallas_call(
        _mm_f32_body,
        grid=(1,),
        in_specs=[_full(x.shape), pl.BlockSpec((k, n_cols), lambda i: (0, 0))],
        out_specs=_full((m, n_cols)),
        out_shape=jax.ShapeDtypeStruct((m, n_cols), F32),
        compiler_params=_cparams("arbitrary"),
    )(x, w)


def _mm_multi(x, w_bf16, widths, tm, first_col=0):
    m, k = x.shape
    col0s, c = [], first_col
    for wd in widths:
        col0s.append(c)
        c += wd
    assert c == w_bf16.shape[1] and m % tm == 0
    return pl.pallas_call(
        functools.partial(_mm_multi_body, col0s=tuple(col0s)),
        grid=(m // tm,),
        in_specs=[pl.BlockSpec((tm, k), lambda i: (i, 0)), _full(w_bf16.shape)],
        out_specs=[pl.BlockSpec((tm, wd), lambda i: (i, 0)) for wd in widths],
        out_shape=[jax.ShapeDtypeStruct((m, wd), F32) for wd in widths],
        compiler_params=_cparams("parallel"),
    )(x, w_bf16)


def _proj_prompt_body(x_ref, w_ref, qt_ref, kt_ref, vt_ref, kb_ref, rw_ref, kmean_ref):
    j = pl.program_id(1)
    tm = x_ref.shape[1]
    xb = x_ref[0].astype(BF16)
    aw = ATT_WIDTH
    q = _dot(xb, w_ref[:, 0:aw])
    k = _dot(xb, w_ref[:, aw:2 * aw])
    v = _dot(xb, w_ref[:, 2 * aw:3 * aw])
    qt_ref[0] = q.T
    kt_ref[0] = k.T
    vt_ref[0] = v.T
    kb_ref[0] = k.astype(BF16)
    for c, w in _col_chunks(RWKV_COLS):
        rw_ref[0, :, c:c + w] = _dot(xb, w_ref[:, 3 * aw + c:3 * aw + c + w])
    blk_per_tile = tm // MOBA_BLOCK
    for i in range(blk_per_tile):
        kmean_ref[0, pl.ds(j * blk_per_tile + i, 1), :] = jnp.mean(
            k[i * MOBA_BLOCK:(i + 1) * MOBA_BLOCK], axis=0, keepdims=True)


def _proj_prompt(x, w_bf16, tm):
    bsz, t, _ = x.shape
    assert t % tm == 0 and tm % MOBA_BLOCK == 0
    n_blk = t // MOBA_BLOCK
    tr_spec = pl.BlockSpec((1, ATT_WIDTH, tm), lambda i, j: (i, 0, j))
    tr_shape = jax.ShapeDtypeStruct((bsz, ATT_WIDTH, t), F32)
    return pl.pallas_call(
        _proj_prompt_body,
        grid=(bsz, t // tm),
        in_specs=[pl.BlockSpec((1, tm, D_MODEL), lambda i, j: (i, j, 0)), _full(w_bf16.shape)],
        out_specs=[tr_spec, tr_spec, tr_spec,
                   pl.BlockSpec((1, tm, ATT_WIDTH), lambda i, j: (i, j, 0)),
                   pl.BlockSpec((1, tm, RWKV_COLS), lambda i, j: (i, j, 0)),
                   pl.BlockSpec((1, n_blk, ATT_WIDTH), lambda i, j: (i, 0, 0))],
        out_shape=[tr_shape, tr_shape, tr_shape,
                   jax.ShapeDtypeStruct((bsz, t, ATT_WIDTH), BF16),
                   jax.ShapeDtypeStruct((bsz, t, RWKV_COLS), F32),
                   jax.ShapeDtypeStruct((bsz, n_blk, ATT_WIDTH), F32)],
        compiler_params=_cparams("parallel", "arbitrary"),
    )(x, w_bf16)


def _mm_res_ln_body(*refs, n_in):
    x_refs = refs[:n_in]
    w_refs = refs[n_in:2 * n_in]
    res_ref, g_ref, b_ref, o_ref = refs[2 * n_in:]
    acc = _dot(x_refs[0][...].astype(BF16), w_refs[0][...])
    for x_ref, w_ref in zip(x_refs[1:], w_refs[1:]):
        acc = acc + _dot(x_ref[...].astype(BF16), w_ref[...])
    o_ref[...] = _layer_norm(ALPHA * res_ref[...] + acc, g_ref[...], b_ref[...])


def _mm_res_ln(xs, ws, res, g, b, tm):
    m = res.shape[0]
    n_in = len(xs)
    assert m % tm == 0
    in_specs = [pl.BlockSpec((tm, x.shape[1]), lambda i: (i, 0)) for x in xs]
    in_specs += [_full(w.shape) for w in ws]
    in_specs += [pl.BlockSpec((tm, D_MODEL), lambda i: (i, 0)), _full((1, D_MODEL)), _full((1, D_MODEL))]
    return pl.pallas_call(
        functools.partial(_mm_res_ln_body, n_in=n_in),
        grid=(m // tm,),
        in_specs=in_specs,
        out_specs=pl.BlockSpec((tm, D_MODEL), lambda i: (i, 0)),
        out_shape=jax.ShapeDtypeStruct((m, D_MODEL), F32),
        compiler_params=_cparams("parallel"),
    )(*xs, *ws, res, g, b)


def _mlp_body(x_ref, w1_ref, w2_ref, g_ref, b_ref, o_ref):
    x = x_ref[...]
    xb = x.astype(BF16)
    acc = jnp.zeros(x.shape, F32)
    for c, w in _col_chunks(D_FF):
        h = jnp.maximum(_dot(xb, w1_ref[:, c:c + w]), 0.0)
        acc = acc + _dot((h * h).astype(BF16), w2_ref[c:c + w, :])
    o_ref[...] = _layer_norm(ALPHA * x + acc, g_ref[...], b_ref[...])


def _mlp(x, w1, w2, g, b, tm):
    m = x.shape[0]
    assert m % tm == 0
    return pl.pallas_call(
        _mlp_body,
        grid=(m // tm,),
        in_specs=[pl.BlockSpec((tm, D_MODEL), lambda i: (i, 0)), _full(w1.shape), _full(w2.shape),
                  _full((1, D_MODEL)), _full((1, D_MODEL))],
        out_specs=pl.BlockSpec((tm, D_MODEL), lambda i: (i, 0)),
        out_shape=jax.ShapeDtypeStruct((m, D_MODEL), F32),
        compiler_params=_cparams("parallel"),
    )(x, w1, w2, g, b)


def _mem_kv_body(x_ref, w_ref, k2_ref, v2_ref, k4_ref, v4_ref):
    xb = x_ref[0].astype(BF16)
    for o2_ref, o4_ref, c0 in ((k2_ref, k4_ref, 0), (v2_ref, v4_ref, D_MODEL)):
        for h in range(MEM_HEADS):
            cols = slice(h * MEM_HEAD_DIM, (h + 1) * MEM_HEAD_DIM)
            part = _dot(xb, w_ref[:, c0 + h * MEM_HEAD_DIM:c0 + (h + 1) * MEM_HEAD_DIM])
            o2_ref[0, :, cols] = part
            o4_ref[0, :, h, :] = part


def _mem_kv(mem, w_bf16):
    bsz = mem.shape[0]
    flat = pl.BlockSpec((1, N_MEM, D_MODEL), lambda i: (i, 0, 0))
    heads = pl.BlockSpec((1, N_MEM, MEM_HEADS, MEM_HEAD_DIM), lambda i: (i, 0, 0, 0))
    flat_t = jax.ShapeDtypeStruct((bsz, N_MEM, D_MODEL), F32)
    heads_t = jax.ShapeDtypeStruct((bsz, N_MEM, MEM_HEADS, MEM_HEAD_DIM), F32)
    return pl.pallas_call(
        _mem_kv_body,
        grid=(bsz,),
        in_specs=[flat, _full(w_bf16.shape)],
        out_specs=[flat, flat, heads, heads],
        out_shape=[flat_t, flat_t, heads_t, heads_t],
        compiler_params=_cparams("parallel"),
    )(mem, w_bf16)


def _mem_attend(q, k_head, v_head):
    heads = range(MEM_HEADS)
    s = [_bdot(q[:, h * MEM_HEAD_DIM:(h + 1) * MEM_HEAD_DIM], k_head(h), _NT) for h in heads]
    p = [jnp.exp(x - jnp.max(x, axis=-1, keepdims=True)) for x in s]
    l = [jnp.sum(x, axis=-1, keepdims=True) for x in p]
    return jnp.concatenate([_bdot(x, v_head(h)) / z for h, x, z in zip(heads, p, l)], axis=-1)


def _mix_memx_prompt_body(x_ref, ya_ref, yr_ref, mk_ref, mv_ref, wa_ref, wr_ref, wq_ref, wo_ref,
                          g1_ref, b1_ref, g2_ref, b2_ref, o_ref):
    mix = _dot(ya_ref[0].astype(BF16), wa_ref[...]) + _dot(yr_ref[0].astype(BF16), wr_ref[...])
    x1 = _layer_norm(ALPHA * x_ref[0] + mix, g1_ref[...], b1_ref[...])
    q = _dot(x1.astype(BF16), wq_ref[...]) * (MEM_HEAD_DIM ** -0.5)
    cols = lambda h: slice(h * MEM_HEAD_DIM, (h + 1) * MEM_HEAD_DIM)
    att = _mem_attend(q, lambda h: mk_ref[0, :, cols(h)], lambda h: mv_ref[0, :, cols(h)])
    acc = _dot(att.astype(BF16), wo_ref[...])
    o_ref[0] = _layer_norm(ALPHA * x1 + acc, g2_ref[...], b2_ref[...])


def _mix_memx_prompt(x, y_att, y_rw, mk, mv, w_att, w_rw, wq, wo, ln1, ln2, tm):
    bsz, t, _ = x.shape
    assert t % tm == 0
    row = lambda w: pl.BlockSpec((1, tm, w), lambda i, j: (i, j, 0))
    mem = pl.BlockSpec((1, N_MEM, D_MODEL), lambda i, j: (i, 0, 0))
    vec = _full((1, D_MODEL))
    return pl.pallas_call(
        _mix_memx_prompt_body,
        grid=(bsz, t // tm),
        in_specs=[row(D_MODEL), row(ATT_WIDTH), row(RWKV_WIDTH), mem, mem,
                  _full(w_att.shape), _full(w_rw.shape), _full(wq.shape), _full(wo.shape), vec, vec, vec, vec],
        out_specs=row(D_MODEL),
        out_shape=jax.ShapeDtypeStruct(x.shape, F32),
        compiler_params=_cparams("parallel", "parallel"),
    )(x, y_att, y_rw, mk, mv, w_att, w_rw, wq, wo, *ln1, *ln2)


def _memx_sample_body(q_ref, mk_ref, mv_ref, o_ref):
    n_rows = N_MEM * MEM_HEADS
    k_all = mk_ref[0].reshape(n_rows, MEM_HEAD_DIM)
    v_all = mv_ref[0].reshape(n_rows, MEM_HEAD_DIM)
    q = q_ref[0] * (MEM_HEAD_DIM ** -0.5)
    q_rows = [q[:, h * MEM_HEAD_DIM:(h + 1) * MEM_HEAD_DIM] for h in range(MEM_HEADS)]
    q8 = jnp.concatenate(q_rows + [jnp.zeros((8 - MEM_HEADS, MEM_HEAD_DIM), F32)], axis=0)
    s = _bdot(q8, k_all, _NT)
    row = lax.broadcasted_iota(jnp.int32, s.shape, 0)
    own = (lax.broadcasted_iota(jnp.int32, s.shape, 1) % MEM_HEADS) == row
    s = jnp.where(own, s, NEG)
    p = jnp.where(own, jnp.exp(s - jnp.max(s, axis=-1, keepdims=True)), 0.0)
    l = jnp.sum(p, axis=-1, keepdims=True)
    o8 = _bdot(p, v_all) / jnp.where(l > 0.0, l, 1.0)
    o_ref[0] = jnp.concatenate([o8[h:h + 1] for h in range(MEM_HEADS)], axis=-1)


def _memx_sample(q, mk, mv):
    bd = q.shape[0]
    mem_spec = pl.BlockSpec((1, N_MEM, MEM_HEADS, MEM_HEAD_DIM), lambda i: (i, 0, 0, 0))
    return pl.pallas_call(
        _memx_sample_body,
        grid=(bd,),
        in_specs=[pl.BlockSpec((1, 1, D_MODEL), lambda i: (i, 0, 0)), mem_spec, mem_spec],
        out_specs=pl.BlockSpec((1, 1, D_MODEL), lambda i: (i, 0, 0)),
        out_shape=jax.ShapeDtypeStruct((bd, 1, D_MODEL), F32),
        compiler_params=_cparams("parallel"),
    )(q, mk, mv)


def _top3_rows(g, n_iota, n_lim):
    g = jnp.where(n_iota < n_lim, g, -jnp.inf)
    n_rows = g.shape[0]
    sel = n_iota < 0
    for _ in range(MOBA_TOPK):
        mx = jnp.max(g, axis=0, keepdims=True)
        idx = jnp.min(jnp.where(g == mx, n_iota, n_rows), axis=0, keepdims=True)
        hit = n_iota == idx
        sel = sel | hit
        g = jnp.where(hit, -jnp.inf, g)
    return sel & (n_iota < n_lim)


MOBA_QT = 256
MASK_BIG = 1e30
FEAT_ROWS = 8


def _moba_prompt_body(qt_ref, kb_ref, vt_ref, kmean_ref, o_ref, vt_s, fe_s, fo_s, acc_s, *, n_blk):
    c = pl.program_id(1)
    half = HEAD_DIM
    lane_k = lax.broadcasted_iota(jnp.int32, (MOBA_BLOCK, 2 * half), 1)
    row_k = lax.broadcasted_iota(jnp.int32, (MOBA_BLOCK, 2 * half), 0)

    @pl.when(c == 0)
    def _():
        for n in range(n_blk):
            vt_s[n] = vt_ref[0, :, n * MOBA_BLOCK:(n + 1) * MOBA_BLOCK].astype(BF16)
            for tbl, f in ((fe_s, lane_k - half), (fo_s, lane_k)):
                feat = jnp.where((f == 0) | (f == 1) | (f == FEAT_ROWS + n), 1.0, 0.0)
                feat = jnp.where(f == 2, row_k.astype(F32), feat)
                feat = jnp.where(f == 3, float(n * MOBA_BLOCK), feat)
                tbl[n] = feat.astype(BF16)

    heads = [slice(h * HEAD_DIM, (h + 1) * HEAD_DIM) for h in range(H_ATT)]
    q_blk = (c * MOBA_QT) // MOBA_BLOCK
    qt = qt_ref[0]
    n_iota = lax.broadcasted_iota(jnp.int32, (n_blk, MOBA_QT), 0)
    gates = [_dot(kmean_ref[0, :, hs], qt[hs, :], precision=HIGHEST) for hs in heads]
    sel = [_top3_rows(g, n_iota, q_blk) | (n_iota == q_blk) for g in gates]
    acc_s[...] = jnp.zeros(acc_s.shape, F32)

    r8 = lax.broadcasted_iota(jnp.int32, (FEAT_ROWS, MOBA_QT), 0)
    q_rel = lax.broadcasted_iota(jnp.int32, (FEAT_ROWS, MOBA_QT), 1).astype(F32)
    chunk_start = (c * MOBA_QT).astype(F32)
    pad = jnp.zeros((half - FEAT_ROWS - n_blk, MOBA_QT), F32)
    q_aug = []
    for h, hs in enumerate(heads):
        slope = 2.0 ** -(h + 1)
        base = jnp.where(r8 == 0, -slope * chunk_start, 0.0)
        base = jnp.where(r8 == 1, -slope * q_rel, base)
        base = jnp.where((r8 == 2) | (r8 == 3), slope, base)
        feat = jnp.concatenate([base, (sel[h].astype(F32) - 1.0) * MASK_BIG, pad], axis=0)
        q_h = qt[hs, :] * (HEAD_DIM ** -0.5)
        q_aug.append(jnp.concatenate([q_h, feat] if h % 2 == 0 else [feat, q_h], axis=0).astype(BF16))

    left = lane_k < half
    causal = (lax.broadcasted_iota(jnp.int32, (MOBA_BLOCK, MOBA_QT), 0)
              <= lax.broadcasted_iota(jnp.int32, (MOBA_BLOCK, MOBA_QT), 1) + (c * MOBA_QT - q_blk * MOBA_BLOCK))
    head_row = lax.broadcasted_iota(jnp.int32, (H_ATT, MOBA_QT), 0)

    def attend(n, m_all, l_all, own):
        rows = pl.ds(pl.multiple_of(n * MOBA_BLOCK, MOBA_BLOCK), MOBA_BLOCK)
        f_even, f_odd = fe_s[n], fo_s[n]
        vblk = vt_s[n]
        k_aug = []
        for j in range(H_ATT // 2):
            k_pair = kb_ref[0, rows, j * 2 * half:(j + 1) * 2 * half]
            k_aug += [jnp.where(left, k_pair, f_even), jnp.where(left, f_odd, k_pair)]
        s = [_dot(k, q) for k, q in zip(k_aug, q_aug)]
        if own:
            s = [jnp.where(causal, x, NEG) for x in s]
        m_old = [m_all[h:h + 1, :] for h in range(H_ATT)]
        m_new = [jnp.maximum(mo, jnp.max(x, axis=0, keepdims=True)) for mo, x in zip(m_old, s)]
        p = [jnp.exp(x - mn) for x, mn in zip(s, m_new)]
        corr = [jnp.exp(mo - mn) for mo, mn in zip(m_old, m_new)]
        pv = [_dot(vblk[hs, :], x.astype(BF16)) for hs, x in zip(heads, p)]
        for h, hs in enumerate(heads):
            acc_s[hs, :] = acc_s[hs, :] * corr[h] + pv[h]
            l_h = l_all[h:h + 1, :] * corr[h] + jnp.sum(p[h], axis=0, keepdims=True)
            l_all = jnp.where(head_row == h, l_h, l_all)
            m_all = jnp.where(head_row == h, m_new[h], m_all)
        return m_all, l_all

    m0 = jnp.full((H_ATT, MOBA_QT), NEG, F32)
    l0 = jnp.zeros((H_ATT, MOBA_QT), F32)
    m_all, l_all = lax.fori_loop(0, q_blk, lambda n, carry: attend(n, *carry, False), (m0, l0))
    m_all, l_all = attend(q_blk, m_all, l_all, True)
    for h, hs in enumerate(heads):
        acc_s[hs, :] = acc_s[hs, :] / l_all[h:h + 1, :]
    o_ref[0] = acc_s[...].T


def _moba_prompt(qt, kb, vt, kmean):
    bsz, t, _ = kb.shape
    assert t % MOBA_BLOCK == 0 and MOBA_BLOCK % MOBA_QT == 0
    n_blk = t // MOBA_BLOCK
    assert n_blk % 8 == 0 and FEAT_ROWS + n_blk <= HEAD_DIM and H_ATT % 2 == 0
    return pl.pallas_call(
        functools.partial(_moba_prompt_body, n_blk=n_blk),
        grid=(bsz, t // MOBA_QT),
        in_specs=[pl.BlockSpec((1, ATT_WIDTH, MOBA_QT), lambda i, j: (i, 0, j)),
                  pl.BlockSpec((1, t, ATT_WIDTH), lambda i, j: (i, 0, 0)),
                  pl.BlockSpec((1, ATT_WIDTH, t), lambda i, j: (i, 0, 0)),
                  pl.BlockSpec((1, n_blk, ATT_WIDTH), lambda i, j: (i, 0, 0))],
        out_specs=pl.BlockSpec((1, MOBA_QT, ATT_WIDTH), lambda i, j: (i, j, 0)),
        out_shape=jax.ShapeDtypeStruct((bsz, t, ATT_WIDTH), F32),
        scratch_shapes=[pltpu.VMEM((n_blk, ATT_WIDTH, MOBA_BLOCK), BF16),
                        pltpu.VMEM((n_blk, MOBA_BLOCK, 2 * HEAD_DIM), BF16),
                        pltpu.VMEM((n_blk, MOBA_BLOCK, 2 * HEAD_DIM), BF16),
                        pltpu.VMEM((ATT_WIDTH, MOBA_QT), F32)],
        compiler_params=_cparams("parallel", "arbitrary"),
    )(qt, kb, vt, kmean)


PAGES_PER_STEP = 32
PAGES_PER_BLOCK = MOBA_BLOCK // PAGE_SIZE
N_SEL_PAGES = MOBA_TOPK * PAGES_PER_BLOCK


def _mlp_scores_body(pt_ref, x_ref, w1_ref, w2_ref, g_ref, b_ref, q_ref, kt_hbm, o_ref, s_ref, top_ref,
                     kbuf, sem, qcol_s, gate_s, *, n_pages, steps_per_seq):
    i = pl.program_id(0)
    part = i % steps_per_seq
    n_groups = n_pages // PAGES_PER_STEP // steps_per_seq
    n_blk = n_pages // PAGES_PER_BLOCK
    blk_per_group = PAGES_PER_STEP // PAGES_PER_BLOCK
    ff_chunks = _col_chunks(D_FF)
    ff_per_group = len(ff_chunks) // n_groups
    blk_per_ff = blk_per_group // ff_per_group

    def page_copies(group, slot):
        return [pltpu.make_async_copy(kt_hbm.at[pt_ref[group * PAGES_PER_STEP + j]], kbuf.at[slot, j], sem.at[slot])
                for j in range(PAGES_PER_STEP)]

    @pl.when(i == 0)
    def _():
        for cp in page_copies(0, 0):
            cp.start()

    @pl.when(part == 0)
    def _():
        for h in range(H_ATT):
            qcol_s[h] = jnp.broadcast_to(q_ref[0, h:h + 1, :], (PAGE_SIZE, HEAD_DIM)).T
        gate_s[...] = jnp.zeros(gate_s.shape, F32)

    lane = lax.broadcasted_iota(jnp.int32, (H_ATT, 128), 1)
    gates = gate_s[...]
    x = x_ref[...]
    xb = x.astype(BF16)
    acc = jnp.zeros(x.shape, F32)
    for g in range(n_groups):
        group = i * n_groups + g

        @pl.when(group + 1 < pl.num_programs(0) * n_groups)
        def _():
            for cp in page_copies(group + 1, (g + 1) % 2):
                cp.start()

        for cp in page_copies(group, g % 2):
            cp.wait()
        for f, (c, w) in enumerate(ff_chunks[g * ff_per_group:(g + 1) * ff_per_group]):
            h = jnp.maximum(_dot(xb, w1_ref[:, c:c + w]), 0.0)
            acc = acc + _dot((h * h).astype(BF16), w2_ref[c:c + w, :])
            for blk in range(f * blk_per_ff, (f + 1) * blk_per_ff):
                blk_sum = jnp.zeros((H_ATT, PAGE_SIZE), F32)
                for jj in range(PAGES_PER_BLOCK):
                    p = blk * PAGES_PER_BLOCK + jj
                    s = jnp.sum(kbuf[g % 2, p] * qcol_s[...], axis=1)
                    s_ref[0, g * PAGES_PER_STEP + p] = s
                    blk_sum = blk_sum + s
                gate = jnp.sum(blk_sum, axis=1, keepdims=True) * (1.0 / MOBA_BLOCK)
                gates = jnp.where(lane == (part * n_groups + g) * blk_per_group + blk, gate, gates)
    o_ref[...] = _layer_norm(ALPHA * x + acc, g_ref[...], b_ref[...])
    gate_s[...] = gates

    @pl.when(part == steps_per_seq - 1)
    def _():
        g_work = jnp.where(lane < n_blk, gates, -jnp.inf)
        out = jnp.zeros((H_ATT, 128), jnp.int32)
        for r in range(MOBA_TOPK):
            mx = jnp.max(g_work, axis=1, keepdims=True)
            idx = jnp.min(jnp.where(g_work == mx, lane, 128), axis=1, keepdims=True)
            out = jnp.where(lane == r, idx, out)
            g_work = jnp.where(lane == idx, -jnp.inf, g_work)
        top_ref[0] = out


def _mlp_scores(x, w1, w2, g, b, q, kt_pool, page_table, tm):
    m = x.shape[0]
    bd, n_pages = page_table.shape
    assert m % (bd * tm) == 0 and n_pages % PAGES_PER_STEP == 0 and PAGES_PER_STEP % PAGES_PER_BLOCK == 0
    sps = m // (bd * tm)
    n_groups = n_pages // PAGES_PER_STEP
    assert n_groups % sps == 0 and (n_groups // sps) % 2 == 0 and len(_col_chunks(D_FF)) % (n_groups // sps) == 0
    assert (PAGES_PER_STEP // PAGES_PER_BLOCK) % (len(_col_chunks(D_FF)) // (n_groups // sps)) == 0
    assert MOBA_TOPK <= n_pages // PAGES_PER_BLOCK <= 128
    resident = lambda shape: pl.BlockSpec(shape, lambda i, pt: (0,) * len(shape), pipeline_mode=pl.Buffered(1))
    grid_spec = pltpu.PrefetchScalarGridSpec(
        num_scalar_prefetch=1,
        grid=(bd * sps,),
        in_specs=[pl.BlockSpec((tm, D_MODEL), lambda i, pt: (i, 0)), resident(w1.shape), resident(w2.shape),
                  resident((1, D_MODEL)), resident((1, D_MODEL)),
                  pl.BlockSpec((1, H_ATT, HEAD_DIM), lambda i, pt: (i // sps, 0, 0)),
                  pl.BlockSpec(memory_space=pl.ANY)],
        out_specs=[pl.BlockSpec((tm, D_MODEL), lambda i, pt: (i, 0)),
                   pl.BlockSpec((1, n_pages // sps, H_ATT, PAGE_SIZE), lambda i, pt: (i // sps, i % sps, 0, 0)),
                   pl.BlockSpec((1, H_ATT, 128), lambda i, pt: (i // sps, 0, 0))],
        scratch_shapes=[pltpu.VMEM((2, PAGES_PER_STEP, H_ATT, HEAD_DIM, PAGE_SIZE), F32),
                        pltpu.SemaphoreType.DMA((2,)),
                        pltpu.VMEM((H_ATT, HEAD_DIM, PAGE_SIZE), F32), pltpu.VMEM((H_ATT, 128), F32)],
    )
    return pl.pallas_call(
        functools.partial(_mlp_scores_body, n_pages=n_pages, steps_per_seq=sps),
        grid_spec=grid_spec,
        out_shape=[jax.ShapeDtypeStruct((m, D_MODEL), F32),
                   jax.ShapeDtypeStruct((bd, n_pages, H_ATT, PAGE_SIZE), F32),
                   jax.ShapeDtypeStruct((bd, H_ATT, 128), jnp.int32)],
        compiler_params=_cparams("arbitrary"),
    )(page_table.reshape(-1), x, w1, w2, g, b, q, kt_pool)


def _moba_sample_body(pt_ref, top_ref, q_ref, kn_ref, vn_ref, s_ref, vt_hbm, o_ref, vbuf, sem, *, past, n_pages):
    b = pl.program_id(0)
    units = [(h, j) for h in range(H_ATT) for j in range(N_SEL_PAGES)]

    def seq_page(seq, h, j):
        return top_ref[(seq * H_ATT + h) * MOBA_TOPK + j // PAGES_PER_BLOCK] * PAGES_PER_BLOCK + j % PAGES_PER_BLOCK

    def tile_copies(seq, slot):
        return [pltpu.make_async_copy(vt_hbm.at[pt_ref[seq * n_pages + seq_page(seq, h, j)], h],
                                      vbuf.at[slot, h * N_SEL_PAGES + j], sem.at[slot]) for h, j in units]

    @pl.when(b == 0)
    def _():
        for cp in tile_copies(0, 0):
            cp.start()

    @pl.when(b + 1 < pl.num_programs(0))
    def _():
        for cp in tile_copies(b + 1, (b + 1) % 2):
            cp.start()

    slot = b % 2
    scale = HEAD_DIM ** -0.5
    lane = lax.broadcasted_iota(jnp.int32, (1, PAGE_SIZE), 1)
    logits = []
    for h, j in units:
        page = seq_page(b, h, j)
        dist = (past - (page * PAGE_SIZE + lane)).astype(F32)
        logits.append(s_ref[0, page, h:h + 1, :] * scale - (2.0 ** -(h + 1)) * dist)
    s_self = jnp.sum(q_ref[0] * kn_ref[0], axis=-1, keepdims=True) * scale
    m, p_self = [], []
    for h in range(H_ATT):
        m_h = s_self[h:h + 1]
        for s in logits[h * N_SEL_PAGES:(h + 1) * N_SEL_PAGES]:
            m_h = jnp.maximum(m_h, jnp.max(s, axis=-1, keepdims=True))
        m.append(m_h)
        p_self.append(jnp.exp(s_self[h:h + 1] - m_h))
    p = [jnp.exp(s - m[h]) for s, (h, j) in zip(logits, units)]
    for cp in tile_copies(b, slot):
        cp.wait()
    pv = [_bdot(jnp.broadcast_to(x, (8, PAGE_SIZE)), vbuf[slot, h * N_SEL_PAGES + j], _NT)[0:1]
          for x, (h, j) in zip(p, units)]
    for h in range(H_ATT):
        sel = range(h * N_SEL_PAGES, (h + 1) * N_SEL_PAGES)
        l = p_self[h] + sum(jnp.sum(p[i], axis=-1, keepdims=True) for i in sel)
        acc = p_self[h] * vn_ref[0, h:h + 1, :] + sum(pv[i] for i in sel)
        o_ref[0, h] = acc / l


def _moba_sample(q, k_new, v_new, scores, vt_pool, page_table, top):
    bd, n_pages = page_table.shape
    past = n_pages * PAGE_SIZE
    assert past % MOBA_BLOCK == 0

    def tok_spec():
        return pl.BlockSpec((1, H_ATT, HEAD_DIM), lambda b, pt, tp: (b, 0, 0))

    grid_spec = pltpu.PrefetchScalarGridSpec(
        num_scalar_prefetch=2,
        grid=(bd,),
        in_specs=[tok_spec(), tok_spec(), tok_spec(),
                  pl.BlockSpec((1, n_pages, H_ATT, PAGE_SIZE), lambda b, pt, tp: (b, 0, 0, 0)),
                  pl.BlockSpec(memory_space=pl.ANY)],
        out_specs=pl.BlockSpec((1, H_ATT, 1, HEAD_DIM), lambda b, pt, tp: (b, 0, 0, 0)),
        scratch_shapes=[pltpu.VMEM((2, H_ATT * N_SEL_PAGES, HEAD_DIM, PAGE_SIZE), F32),
                        pltpu.SemaphoreType.DMA((2,))],
    )
    return pl.pallas_call(
        functools.partial(_moba_sample_body, past=past, n_pages=n_pages),
        grid_spec=grid_spec,
        out_shape=jax.ShapeDtypeStruct((bd, H_ATT, 1, HEAD_DIM), F32),
        compiler_params=_cparams("arbitrary"),
    )(page_table.reshape(-1), top.reshape(-1), q, k_new, v_new, scores, vt_pool)


def _softplus(z):
    return jnp.maximum(z, 0.0) + jnp.log(1.0 + jnp.exp(-jnp.abs(z)))


def _sigmoid(z):
    return 1.0 / (1.0 + jnp.exp(-z))


def _rwkv_token_terms(m, w0, w_up, a0, a_up, g_up, k_k, k_a):
    rw = RWKV_WIDTH
    r, k, v = m[:, 0:rw], m[:, rw:2 * rw], m[:, 2 * rw:3 * rw]
    xw = m[:, 3 * rw:3 * rw + W_LORA]
    xa = m[:, 3 * rw + W_LORA:3 * rw + W_LORA + A_LORA]
    xg = m[:, 3 * rw + W_LORA + A_LORA:]
    w_log = -_softplus(-(w0 + _dot(jnp.tanh(xw), w_up, precision=HIGHEST))) - 0.5
    log_decay = -jnp.exp(w_log)
    a = _sigmoid(a0 + _dot(xa, a_up, precision=HIGHEST))
    g = _bdot(_sigmoid(xg), g_up)
    kk = k * k_k
    k2 = k * (1.0 + (a - 1.0) * k_a)
    return r, log_decay, k2, v, kk, a, g


def _head_norm(kk_h):
    return kk_h * lax.rsqrt(jnp.maximum(jnp.sum(kk_h * kk_h, axis=-1, keepdims=True), 1e-24))


def _rwkv_head_out(y, r_h, k_h, v_h, g_h, rk_h, gng_h, gnb_h):
    mu = jnp.mean(y, axis=-1, keepdims=True)
    d = y - mu
    var = jnp.mean(d * d, axis=-1, keepdims=True)
    yn = d * lax.rsqrt(var + GN_EPS) * gng_h + gnb_h
    bonus = jnp.sum(r_h * k_h * rk_h, axis=-1, keepdims=True) * v_h
    return (yn + bonus) * g_h


PAIR = 2 * RWKV_HEAD


def _pair_blockdiag(y):
    left = lax.broadcasted_iota(jnp.int32, y.shape, 1) < RWKV_HEAD
    zero = jnp.zeros_like(y)
    return jnp.concatenate([jnp.where(left, y, zero), jnp.where(left, zero, y)], axis=0)


def _pair_nn(x, y):
    return _dot(x.astype(BF16), _pair_blockdiag(y.astype(BF16)))


def _pair_nt(x, y):
    return _dot(x.astype(BF16), _pair_blockdiag(y.astype(BF16)), _NT)


def _pair_tn(x, y):
    full = _dot(x.astype(BF16), y.astype(BF16), _TN)
    left = lax.broadcasted_iota(jnp.int32, (RWKV_HEAD, PAIR), 1) < RWKV_HEAD
    return jnp.where(left, full[:RWKV_HEAD], full[RWKV_HEAD:])


def _pair_sum(x):
    left = lax.broadcasted_iota(jnp.int32, x.shape, 1) < RWKV_HEAD
    s_a = jnp.sum(jnp.where(left, x, 0.0), axis=-1, keepdims=True)
    s_b = jnp.sum(jnp.where(left, 0.0, x), axis=-1, keepdims=True)
    return jnp.where(left, s_a, s_b)


def _rwkv_pair_out(y, r_p, k_p, v_p, g_p, rk_p, gng_p, gnb_p):
    inv_n = 1.0 / RWKV_HEAD
    d = y - _pair_sum(y) * inv_n
    var = _pair_sum(d * d) * inv_n
    yn = d * lax.rsqrt(var + GN_EPS) * gng_p + gnb_p
    return (yn + _pair_sum(r_p * k_p * rk_p) * v_p) * g_p


def _rwkv_prompt_body(rw_ref, prev_ref, s0_ref, mu_ref, w0_ref, wup_ref, a0_ref, aup_ref, gup_ref,
                      kk_ref, ka_ref, rk_ref, gng_ref, gnb_ref, y_ref, sfin_ref, state_s, prev_s):
    c = pl.program_id(1)
    ch = RWKV_CHUNK
    tile = rw_ref.shape[1]
    n_ch = tile // ch
    n_pairs = H_RWKV // 2
    pairs = [slice(p * PAIR, (p + 1) * PAIR) for p in range(n_pairs)]

    @pl.when(c == 0)
    def _():
        for p in range(n_pairs):
            state_s[p] = jnp.concatenate([s0_ref[0, (2 * p) * RWKV_HEAD:(2 * p + 1) * RWKV_HEAD, :],
                                          s0_ref[0, (2 * p + 1) * RWKV_HEAD:(2 * p + 2) * RWKV_HEAD, :]], axis=1)
        prev_s[...] = prev_ref[0]

    rw = rw_ref[0]
    row = lax.broadcasted_iota(jnp.int32, rw.shape, 0)
    rw_prev = jnp.where(row == 0, prev_s[...], pltpu.roll(rw, 1, 0))
    prev_s[...] = rw[tile - 1:tile, :]
    m = rw + (rw_prev - rw) * mu_ref[...]
    r, log_decay, k2, v, kk, a, g = _rwkv_token_terms(
        m, w0_ref[...], wup_ref[...], a0_ref[...], aup_ref[...], gup_ref[...], kk_ref[...], ka_ref[...])

    ti = lax.broadcasted_iota(jnp.int32, (tile, tile), 0)
    si = lax.broadcasted_iota(jnp.int32, (tile, tile), 1)
    in_chunk_lower = (si <= ti) & (si // ch == ti // ch)
    cs = _dot(in_chunk_lower.astype(F32), log_decay, precision=HIGHEST)
    cs_end = cs[ch - 1:ch, :]
    rows = lax.broadcasted_iota(jnp.int32, cs.shape, 0)
    for j in range(1, n_ch):
        cs_end = jnp.where(rows >= j * ch, cs[(j + 1) * ch - 1:(j + 1) * ch, :], cs_end)
    gam = jnp.exp(cs)
    gam_prev = jnp.exp(cs - log_decay)
    gam_inv = jnp.exp(-cs)
    gam_tail = jnp.exp(cs_end - cs)

    t_p = lax.broadcasted_iota(jnp.int32, (ch, PAIR), 0)
    s_p = lax.broadcasted_iota(jnp.int32, (ch, PAIR), 1) % RWKV_HEAD
    lower = s_p <= t_p
    strict = s_p < t_p
    eye = (s_p == t_p).astype(F32)

    units = [(slice(j * ch, (j + 1) * ch), ps) for j in range(n_ch) for ps in pairs]
    kk_n = [kk[ts, ps] for ts, ps in units]
    kk_n = [x * lax.rsqrt(jnp.maximum(_pair_sum(x * x), 1e-24)) for x in kk_n]
    a_t = [-n * gam_prev[ts, ps] for n, (ts, ps) in zip(kk_n, units)]
    b_h = [n * a[ts, ps] for n, (ts, ps) in zip(kk_n, units)]
    b_t = [x * gam_inv[ts, ps] for x, (ts, ps) in zip(b_h, units)]
    k_t = [k2[ts, ps] * gam_inv[ts, ps] for ts, ps in units]
    r_t = [r[ts, ps] * gam[ts, ps] for ts, ps in units]
    v_h = [v[ts, ps] for ts, ps in units]
    l_ab = [jnp.where(strict, _pair_nt(x, z), 0.0) for x, z in zip(a_t, b_t)]
    l_ak = [jnp.where(strict, _pair_nt(x, z), 0.0) for x, z in zip(a_t, k_t)]
    m_rb = [jnp.where(lower, _pair_nt(x, z), 0.0) for x, z in zip(r_t, b_t)]
    m_rk = [jnp.where(lower, _pair_nt(x, z), 0.0) for x, z in zip(r_t, k_t)]
    inv = [eye + x for x in l_ab]
    pw = list(l_ab)
    span = 2
    while span < ch:
        pw = [_pair_nn(x, x) for x in pw]
        inv = [i + _pair_nn(i, x) for i, x in zip(inv, pw)]
        span *= 2
    lv = [_pair_nn(l, vv) for l, vv in zip(l_ak, v_h)]
    t_a = [_pair_nn(i, x) for i, x in zip(inv, a_t)]
    t_l = [_pair_nn(i, x) for i, x in zip(inv, lv)]
    b_e = [x * gam_tail[u_] for x, u_ in zip(b_h, units)]
    k_e = [k2[u_] * gam_tail[u_] for u_ in units]
    r_a = [x + _pair_nn(mb, ta) for x, mb, ta in zip(r_t, m_rb, t_a)]
    y_0 = [_pair_nn(mb, tl) + _pair_nn(mk, vv) for mb, tl, mk, vv in zip(m_rb, t_l, m_rk, v_h)]
    m_s = [_pair_tn(ta, be) for ta, be in zip(t_a, b_e)]
    c_s = [_pair_tn(tl, be) + _pair_tn(vv, ke) for tl, be, vv, ke in zip(t_l, b_e, v_h, k_e)]
    s_cur = [state_s[p] for p in range(n_pairs)]
    for j in range(n_ch):
        sel = range(j * n_pairs, (j + 1) * n_pairs)
        y = [_pair_nt(r_a[i], s) + y_0[i] for i, s in zip(sel, s_cur)]
        s_cur = [s * gam[(j + 1) * ch - 1:(j + 1) * ch, units[i][1]] + _pair_nn(s, m_s[i]) + c_s[i]
                 for i, s in zip(sel, s_cur)]
        for i, yy in zip(sel, y):
            ts, ps = units[i]
            y_ref[0, ts, ps] = _rwkv_pair_out(yy, r[ts, ps], k2[ts, ps], v_h[i], g[ts, ps],
                                              rk_ref[:, ps], gng_ref[:, ps], gnb_ref[:, ps])
    for p in range(n_pairs):
        state_s[p] = s_cur[p]

    @pl.when(c == pl.num_programs(1) - 1)
    def _():
        for p in range(n_pairs):
            sfin_ref[0, (2 * p) * RWKV_HEAD:(2 * p + 1) * RWKV_HEAD, :] = state_s[p][:, :RWKV_HEAD]
            sfin_ref[0, (2 * p + 1) * RWKV_HEAD:(2 * p + 2) * RWKV_HEAD, :] = state_s[p][:, RWKV_HEAD:]


def _rwkv_prompt(rw, prev_row, s0, p, tile):
    bsz, t, _ = rw.shape
    assert t % tile == 0 and tile % RWKV_CHUNK == 0 and H_RWKV % 2 == 0
    hn = H_RWKV * RWKV_HEAD
    vec = lambda n: _full((1, n))
    return pl.pallas_call(
        _rwkv_prompt_body,
        grid=(bsz, t // tile),
        in_specs=[pl.BlockSpec((1, tile, RWKV_COLS), lambda i, j: (i, j, 0)),
                  pl.BlockSpec((1, 1, RWKV_COLS), lambda i, j: (i, 0, 0)),
                  pl.BlockSpec((1, hn, RWKV_HEAD), lambda i, j: (i, 0, 0)),
                  vec(RWKV_COLS), vec(RWKV_WIDTH), _full((W_LORA, RWKV_WIDTH)), vec(RWKV_WIDTH),
                  _full((A_LORA, RWKV_WIDTH)), _full((G_LORA, RWKV_WIDTH)),
                  vec(RWKV_WIDTH), vec(RWKV_WIDTH), vec(RWKV_WIDTH), vec(RWKV_WIDTH), vec(RWKV_WIDTH)],
        out_specs=[pl.BlockSpec((1, tile, RWKV_WIDTH), lambda i, j: (i, j, 0)),
                   pl.BlockSpec((1, hn, RWKV_HEAD), lambda i, j: (i, 0, 0))],
        out_shape=[jax.ShapeDtypeStruct((bsz, t, RWKV_WIDTH), F32),
                   jax.ShapeDtypeStruct((bsz, hn, RWKV_HEAD), F32)],
        scratch_shapes=[pltpu.VMEM((H_RWKV // 2, RWKV_HEAD, PAIR), F32), pltpu.VMEM((1, RWKV_COLS), F32)],
        compiler_params=_cparams("parallel", "arbitrary"),
    )(rw, prev_row, s0, p["mu_shift"], p["w0"], p["w_up"], p["a0"], p["a_up"], p["g_up"],
      p["k_k"], p["k_a"], p["r_k"], p["gn_g"], p["gn_b"])


def _rows8(*rows):
    ri = lax.broadcasted_iota(jnp.int32, (8, rows[0].shape[1]), 0)
    out = jnp.zeros((8, rows[0].shape[1]), F32)
    for i, x in enumerate(rows):
        out = jnp.where(ri == i, x, out)
    return out


STEP_ROWS = 8


def _rwkv_step_body(rw_ref, prev_ref, s0_ref, mu_ref, w0_ref, wup_ref, a0_ref, aup_ref, gup_ref,
                    kk_ref, ka_ref, rk_ref, gng_ref, gnb_ref, y_ref, sfin_ref):
    rw = rw_ref[...]
    m = rw + (prev_ref[...] - rw) * mu_ref[...]
    r, log_decay, k2, v, kk, a, g = _rwkv_token_terms(
        m, w0_ref[...], wup_ref[...], a0_ref[...], aup_ref[...], gup_ref[...], kk_ref[...], ka_ref[...])
    decay = jnp.exp(log_decay)
    heads = [slice(h * RWKV_HEAD, (h + 1) * RWKV_HEAD) for h in range(H_RWKV)]
    kkn = [_head_norm(kk[:, hs]) for hs in heads]
    b_h = [n * a[:, hs] for n, hs in zip(kkn, heads)]
    units = [(i, h) for i in range(STEP_ROWS) for h in range(H_RWKV)]
    row = lambda x, i: x[i:i + 1, :]
    s_0 = [s0_ref[i, heads[h], :] for i, h in units]
    sa = [_dot(_rows8(-row(kkn[h], i)), s, _NT, HIGHEST)[0:1] for (i, h), s in zip(units, s_0)]
    s_1 = [s * row(decay[:, heads[h]], i)
           + _dot(_rows8(x, row(v[:, heads[h]], i)), _rows8(row(b_h[h], i), row(k2[:, heads[h]], i)), _TN, HIGHEST)
           for (i, h), s, x in zip(units, s_0, sa)]
    y = [_dot(_rows8(row(r[:, heads[h]], i)), s, _NT, HIGHEST)[0:1] for (i, h), s in zip(units, s_1)]
    for (i, h), s in zip(units, s_1):
        sfin_ref[i, heads[h], :] = s
    seq = lax.broadcasted_iota(jnp.int32, (STEP_ROWS, RWKV_HEAD), 0)
    for h, hs in enumerate(heads):
        y_h = jnp.zeros((STEP_ROWS, RWKV_HEAD), F32)
        for i in range(STEP_ROWS):
            y_h = jnp.where(seq == i, y[i * H_RWKV + h], y_h)
        y_ref[:, hs] = _rwkv_head_out(y_h, r[:, hs], k2[:, hs], v[:, hs], g[:, hs],
                                      rk_ref[:, hs], gng_ref[:, hs], gnb_ref[:, hs])


def _rwkv_step(rw, prev_row, s0, p):
    bd = rw.shape[0]
    assert bd % STEP_ROWS == 0
    hn = H_RWKV * RWKV_HEAD
    vec = lambda n: _full((1, n))
    rows = lambda n: pl.BlockSpec((STEP_ROWS, n), lambda i: (i, 0))
    state = pl.BlockSpec((STEP_ROWS, hn, RWKV_HEAD), lambda i: (i, 0, 0))
    return pl.pallas_call(
        _rwkv_step_body,
        grid=(bd // STEP_ROWS,),
        in_specs=[rows(RWKV_COLS), rows(RWKV_COLS), state,
                  vec(RWKV_COLS), vec(RWKV_WIDTH), _full((W_LORA, RWKV_WIDTH)), vec(RWKV_WIDTH),
                  _full((A_LORA, RWKV_WIDTH)), _full((G_LORA, RWKV_WIDTH)),
                  vec(RWKV_WIDTH), vec(RWKV_WIDTH), vec(RWKV_WIDTH), vec(RWKV_WIDTH), vec(RWKV_WIDTH)],
        out_specs=[rows(RWKV_WIDTH), state],
        out_shape=[jax.ShapeDtypeStruct((bd, RWKV_WIDTH), F32),
                   jax.ShapeDtypeStruct((bd, hn, RWKV_HEAD), F32)],
        compiler_params=_cparams("parallel"),
    )(rw, prev_row, s0, p["mu_shift"], p["w0"], p["w_up"], p["a0"], p["a_up"], p["g_up"],
      p["k_k"], p["k_a"], p["r_k"], p["gn_g"], p["gn_b"])


def _row(x):
    return x.reshape(1, -1)


def kernel(x_prompt, x_sample, mem_prompt, cache_k_pool, cache_v_pool, page_table, cache_mem_k, cache_mem_v, state_wkv, state_shift, w_in, mu_shift, w0, w_up, a0, a_up, g_up, k_k, k_a, r_k, gn_g, gn_b, w_out, ln1_g, ln1_b, w_q_mem, w_kv_mem, w_o_mem, ln2_g, ln2_b, w_ff1, w_ff2, ln3_g, ln3_b):
    assert w_in.shape[0] == DEPTH == 1
    bsz, t, _ = x_prompt.shape
    bd = x_sample.shape[0]
    hn = H_RWKV * RWKV_HEAD

    w_in_b = w_in[0].astype(BF16)
    w_out_b = w_out[0].astype(BF16)
    wq_b = w_q_mem[0].astype(BF16)
    wkv_b = w_kv_mem[0].astype(BF16)
    wo_b = w_o_mem[0].astype(BF16)
    w1_b = w_ff1[0].astype(BF16)
    w2_b = w_ff2[0].astype(BF16)
    ln1 = (_row(ln1_g[0]), _row(ln1_b[0]))
    ln2 = (_row(ln2_g[0]), _row(ln2_b[0]))
    ln3 = (_row(ln3_g[0]), _row(ln3_b[0]))
    rwkv_p = dict(mu_shift=_row(mu_shift[0]), w0=_row(w0[0]), w_up=w_up[0], a0=_row(a0[0]), a_up=a_up[0],
                  g_up=g_up[0], k_k=_row(k_k[0]), k_a=_row(k_a[0]), r_k=_row(r_k[0]),
                  gn_g=_row(gn_g[0]), gn_b=_row(gn_b[0]))

    qt_p, kt_p, vt_p, kb_p, rw_p3, kmean_p = _proj_prompt(x_prompt, w_in_b, tm=512)
    y_att_p = _moba_prompt(qt_p, kb_p, vt_p, kmean_p)
    y_rw_p, s_p = _rwkv_prompt(rw_p3, jnp.zeros((bsz, 1, RWKV_COLS), F32), jnp.zeros((bsz, hn, RWKV_HEAD), F32), rwkv_p,
                               tile=RWKV_TILE)
    mk_p, mv_p, mk_p4, mv_p4 = _mem_kv(mem_prompt, wkv_b)
    x2_p = _mix_memx_prompt(x_prompt, y_att_p, y_rw_p, mk_p, mv_p, w_out_b[:ATT_WIDTH], w_out_b[ATT_WIDTH:],
                            wq_b, wo_b, ln1, ln2, tm=512)

    kt_pool = cache_k_pool[0].transpose(0, 2, 3, 1)
    vt_pool = cache_v_pool[0].transpose(0, 2, 3, 1)
    xs = x_sample.reshape(bd, D_MODEL)
    q_s = _mm_f32(xs, w_in[0], ATT_WIDTH)
    k_s, v_s, rw_s = _mm_multi(xs, w_in_b, (ATT_WIDTH, ATT_WIDTH, RWKV_COLS), tm=bd, first_col=ATT_WIDTH)
    q_s3, k_s3, v_s3 = (z.reshape(bd, H_ATT, HEAD_DIM) for z in (q_s, k_s, v_s))
    y_p, scores_s, top = _mlp_scores(x2_p.reshape(bsz * t, D_MODEL), w1_b, w2_b, *ln3, q_s3, kt_pool, page_table,
                                     tm=MLP_TM)
    y_att_s = _moba_sample(q_s3, k_s3, v_s3, scores_s, vt_pool, page_table, top[:, :, :MOBA_TOPK])
    y_rw_s, s_s = _rwkv_step(rw_s, state_shift[0], state_wkv[0].reshape(bd, hn, RWKV_HEAD), rwkv_p)
    x1_s = _mm_res_ln([y_att_s.reshape(bd, ATT_WIDTH), y_rw_s],
                      [w_out_b[:ATT_WIDTH], w_out_b[ATT_WIDTH:]], xs, *ln1, tm=bd)
    (qm_s,) = _mm_multi(x1_s, wq_b, (D_MODEL,), tm=bd)
    att_s = _memx_sample(qm_s.reshape(bd, 1, D_MODEL), cache_mem_k[0], cache_mem_v[0])
    x2_s = _mm_res_ln([att_s.reshape(bd, D_MODEL)], [wo_b], x1_s, *ln2, tm=bd)
    y_s = _mlp(x2_s, w1_b, w2_b, *ln3, tm=bd)

    return (y_p.reshape(bsz, t, D_MODEL), y_s.reshape(bd, 1, D_MODEL),
            kt_p.reshape(bsz, H_ATT, HEAD_DIM, t).transpose(0, 3, 1, 2)[None],
            vt_p.reshape(bsz, H_ATT, HEAD_DIM, t).transpose(0, 3, 1, 2)[None],
            k_s.reshape(1, bd, 1, H_ATT, HEAD_DIM), v_s.reshape(1, bd, 1, H_ATT, HEAD_DIM),
            mk_p4[None], mv_p4[None],
            s_p.reshape(1, bsz, H_RWKV, RWKV_HEAD, RWKV_HEAD), rw_p3[:, -1][None],
            s_s.reshape(1, bd, H_RWKV, RWKV_HEAD, RWKV_HEAD), rw_s[None])
```

```python
import functools

import jax
import jax.numpy as jnp
from jax import lax
from jax.experimental import pallas as pl
from jax.experimental.pallas import tpu as pltpu

F32 = jnp.float32
BF16 = jnp.bfloat16
HIGHEST = lax.Precision.HIGHEST

D_MODEL = 1024
PAGE_SIZE = 128
ATT_WIDTH = 512
RWKV_WIDTH = 512
HEAD_DIM = 64
H_ATT = 8
MOBA_BLOCK = 256
MOBA_TOPK = 3
RWKV_HEAD = 64
H_RWKV = 8
W_LORA = 64
A_LORA = 64
G_LORA = 128
RWKV_COLS = 3 * RWKV_WIDTH + W_LORA + A_LORA + G_LORA
PROJ_COLS = 3 * ATT_WIDTH + RWKV_COLS
N_MEM = 256
MEM_HEADS = 4
MEM_HEAD_DIM = 256
D_FF = 4 * D_MODEL
LN_EPS = 1e-5
GN_EPS = 64e-5
DEPTH = 1
ALPHA = (2.0 * DEPTH) ** 0.25

MLP_TM = 512
RWKV_TILE = 512
RWKV_CHUNK = 64
NEG = -1e30
VMEM_LIMIT = 56 * 1024 * 1024
LANES = 128
SUBLANES = 8

_NT = (((1,), (1,)), ((), ()))
_TN = (((0,), (0,)), ((), ()))


def _cparams(*sem):
    return pltpu.CompilerParams(dimension_semantics=sem, vmem_limit_bytes=VMEM_LIMIT)


def _dot(a, b, dims=None, precision=None):
    if dims is None:
        return jnp.dot(a, b, preferred_element_type=F32, precision=precision)
    return lax.dot_general(a, b, dims, preferred_element_type=F32, precision=precision)


def _bdot(a, b, dims=None):
    return _dot(a.astype(BF16), b.astype(BF16), dims)


def _layer_norm(z, g, b):
    mu = jnp.mean(z, axis=-1, keepdims=True)
    d = z - mu
    var = jnp.mean(d * d, axis=-1, keepdims=True)
    return d * lax.rsqrt(var + LN_EPS) * g + b


def _full(shape):
    n = len(shape)
    return pl.BlockSpec(shape, lambda *_: (0,) * n)


def _col_chunks(n, width=512):
    out, c = [], 0
    while c < n:
        w = min(width, n - c)
        out.append((c, w))
        c += w
    return out


def _mm_multi_body(x_ref, w_ref, *o_refs, col0s):
    xb = x_ref[...].astype(BF16)
    for o_ref, c0 in zip(o_refs, col0s):
        for c, w in _col_chunks(o_ref.shape[1]):
            o_ref[:, c:c + w] = _dot(xb, w_ref[:, c0 + c:c0 + c + w])


def _mm_f32_body(x_ref, w_ref, o_ref):
    o_ref[...] = _dot(x_ref[...], w_ref[...], precision=HIGHEST)


def _mm_f32(x, w, n_cols):
    m, k = x.shape
    return pl.pallas_call(
        _mm_f32_body,
        grid=(1,),
        in_specs=[_full(x.shape), pl.BlockSpec((k, n_cols), lambda i: (0, 0))],
        out_specs=_full((m, n_cols)),
        out_shape=jax.ShapeDtypeStruct((m, n_cols), F32),
        compiler_params=_cparams("arbitrary"),
    )(x, w)


def _mm_multi(x, w_bf16, widths, tm, first_col=0):
    m, k = x.shape
    col0s, c = [], first_col
    for wd in widths:
        col0s.append(c)
        c += wd
    assert c == w_bf16.shape[1] and m % tm == 0
    return pl.pallas_call(
        functools.partial(_mm_multi_body, col0s=tuple(col0s)),
        grid=(m // tm,),
        in_specs=[pl.BlockSpec((tm, k), lambda i: (i, 0)), _full(w_bf16.shape)],
        out_specs=[pl.BlockSpec((tm, wd), lambda i: (i, 0)) for wd in widths],
        out_shape=[jax.ShapeDtypeStruct((m, wd), F32) for wd in widths],
        compiler_params=_cparams("parallel"),
    )(x, w_bf16)


def _proj_prompt_body(x_ref, w_ref, qt_ref, kt_ref, vt_ref, kb_ref, rw_ref, kmean_ref):
    j = pl.program_id(1)
    tm = x_ref.shape[1]
    xb = x_ref[0].astype(BF16)
    aw = ATT_WIDTH
    q = _dot(xb, w_ref[:, 0:aw])
    k = _dot(xb, w_ref[:, aw:2 * aw])
    v = _dot(xb, w_ref[:, 2 * aw:3 * aw])
    qt_ref[0] = q.T
    kt_ref[0] = k.T
    vt_ref[0] = v.T
    kb_ref[0] = k.astype(BF16)
    for c, w in _col_chunks(RWKV_COLS):
        rw_ref[0, :, c:c + w] = _dot(xb, w_ref[:, 3 * aw + c:3 * aw + c + w])
    blk_per_tile = tm // MOBA_BLOCK
    for i in range(blk_per_tile):
        kmean_ref[0, pl.ds(j * blk_per_tile + i, 1), :] = jnp.mean(
            k[i * MOBA_BLOCK:(i + 1) * MOBA_BLOCK], axis=0, keepdims=True)


def _proj_prompt(x, w_bf16, tm):
    bsz, t, _ = x.shape
    assert t % tm == 0 and tm % MOBA_BLOCK == 0
    n_blk = t // MOBA_BLOCK
    tr_spec = pl.BlockSpec((1, ATT_WIDTH, tm), lambda i, j: (i, 0, j))
    tr_shape = jax.ShapeDtypeStruct((bsz, ATT_WIDTH, t), F32)
    return pl.pallas_call(
        _proj_prompt_body,
        grid=(bsz, t // tm),
        in_specs=[pl.BlockSpec((1, tm, D_MODEL), lambda i, j: (i, j, 0)), _full(w_bf16.shape)],
        out_specs=[tr_spec, tr_spec, tr_spec,
                   pl.BlockSpec((1, tm, ATT_WIDTH), lambda i, j: (i, j, 0)),
                   pl.BlockSpec((1, tm, RWKV_COLS), lambda i, j: (i, j, 0)),
                   pl.BlockSpec((1, n_blk, ATT_WIDTH), lambda i, j: (i, 0, 0))],
        out_shape=[tr_shape, tr_shape, tr_shape,
                   jax.ShapeDtypeStruct((bsz, t, ATT_WIDTH), BF16),
                   jax.ShapeDtypeStruct((bsz, t, RWKV_COLS), F32),
                   jax.ShapeDtypeStruct((bsz, n_blk, ATT_WIDTH), F32)],
        compiler_params=_cparams("parallel", "arbitrary"),
    )(x, w_bf16)


def _mm_res_ln_body(*refs, n_in):
    x_refs = refs[:n_in]
    w_refs = refs[n_in:2 * n_in]
    res_ref, g_ref, b_ref, o_ref = refs[2 * n_in:]
    acc = _dot(x_refs[0][...].astype(BF16), w_refs[0][...])
    for x_ref, w_ref in zip(x_refs[1:], w_refs[1:]):
        acc = acc + _dot(x_ref[...].astype(BF16), w_ref[...])
    o_ref[...] = _layer_norm(ALPHA * res_ref[...] + acc, g_ref[...], b_ref[...])


def _mm_res_ln(xs, ws, res, g, b, tm):
    m = res.shape[0]
    n_in = len(xs)
    assert m % tm == 0
    in_specs = [pl.BlockSpec((tm, x.shape[1]), lambda i: (i, 0)) for x in xs]
    in_specs += [_full(w.shape) for w in ws]
    in_specs += [pl.BlockSpec((tm, D_MODEL), lambda i: (i, 0)), _full((1, D_MODEL)), _full((1, D_MODEL))]
    return pl.pallas_call(
        functools.partial(_mm_res_ln_body, n_in=n_in),
        grid=(m // tm,),
        in_specs=in_specs,
        out_specs=pl.BlockSpec((tm, D_MODEL), lambda i: (i, 0)),
        out_shape=jax.ShapeDtypeStruct((m, D_MODEL), F32),
        compiler_params=_cparams("parallel"),
    )(*xs, *ws, res, g, b)


def _mlp_body(x_ref, w1_ref, w2_ref, g_ref, b_ref, o_ref):
    x = x_ref[...]
    xb = x.astype(BF16)
    acc = jnp.zeros(x.shape, F32)
    for c, w in _col_chunks(D_FF):
        h = jnp.maximum(_dot(xb, w1_ref[:, c:c + w]), 0.0)
        acc = acc + _dot((h * h).astype(BF16), w2_ref[c:c + w, :])
    o_ref[...] = _layer_norm(ALPHA * x + acc, g_ref[...], b_ref[...])


def _mlp(x, w1, w2, g, b, tm):
    m = x.shape[0]
    assert m % tm == 0
    return pl.pallas_call(
        _mlp_body,
        grid=(m // tm,),
        in_specs=[pl.BlockSpec((tm, D_MODEL), lambda i: (i, 0)), _full(w1.shape), _full(w2.shape),
                  _full((1, D_MODEL)), _full((1, D_MODEL))],
        out_specs=pl.BlockSpec((tm, D_MODEL), lambda i: (i, 0)),
        out_shape=jax.ShapeDtypeStruct((m, D_MODEL), F32),
        compiler_params=_cparams("parallel"),
    )(x, w1, w2, g, b)


def _mem_kv_body(x_ref, w_ref, k2_ref, v2_ref, k4_ref, v4_ref):
    xb = x_ref[0].astype(BF16)
    for o2_ref, o4_ref, c0 in ((k2_ref, k4_ref, 0), (v2_ref, v4_ref, D_MODEL)):
        for h in range(MEM_HEADS):
            cols = slice(h * MEM_HEAD_DIM, (h + 1) * MEM_HEAD_DIM)
            part = _dot(xb, w_ref[:, c0 + h * MEM_HEAD_DIM:c0 + (h + 1) * MEM_HEAD_DIM])
            o2_ref[0, :, cols] = part
            o4_ref[0, :, h, :] = part


def _mem_kv(mem, w_bf16):
    bsz = mem.shape[0]
    flat = pl.BlockSpec((1, N_MEM, D_MODEL), lambda i: (i, 0, 0))
    heads = pl.BlockSpec((1, N_MEM, MEM_HEADS, MEM_HEAD_DIM), lambda i: (i, 0, 0, 0))
    flat_t = jax.ShapeDtypeStruct((bsz, N_MEM, D_MODEL), F32)
    heads_t = jax.ShapeDtypeStruct((bsz, N_MEM, MEM_HEADS, MEM_HEAD_DIM), F32)
    return pl.pallas_call(
        _mem_kv_body,
        grid=(bsz,),
        in_specs=[flat, _full(w_bf16.shape)],
        out_specs=[flat, flat, heads, heads],
        out_shape=[flat_t, flat_t, heads_t, heads_t],
        compiler_params=_cparams("parallel"),
    )(mem, w_bf16)


def _mem_attend(q, k_head, v_head):
    heads = range(MEM_HEADS)
    s = [_bdot(q[:, h * MEM_HEAD_DIM:(h + 1) * MEM_HEAD_DIM], k_head(h), _NT) for h in heads]
    p = [jnp.exp(x - jnp.max(x, axis=-1, keepdims=True)) for x in s]
    l = [jnp.sum(x, axis=-1, keepdims=True) for x in p]
    return jnp.concatenate([_bdot(x, v_head(h)) / z for h, x, z in zip(heads, p, l)], axis=-1)


def _mix_memx_prompt_body(x_ref, ya_ref, yr_ref, mk_ref, mv_ref, wa_ref, wr_ref, wq_ref, wo_ref,
                          g1_ref, b1_ref, g2_ref, b2_ref, o_ref):
    mix = _dot(ya_ref[0].astype(BF16), wa_ref[...]) + _dot(yr_ref[0].astype(BF16), wr_ref[...])
    x1 = _layer_norm(ALPHA * x_ref[0] + mix, g1_ref[...], b1_ref[...])
    q = _dot(x1.astype(BF16), wq_ref[...]) * (MEM_HEAD_DIM ** -0.5)
    cols = lambda h: slice(h * MEM_HEAD_DIM, (h + 1) * MEM_HEAD_DIM)
    att = _mem_attend(q, lambda h: mk_ref[0, :, cols(h)], lambda h: mv_ref[0, :, cols(h)])
    acc = _dot(att.astype(BF16), wo_ref[...])
    o_ref[0] = _layer_norm(ALPHA * x1 + acc, g2_ref[...], b2_ref[...])


def _mix_memx_prompt(x, y_att, y_rw, mk, mv, w_att, w_rw, wq, wo, ln1, ln2, tm):
    bsz, t, _ = x.shape
    assert t % tm == 0
    row = lambda w: pl.BlockSpec((1, tm, w), lambda i, j: (i, j, 0))
    mem = pl.BlockSpec((1, N_MEM, D_MODEL), lambda i, j: (i, 0, 0))
    vec = _full((1, D_MODEL))
    return pl.pallas_call(
        _mix_memx_prompt_body,
        grid=(bsz, t // tm),
        in_specs=[row(D_MODEL), row(ATT_WIDTH), row(RWKV_WIDTH), mem, mem,
                  _full(w_att.shape), _full(w_rw.shape), _full(wq.shape), _full(wo.shape), vec, vec, vec, vec],
        out_specs=row(D_MODEL),
        out_shape=jax.ShapeDtypeStruct(x.shape, F32),
        compiler_params=_cparams("parallel", "parallel"),
    )(x, y_att, y_rw, mk, mv, w_att, w_rw, wq, wo, *ln1, *ln2)


def _memx_sample_body(q_ref, mk_ref, mv_ref, o_ref):
    n_rows = N_MEM * MEM_HEADS
    k_all = mk_ref[0].reshape(n_rows, MEM_HEAD_DIM)
    v_all = mv_ref[0].reshape(n_rows, MEM_HEAD_DIM)
    q = q_ref[0] * (MEM_HEAD_DIM ** -0.5)
    q_rows = [q[:, h * MEM_HEAD_DIM:(h + 1) * MEM_HEAD_DIM] for h in range(MEM_HEADS)]
    q8 = jnp.concatenate(q_rows + [jnp.zeros((SUBLANES - MEM_HEADS, MEM_HEAD_DIM), F32)], axis=0)
    s = _bdot(q8, k_all, _NT)
    row = lax.broadcasted_iota(jnp.int32, s.shape, 0)
    own = (lax.broadcasted_iota(jnp.int32, s.shape, 1) % MEM_HEADS) == row
    s = jnp.where(own, s, NEG)
    p = jnp.where(own, jnp.exp(s - jnp.max(s, axis=-1, keepdims=True)), 0.0)
    l = jnp.sum(p, axis=-1, keepdims=True)
    o8 = _bdot(p, v_all) / jnp.where(l > 0.0, l, 1.0)
    o_ref[0] = jnp.concatenate([o8[h:h + 1] for h in range(MEM_HEADS)], axis=-1)


def _memx_sample(q, mk, mv):
    bd = q.shape[0]
    mem_spec = pl.BlockSpec((1, N_MEM, MEM_HEADS, MEM_HEAD_DIM), lambda i: (i, 0, 0, 0))
    return pl.pallas_call(
        _memx_sample_body,
        grid=(bd,),
        in_specs=[pl.BlockSpec((1, 1, D_MODEL), lambda i: (i, 0, 0)), mem_spec, mem_spec],
        out_specs=pl.BlockSpec((1, 1, D_MODEL), lambda i: (i, 0, 0)),
        out_shape=jax.ShapeDtypeStruct((bd, 1, D_MODEL), F32),
        compiler_params=_cparams("parallel"),
    )(q, mk, mv)


def _top3_rows(g, n_iota, n_lim):
    g = jnp.where(n_iota < n_lim, g, -jnp.inf)
    n_rows = g.shape[0]
    sel = n_iota < 0
    for _ in range(MOBA_TOPK):
        mx = jnp.max(g, axis=0, keepdims=True)
        idx = jnp.min(jnp.where(g == mx, n_iota, n_rows), axis=0, keepdims=True)
        hit = n_iota == idx
        sel = sel | hit
        g = jnp.where(hit, -jnp.inf, g)
    return sel & (n_iota < n_lim)


MOBA_QT = 256
MASK_BIG = 1e30
FEAT_ROWS = SUBLANES


def _moba_prompt_body(qt_ref, kb_ref, vt_ref, kmean_ref, o_ref, vt_s, fe_s, fo_s, acc_s, *, n_blk):
    c = pl.program_id(1)
    half = HEAD_DIM
    lane_k = lax.broadcasted_iota(jnp.int32, (MOBA_BLOCK, 2 * half), 1)
    row_k = lax.broadcasted_iota(jnp.int32, (MOBA_BLOCK, 2 * half), 0)

    @pl.when(c == 0)
    def _():
        for n in range(n_blk):
            vt_s[n] = vt_ref[0, :, n * MOBA_BLOCK:(n + 1) * MOBA_BLOCK].astype(BF16)
            for tbl, f in ((fe_s, lane_k - half), (fo_s, lane_k)):
                feat = jnp.where((f == 0) | (f == 1) | (f == FEAT_ROWS + n), 1.0, 0.0)
                feat = jnp.where(f == 2, row_k.astype(F32), feat)
                feat = jnp.where(f == 3, float(n * MOBA_BLOCK), feat)
                tbl[n] = feat.astype(BF16)

    heads = [slice(h * HEAD_DIM, (h + 1) * HEAD_DIM) for h in range(H_ATT)]
    q_blk = (c * MOBA_QT) // MOBA_BLOCK
    qt = qt_ref[0]
    n_iota = lax.broadcasted_iota(jnp.int32, (n_blk, MOBA_QT), 0)
    gates = [_dot(kmean_ref[0, :, hs], qt[hs, :], precision=HIGHEST) for hs in heads]
    sel = [_top3_rows(g, n_iota, q_blk) | (n_iota == q_blk) for g in gates]
    acc_s[...] = jnp.zeros(acc_s.shape, F32)

    r8 = lax.broadcasted_iota(jnp.int32, (FEAT_ROWS, MOBA_QT), 0)
    q_rel = lax.broadcasted_iota(jnp.int32, (FEAT_ROWS, MOBA_QT), 1).astype(F32)
    chunk_start = (c * MOBA_QT).astype(F32)
    pad = jnp.zeros((half - FEAT_ROWS - n_blk, MOBA_QT), F32)
    q_aug = []
    for h, hs in enumerate(heads):
        slope = 2.0 ** -(h + 1)
        base = jnp.where(r8 == 0, -slope * chunk_start, 0.0)
        base = jnp.where(r8 == 1, -slope * q_rel, base)
        base = jnp.where((r8 == 2) | (r8 == 3), slope, base)
        feat = jnp.concatenate([base, (sel[h].astype(F32) - 1.0) * MASK_BIG, pad], axis=0)
        q_h = qt[hs, :] * (HEAD_DIM ** -0.5)
        q_aug.append(jnp.concatenate([q_h, feat] if h % 2 == 0 else [feat, q_h], axis=0).astype(BF16))

    left = lane_k < half
    causal = (lax.broadcasted_iota(jnp.int32, (MOBA_BLOCK, MOBA_QT), 0)
              <= lax.broadcasted_iota(jnp.int32, (MOBA_BLOCK, MOBA_QT), 1) + (c * MOBA_QT - q_blk * MOBA_BLOCK))
    head_row = lax.broadcasted_iota(jnp.int32, (H_ATT, MOBA_QT), 0)

    def attend(n, m_all, l_all, own):
        rows = pl.ds(pl.multiple_of(n * MOBA_BLOCK, MOBA_BLOCK), MOBA_BLOCK)
        f_even, f_odd = fe_s[n], fo_s[n]
        vblk = vt_s[n]
        k_aug = []
        for j in range(H_ATT // 2):
            k_pair = kb_ref[0, rows, j * 2 * half:(j + 1) * 2 * half]
            k_aug += [jnp.where(left, k_pair, f_even), jnp.where(left, f_odd, k_pair)]
        s = [_dot(k, q) for k, q in zip(k_aug, q_aug)]
        if own:
            s = [jnp.where(causal, x, NEG) for x in s]
        m_old = [m_all[h:h + 1, :] for h in range(H_ATT)]
        m_new = [jnp.maximum(mo, jnp.max(x, axis=0, keepdims=True)) for mo, x in zip(m_old, s)]
        p = [jnp.exp(x - mn) for x, mn in zip(s, m_new)]
        corr = [jnp.exp(mo - mn) for mo, mn in zip(m_old, m_new)]
        pv = [_dot(vblk[hs, :], x.astype(BF16)) for hs, x in zip(heads, p)]
        for h, hs in enumerate(heads):
            acc_s[hs, :] = acc_s[hs, :] * corr[h] + pv[h]
            l_h = l_all[h:h + 1, :] * corr[h] + jnp.sum(p[h], axis=0, keepdims=True)
            l_all = jnp.where(head_row == h, l_h, l_all)
            m_all = jnp.where(head_row == h, m_new[h], m_all)
        return m_all, l_all

    m0 = jnp.full((H_ATT, MOBA_QT), NEG, F32)
    l0 = jnp.zeros((H_ATT, MOBA_QT), F32)
    m_all, l_all = lax.fori_loop(0, q_blk, lambda n, carry: attend(n, *carry, False), (m0, l0))
    m_all, l_all = attend(q_blk, m_all, l_all, True)
    for h, hs in enumerate(heads):
        acc_s[hs, :] = acc_s[hs, :] / l_all[h:h + 1, :]
    o_ref[0] = acc_s[...].T


def _moba_prompt(qt, kb, vt, kmean):
    bsz, t, _ = kb.shape
    assert t % MOBA_BLOCK == 0 and MOBA_BLOCK % MOBA_QT == 0
    n_blk = t // MOBA_BLOCK
    assert n_blk % SUBLANES == 0 and FEAT_ROWS + n_blk <= HEAD_DIM and H_ATT % 2 == 0
    return pl.pallas_call(
        functools.partial(_moba_prompt_body, n_blk=n_blk),
        grid=(bsz, t // MOBA_QT),
        in_specs=[pl.BlockSpec((1, ATT_WIDTH, MOBA_QT), lambda i, j: (i, 0, j)),
                  pl.BlockSpec((1, t, ATT_WIDTH), lambda i, j: (i, 0, 0)),
                  pl.BlockSpec((1, ATT_WIDTH, t), lambda i, j: (i, 0, 0)),
                  pl.BlockSpec((1, n_blk, ATT_WIDTH), lambda i, j: (i, 0, 0))],
        out_specs=pl.BlockSpec((1, MOBA_QT, ATT_WIDTH), lambda i, j: (i, j, 0)),
        out_shape=jax.ShapeDtypeStruct((bsz, t, ATT_WIDTH), F32),
        scratch_shapes=[pltpu.VMEM((n_blk, ATT_WIDTH, MOBA_BLOCK), BF16),
                        pltpu.VMEM((n_blk, MOBA_BLOCK, 2 * HEAD_DIM), BF16),
                        pltpu.VMEM((n_blk, MOBA_BLOCK, 2 * HEAD_DIM), BF16),
                        pltpu.VMEM((ATT_WIDTH, MOBA_QT), F32)],
        compiler_params=_cparams("parallel", "arbitrary"),
    )(qt, kb, vt, kmean)


PAGES_PER_STEP = 32
PAGES_PER_BLOCK = MOBA_BLOCK // PAGE_SIZE
N_SEL_PAGES = MOBA_TOPK * PAGES_PER_BLOCK


def _mlp_scores_body(pt_ref, x_ref, w1_ref, w2_ref, g_ref, b_ref, q_ref, kt_hbm, o_ref, s_ref, top_ref,
                     kbuf, sem, qcol_s, gate_s, *, n_pages, steps_per_seq):
    i = pl.program_id(0)
    part = i % steps_per_seq
    n_groups = n_pages // PAGES_PER_STEP // steps_per_seq
    n_blk = n_pages // PAGES_PER_BLOCK
    blk_per_group = PAGES_PER_STEP // PAGES_PER_BLOCK
    ff_chunks = _col_chunks(D_FF)
    ff_per_group = len(ff_chunks) // n_groups
    blk_per_ff = blk_per_group // ff_per_group

    def page_copies(group, slot):
        return [pltpu.make_async_copy(kt_hbm.at[pt_ref[group * PAGES_PER_STEP + j]], kbuf.at[slot, j], sem.at[slot])
                for j in range(PAGES_PER_STEP)]

    @pl.when(i == 0)
    def _():
        for cp in page_copies(0, 0):
            cp.start()

    @pl.when(part == 0)
    def _():
        for h in range(H_ATT):
            qcol_s[h] = jnp.broadcast_to(q_ref[0, h:h + 1, :], (PAGE_SIZE, HEAD_DIM)).T
        gate_s[...] = jnp.zeros(gate_s.shape, F32)

    lane = lax.broadcasted_iota(jnp.int32, (H_ATT, LANES), 1)
    gates = gate_s[...]
    x = x_ref[...]
    xb = x.astype(BF16)
    acc = jnp.zeros(x.shape, F32)
    for g in range(n_groups):
        group = i * n_groups + g

        @pl.when(group + 1 < pl.num_programs(0) * n_groups)
        def _():
            for cp in page_copies(group + 1, (g + 1) % 2):
                cp.start()

        for cp in page_copies(group, g % 2):
            cp.wait()
        for f, (c, w) in enumerate(ff_chunks[g * ff_per_group:(g + 1) * ff_per_group]):
            h = jnp.maximum(_dot(xb, w1_ref[:, c:c + w]), 0.0)
            acc = acc + _dot((h * h).astype(BF16), w2_ref[c:c + w, :])
            for blk in range(f * blk_per_ff, (f + 1) * blk_per_ff):
                blk_sum = jnp.zeros((H_ATT, PAGE_SIZE), F32)
                for jj in range(PAGES_PER_BLOCK):
                    p = blk * PAGES_PER_BLOCK + jj
                    s = jnp.sum(kbuf[g % 2, p] * qcol_s[...], axis=1)
                    s_ref[0, g * PAGES_PER_STEP + p] = s
                    blk_sum = blk_sum + s
                gate = jnp.sum(blk_sum, axis=1, keepdims=True) * (1.0 / MOBA_BLOCK)
                gates = jnp.where(lane == (part * n_groups + g) * blk_per_group + blk, gate, gates)
    o_ref[...] = _layer_norm(ALPHA * x + acc, g_ref[...], b_ref[...])
    gate_s[...] = gates

    @pl.when(part == steps_per_seq - 1)
    def _():
        g_work = jnp.where(lane < n_blk, gates, -jnp.inf)
        out = jnp.zeros((H_ATT, LANES), jnp.int32)
        for r in range(MOBA_TOPK):
            mx = jnp.max(g_work, axis=1, keepdims=True)
            idx = jnp.min(jnp.where(g_work == mx, lane, LANES), axis=1, keepdims=True)
            out = jnp.where(lane == r, idx, out)
            g_work = jnp.where(lane == idx, -jnp.inf, g_work)
        top_ref[0] = out


def _mlp_scores(x, w1, w2, g, b, q, kt_pool, page_table, tm):
    m = x.shape[0]
    bd, n_pages = page_table.shape
    assert m % (bd * tm) == 0 and n_pages % PAGES_PER_STEP == 0 and PAGES_PER_STEP % PAGES_PER_BLOCK == 0
    sps = m // (bd * tm)
    n_groups = n_pages // PAGES_PER_STEP
    assert n_groups % sps == 0 and (n_groups // sps) % 2 == 0 and len(_col_chunks(D_FF)) % (n_groups // sps) == 0
    assert (PAGES_PER_STEP // PAGES_PER_BLOCK) % (len(_col_chunks(D_FF)) // (n_groups // sps)) == 0
    assert MOBA_TOPK <= n_pages // PAGES_PER_BLOCK <= LANES
    resident = lambda shape: pl.BlockSpec(shape, lambda i, pt: (0,) * len(shape), pipeline_mode=pl.Buffered(1))
    grid_spec = pltpu.PrefetchScalarGridSpec(
        num_scalar_prefetch=1,
        grid=(bd * sps,),
        in_specs=[pl.BlockSpec((tm, D_MODEL), lambda i, pt: (i, 0)), resident(w1.shape), resident(w2.shape),
                  resident((1, D_MODEL)), resident((1, D_MODEL)),
                  pl.BlockSpec((1, H_ATT, HEAD_DIM), lambda i, pt: (i // sps, 0, 0)),
                  pl.BlockSpec(memory_space=pl.ANY)],
        out_specs=[pl.BlockSpec((tm, D_MODEL), lambda i, pt: (i, 0)),
                   pl.BlockSpec((1, n_pages // sps, H_ATT, PAGE_SIZE), lambda i, pt: (i // sps, i % sps, 0, 0)),
                   pl.BlockSpec((1, H_ATT, LANES), lambda i, pt: (i // sps, 0, 0))],
        scratch_shapes=[pltpu.VMEM((2, PAGES_PER_STEP, H_ATT, HEAD_DIM, PAGE_SIZE), F32),
                        pltpu.SemaphoreType.DMA((2,)),
                        pltpu.VMEM((H_ATT, HEAD_DIM, PAGE_SIZE), F32), pltpu.VMEM((H_ATT, LANES), F32)],
    )
    return pl.pallas_call(
        functools.partial(_mlp_scores_body, n_pages=n_pages, steps_per_seq=sps),
        grid_spec=grid_spec,
        out_shape=[jax.ShapeDtypeStruct((m, D_MODEL), F32),
                   jax.ShapeDtypeStruct((bd, n_pages, H_ATT, PAGE_SIZE), F32),
                   jax.ShapeDtypeStruct((bd, H_ATT, LANES), jnp.int32)],
        compiler_params=_cparams("arbitrary"),
    )(page_table.reshape(-1), x, w1, w2, g, b, q, kt_pool)


def _moba_sample_body(pt_ref, top_ref, q_ref, kn_ref, vn_ref, s_ref, vt_hbm, o_ref, vbuf, sem, *, past, n_pages):
    b = pl.program_id(0)
    units = [(h, j) for h in range(H_ATT) for j in range(N_SEL_PAGES)]

    def seq_page(seq, h, j):
        return top_ref[(seq * H_ATT + h) * MOBA_TOPK + j // PAGES_PER_BLOCK] * PAGES_PER_BLOCK + j % PAGES_PER_BLOCK

    def tile_copies(seq, slot):
        return [pltpu.make_async_copy(vt_hbm.at[pt_ref[seq * n_pages + seq_page(seq, h, j)], h],
                                      vbuf.at[slot, h * N_SEL_PAGES + j], sem.at[slot]) for h, j in units]

    @pl.when(b == 0)
    def _():
        for cp in tile_copies(0, 0):
            cp.start()

    @pl.when(b + 1 < pl.num_programs(0))
    def _():
        for cp in tile_copies(b + 1, (b + 1) % 2):
            cp.start()

    slot = b % 2
    scale = HEAD_DIM ** -0.5
    lane = lax.broadcasted_iota(jnp.int32, (1, PAGE_SIZE), 1)
    logits = []
    for h, j in units:
        page = seq_page(b, h, j)
        dist = (past - (page * PAGE_SIZE + lane)).astype(F32)
        logits.append(s_ref[0, page, h:h + 1, :] * scale - (2.0 ** -(h + 1)) * dist)
    s_self = jnp.sum(q_ref[0] * kn_ref[0], axis=-1, keepdims=True) * scale
    m, p_self = [], []
    for h in range(H_ATT):
        m_h = s_self[h:h + 1]
        for s in logits[h * N_SEL_PAGES:(h + 1) * N_SEL_PAGES]:
            m_h = jnp.maximum(m_h, jnp.max(s, axis=-1, keepdims=True))
        m.append(m_h)
        p_self.append(jnp.exp(s_self[h:h + 1] - m_h))
    p = [jnp.exp(s - m[h]) for s, (h, j) in zip(logits, units)]
    for cp in tile_copies(b, slot):
        cp.wait()
    pv = [_bdot(jnp.broadcast_to(x, (SUBLANES, PAGE_SIZE)), vbuf[slot, h * N_SEL_PAGES + j], _NT)[0:1]
          for x, (h, j) in zip(p, units)]
    for h in range(H_ATT):
        sel = range(h * N_SEL_PAGES, (h + 1) * N_SEL_PAGES)
        l = p_self[h] + sum(jnp.sum(p[i], axis=-1, keepdims=True) for i in sel)
        acc = p_self[h] * vn_ref[0, h:h + 1, :] + sum(pv[i] for i in sel)
        o_ref[0, h] = acc / l


def _moba_sample(q, k_new, v_new, scores, vt_pool, page_table, top):
    bd, n_pages = page_table.shape
    past = n_pages * PAGE_SIZE
    assert past % MOBA_BLOCK == 0

    def tok_spec():
        return pl.BlockSpec((1, H_ATT, HEAD_DIM), lambda b, pt, tp: (b, 0, 0))

    grid_spec = pltpu.PrefetchScalarGridSpec(
        num_scalar_prefetch=2,
        grid=(bd,),
        in_specs=[tok_spec(), tok_spec(), tok_spec(),
                  pl.BlockSpec((1, n_pages, H_ATT, PAGE_SIZE), lambda b, pt, tp: (b, 0, 0, 0)),
                  pl.BlockSpec(memory_space=pl.ANY)],
        out_specs=pl.BlockSpec((1, H_ATT, 1, HEAD_DIM), lambda b, pt, tp: (b, 0, 0, 0)),
        scratch_shapes=[pltpu.VMEM((2, H_ATT * N_SEL_PAGES, HEAD_DIM, PAGE_SIZE), F32),
                        pltpu.SemaphoreType.DMA((2,))],
    )
    return pl.pallas_call(
        functools.partial(_moba_sample_body, past=past, n_pages=n_pages),
        grid_spec=grid_spec,
        out_shape=jax.ShapeDtypeStruct((bd, H_ATT, 1, HEAD_DIM), F32),
        compiler_params=_cparams("arbitrary"),
    )(page_table.reshape(-1), top.reshape(-1), q, k_new, v_new, scores, vt_pool)


def _softplus(z):
    return jnp.maximum(z, 0.0) + jnp.log(1.0 + jnp.exp(-jnp.abs(z)))


def _sigmoid(z):
    return 1.0 / (1.0 + jnp.exp(-z))


def _rwkv_token_terms(m, w0, w_up, a0, a_up, g_up, k_k, k_a):
    rw = RWKV_WIDTH
    r, k, v = m[:, 0:rw], m[:, rw:2 * rw], m[:, 2 * rw:3 * rw]
    xw = m[:, 3 * rw:3 * rw + W_LORA]
    xa = m[:, 3 * rw + W_LORA:3 * rw + W_LORA + A_LORA]
    xg = m[:, 3 * rw + W_LORA + A_LORA:]
    w_log = -_softplus(-(w0 + _dot(jnp.tanh(xw), w_up, precision=HIGHEST))) - 0.5
    log_decay = -jnp.exp(w_log)
    a = _sigmoid(a0 + _dot(xa, a_up, precision=HIGHEST))
    g = _bdot(_sigmoid(xg), g_up)
    kk = k * k_k
    k2 = k * (1.0 + (a - 1.0) * k_a)
    return r, log_decay, k2, v, kk, a, g


def _head_norm(kk_h):
    return kk_h * lax.rsqrt(jnp.maximum(jnp.sum(kk_h * kk_h, axis=-1, keepdims=True), 1e-24))


def _rwkv_head_out(y, r_h, k_h, v_h, g_h, rk_h, gng_h, gnb_h):
    mu = jnp.mean(y, axis=-1, keepdims=True)
    d = y - mu
    var = jnp.mean(d * d, axis=-1, keepdims=True)
    yn = d * lax.rsqrt(var + GN_EPS) * gng_h + gnb_h
    bonus = jnp.sum(r_h * k_h * rk_h, axis=-1, keepdims=True) * v_h
    return (yn + bonus) * g_h


PAIR = 2 * RWKV_HEAD


def _pair_blockdiag(y):
    left = lax.broadcasted_iota(jnp.int32, y.shape, 1) < RWKV_HEAD
    zero = jnp.zeros_like(y)
    return jnp.concatenate([jnp.where(left, y, zero), jnp.where(left, zero, y)], axis=0)


def _pair_nn(x, y):
    return _dot(x.astype(BF16), _pair_blockdiag(y.astype(BF16)))


def _pair_nt(x, y):
    return _dot(x.astype(BF16), _pair_blockdiag(y.astype(BF16)), _NT)


def _pair_tn(x, y):
    full = _dot(x.astype(BF16), y.astype(BF16), _TN)
    left = lax.broadcasted_iota(jnp.int32, (RWKV_HEAD, PAIR), 1) < RWKV_HEAD
    return jnp.where(left, full[:RWKV_HEAD], full[RWKV_HEAD:])


def _pair_sum(x):
    left = lax.broadcasted_iota(jnp.int32, x.shape, 1) < RWKV_HEAD
    s_a = jnp.sum(jnp.where(left, x, 0.0), axis=-1, keepdims=True)
    s_b = jnp.sum(jnp.where(left, 0.0, x), axis=-1, keepdims=True)
    return jnp.where(left, s_a, s_b)


def _rwkv_pair_out(y, r_p, k_p, v_p, g_p, rk_p, gng_p, gnb_p):
    inv_n = 1.0 / RWKV_HEAD
    d = y - _pair_sum(y) * inv_n
    var = _pair_sum(d * d) * inv_n
    yn = d * lax.rsqrt(var + GN_EPS) * gng_p + gnb_p
    return (yn + _pair_sum(r_p * k_p * rk_p) * v_p) * g_p


def _rwkv_prompt_body(rw_ref, prev_ref, s0_ref, mu_ref, w0_ref, wup_ref, a0_ref, aup_ref, gup_ref,
                      kk_ref, ka_ref, rk_ref, gng_ref, gnb_ref, y_ref, sfin_ref, state_s, prev_s):
    c = pl.program_id(1)
    ch = RWKV_CHUNK
    tile = rw_ref.shape[1]
    n_ch = tile // ch
    n_pairs = H_RWKV // 2
    pairs = [slice(p * PAIR, (p + 1) * PAIR) for p in range(n_pairs)]

    @pl.when(c == 0)
    def _():
        for p in range(n_pairs):
            state_s[p] = jnp.concatenate([s0_ref[0, (2 * p) * RWKV_HEAD:(2 * p + 1) * RWKV_HEAD, :],
                                          s0_ref[0, (2 * p + 1) * RWKV_HEAD:(2 * p + 2) * RWKV_HEAD, :]], axis=1)
        prev_s[...] = prev_ref[0]

    rw = rw_ref[0]
    row = lax.broadcasted_iota(jnp.int32, rw.shape, 0)
    rw_prev = jnp.where(row == 0, prev_s[...], pltpu.roll(rw, 1, 0))
    prev_s[...] = rw[tile - 1:tile, :]
    m = rw + (rw_prev - rw) * mu_ref[...]
    r, log_decay, k2, v, kk, a, g = _rwkv_token_terms(
        m, w0_ref[...], wup_ref[...], a0_ref[...], aup_ref[...], gup_ref[...], kk_ref[...], ka_ref[...])

    ti = lax.broadcasted_iota(jnp.int32, (tile, tile), 0)
    si = lax.broadcasted_iota(jnp.int32, (tile, tile), 1)
    in_chunk_lower = (si <= ti) & (si // ch == ti // ch)
    cs = _dot(in_chunk_lower.astype(F32), log_decay, precision=HIGHEST)
    cs_end = cs[ch - 1:ch, :]
    rows = lax.broadcasted_iota(jnp.int32, cs.shape, 0)
    for j in range(1, n_ch):
        cs_end = jnp.where(rows >= j * ch, cs[(j + 1) * ch - 1:(j + 1) * ch, :], cs_end)
    gam = jnp.exp(cs)
    gam_prev = jnp.exp(cs - log_decay)
    gam_inv = jnp.exp(-cs)
    gam_tail = jnp.exp(cs_end - cs)

    t_p = lax.broadcasted_iota(jnp.int32, (ch, PAIR), 0)
    s_p = lax.broadcasted_iota(jnp.int32, (ch, PAIR), 1) % RWKV_HEAD
    lower = s_p <= t_p
    strict = s_p < t_p
    eye = (s_p == t_p).astype(F32)

    units = [(slice(j * ch, (j + 1) * ch), ps) for j in range(n_ch) for ps in pairs]
    kk_n = [kk[ts, ps] for ts, ps in units]
    kk_n = [x * lax.rsqrt(jnp.maximum(_pair_sum(x * x), 1e-24)) for x in kk_n]
    a_t = [-n * gam_prev[ts, ps] for n, (ts, ps) in zip(kk_n, units)]
    b_h = [n * a[ts, ps] for n, (ts, ps) in zip(kk_n, units)]
    b_t = [x * gam_inv[ts, ps] for x, (ts, ps) in zip(b_h, units)]
    k_t = [k2[ts, ps] * gam_inv[ts, ps] for ts, ps in units]
    r_t = [r[ts, ps] * gam[ts, ps] for ts, ps in units]
    v_h = [v[ts, ps] for ts, ps in units]
    l_ab = [jnp.where(strict, _pair_nt(x, z), 0.0) for x, z in zip(a_t, b_t)]
    l_ak = [jnp.where(strict, _pair_nt(x, z), 0.0) for x, z in zip(a_t, k_t)]
    m_rb = [jnp.where(lower, _pair_nt(x, z), 0.0) for x, z in zip(r_t, b_t)]
    m_rk = [jnp.where(lower, _pair_nt(x, z), 0.0) for x, z in zip(r_t, k_t)]
    inv = [eye + x for x in l_ab]
    pw = list(l_ab)
    span = 2
    while span < ch:
        pw = [_pair_nn(x, x) for x in pw]
        inv = [i + _pair_nn(i, x) for i, x in zip(inv, pw)]
        span *= 2
    lv = [_pair_nn(l, vv) for l, vv in zip(l_ak, v_h)]
    t_a = [_pair_nn(i, x) for i, x in zip(inv, a_t)]
    t_l = [_pair_nn(i, x) for i, x in zip(inv, lv)]
    b_e = [x * gam_tail[u_] for x, u_ in zip(b_h, units)]
    k_e = [k2[u_] * gam_tail[u_] for u_ in units]
    r_a = [x + _pair_nn(mb, ta) for x, mb, ta in zip(r_t, m_rb, t_a)]
    y_0 = [_pair_nn(mb, tl) + _pair_nn(mk, vv) for mb, tl, mk, vv in zip(m_rb, t_l, m_rk, v_h)]
    m_s = [_pair_tn(ta, be) for ta, be in zip(t_a, b_e)]
    c_s = [_pair_tn(tl, be) + _pair_tn(vv, ke) for tl, be, vv, ke in zip(t_l, b_e, v_h, k_e)]
    s_cur = [state_s[p] for p in range(n_pairs)]
    for j in range(n_ch):
        sel = range(j * n_pairs, (j + 1) * n_pairs)
        y = [_pair_nt(r_a[i], s) + y_0[i] for i, s in zip(sel, s_cur)]
        s_cur = [s * gam[(j + 1) * ch - 1:(j + 1) * ch, units[i][1]] + _pair_nn(s, m_s[i]) + c_s[i]
                 for i, s in zip(sel, s_cur)]
        for i, yy in zip(sel, y):
            ts, ps = units[i]
            y_ref[0, ts, ps] = _rwkv_pair_out(yy, r[ts, ps], k2[ts, ps], v_h[i], g[ts, ps],
                                              rk_ref[:, ps], gng_ref[:, ps], gnb_ref[:, ps])
    for p in range(n_pairs):
        state_s[p] = s_cur[p]

    @pl.when(c == pl.num_programs(1) - 1)
    def _():
        for p in range(n_pairs):
            sfin_ref[0, (2 * p) * RWKV_HEAD:(2 * p + 1) * RWKV_HEAD, :] = state_s[p][:, :RWKV_HEAD]
            sfin_ref[0, (2 * p + 1) * RWKV_HEAD:(2 * p + 2) * RWKV_HEAD, :] = state_s[p][:, RWKV_HEAD:]


def _rwkv_prompt(rw, prev_row, s0, p, tile):
    bsz, t, _ = rw.shape
    assert t % tile == 0 and tile % RWKV_CHUNK == 0 and H_RWKV % 2 == 0
    hn = H_RWKV * RWKV_HEAD
    vec = lambda n: _full((1, n))
    return pl.pallas_call(
        _rwkv_prompt_body,
        grid=(bsz, t // tile),
        in_specs=[pl.BlockSpec((1, tile, RWKV_COLS), lambda i, j: (i, j, 0)),
                  pl.BlockSpec((1, 1, RWKV_COLS), lambda i, j: (i, 0, 0)),
                  pl.BlockSpec((1, hn, RWKV_HEAD), lambda i, j: (i, 0, 0)),
                  vec(RWKV_COLS), vec(RWKV_WIDTH), _full((W_LORA, RWKV_WIDTH)), vec(RWKV_WIDTH),
                  _full((A_LORA, RWKV_WIDTH)), _full((G_LORA, RWKV_WIDTH)),
                  vec(RWKV_WIDTH), vec(RWKV_WIDTH), vec(RWKV_WIDTH), vec(RWKV_WIDTH), vec(RWKV_WIDTH)],
        out_specs=[pl.BlockSpec((1, tile, RWKV_WIDTH), lambda i, j: (i, j, 0)),
                   pl.BlockSpec((1, hn, RWKV_HEAD), lambda i, j: (i, 0, 0))],
        out_shape=[jax.ShapeDtypeStruct((bsz, t, RWKV_WIDTH), F32),
                   jax.ShapeDtypeStruct((bsz, hn, RWKV_HEAD), F32)],
        scratch_shapes=[pltpu.VMEM((H_RWKV // 2, RWKV_HEAD, PAIR), F32), pltpu.VMEM((1, RWKV_COLS), F32)],
        compiler_params=_cparams("parallel", "arbitrary"),
    )(rw, prev_row, s0, p["mu_shift"], p["w0"], p["w_up"], p["a0"], p["a_up"], p["g_up"],
      p["k_k"], p["k_a"], p["r_k"], p["gn_g"], p["gn_b"])


def _rows8(*rows):
    ri = lax.broadcasted_iota(jnp.int32, (SUBLANES, rows[0].shape[1]), 0)
    out = jnp.zeros((SUBLANES, rows[0].shape[1]), F32)
    for i, x in enumerate(rows):
        out = jnp.where(ri == i, x, out)
    return out


STEP_ROWS = SUBLANES


def _rwkv_step_body(rw_ref, prev_ref, s0_ref, mu_ref, w0_ref, wup_ref, a0_ref, aup_ref, gup_ref,
                    kk_ref, ka_ref, rk_ref, gng_ref, gnb_ref, y_ref, sfin_ref):
    rw = rw_ref[...]
    m = rw + (prev_ref[...] - rw) * mu_ref[...]
    r, log_decay, k2, v, kk, a, g = _rwkv_token_terms(
        m, w0_ref[...], wup_ref[...], a0_ref[...], aup_ref[...], gup_ref[...], kk_ref[...], ka_ref[...])
    decay = jnp.exp(log_decay)
    heads = [slice(h * RWKV_HEAD, (h + 1) * RWKV_HEAD) for h in range(H_RWKV)]
    kkn = [_head_norm(kk[:, hs]) for hs in heads]
    b_h = [n * a[:, hs] for n, hs in zip(kkn, heads)]
    units = [(i, h) for i in range(STEP_ROWS) for h in range(H_RWKV)]
    row = lambda x, i: x[i:i + 1, :]
    s_0 = [s0_ref[i, heads[h], :] for i, h in units]
    sa = [_dot(_rows8(-row(kkn[h], i)), s, _NT, HIGHEST)[0:1] for (i, h), s in zip(units, s_0)]
    s_1 = [s * row(decay[:, heads[h]], i)
           + _dot(_rows8(x, row(v[:, heads[h]], i)), _rows8(row(b_h[h], i), row(k2[:, heads[h]], i)), _TN, HIGHEST)
           for (i, h), s, x in zip(units, s_0, sa)]
    y = [_dot(_rows8(row(r[:, heads[h]], i)), s, _NT, HIGHEST)[0:1] for (i, h), s in zip(units, s_1)]
    for (i, h), s in zip(units, s_1):
        sfin_ref[i, heads[h], :] = s
    seq = lax.broadcasted_iota(jnp.int32, (STEP_ROWS, RWKV_HEAD), 0)
    for h, hs in enumerate(heads):
        y_h = jnp.zeros((STEP_ROWS, RWKV_HEAD), F32)
        for i in range(STEP_ROWS):
            y_h = jnp.where(seq == i, y[i * H_RWKV + h], y_h)
        y_ref[:, hs] = _rwkv_head_out(y_h, r[:, hs], k2[:, hs], v[:, hs], g[:, hs],
                                      rk_ref[:, hs], gng_ref[:, hs], gnb_ref[:, hs])


def _rwkv_step(rw, prev_row, s0, p):
    bd = rw.shape[0]
    assert bd % STEP_ROWS == 0
    hn = H_RWKV * RWKV_HEAD
    vec = lambda n: _full((1, n))
    rows = lambda n: pl.BlockSpec((STEP_ROWS, n), lambda i: (i, 0))
    state = pl.BlockSpec((STEP_ROWS, hn, RWKV_HEAD), lambda i: (i, 0, 0))
    return pl.pallas_call(
        _rwkv_step_body,
        grid=(bd // STEP_ROWS,),
        in_specs=[rows(RWKV_COLS), rows(RWKV_COLS), state,
                  vec(RWKV_COLS), vec(RWKV_WIDTH), _full((W_LORA, RWKV_WIDTH)), vec(RWKV_WIDTH),
                  _full((A_LORA, RWKV_WIDTH)), _full((G_LORA, RWKV_WIDTH)),
                  vec(RWKV_WIDTH), vec(RWKV_WIDTH), vec(RWKV_WIDTH), vec(RWKV_WIDTH), vec(RWKV_WIDTH)],
        out_specs=[rows(RWKV_WIDTH), state],
        out_shape=[jax.ShapeDtypeStruct((bd, RWKV_WIDTH), F32),
                   jax.ShapeDtypeStruct((bd, hn, RWKV_HEAD), F32)],
        compiler_params=_cparams("parallel"),
    )(rw, prev_row, s0, p["mu_shift"], p["w0"], p["w_up"], p["a0"], p["a_up"], p["g_up"],
      p["k_k"], p["k_a"], p["r_k"], p["gn_g"], p["gn_b"])


def _row(x):
    return x.reshape(1, -1)


def kernel(x_prompt, x_sample, mem_prompt, cache_k_pool, cache_v_pool, page_table, cache_mem_k, cache_mem_v, state_wkv, state_shift, w_in, mu_shift, w0, w_up, a0, a_up, g_up, k_k, k_a, r_k, gn_g, gn_b, w_out, ln1_g, ln1_b, w_q_mem, w_kv_mem, w_o_mem, ln2_g, ln2_b, w_ff1, w_ff2, ln3_g, ln3_b):
    assert w_in.shape[0] == DEPTH == 1
    bsz, t, _ = x_prompt.shape
    bd = x_sample.shape[0]
    hn = H_RWKV * RWKV_HEAD

    w_in_b = w_in[0].astype(BF16)
    w_out_b = w_out[0].astype(BF16)
    wq_b = w_q_mem[0].astype(BF16)
    wkv_b = w_kv_mem[0].astype(BF16)
    wo_b = w_o_mem[0].astype(BF16)
    w1_b = w_ff1[0].astype(BF16)
    w2_b = w_ff2[0].astype(BF16)
    ln1 = (_row(ln1_g[0]), _row(ln1_b[0]))
    ln2 = (_row(ln2_g[0]), _row(ln2_b[0]))
    ln3 = (_row(ln3_g[0]), _row(ln3_b[0]))
    rwkv_p = dict(mu_shift=_row(mu_shift[0]), w0=_row(w0[0]), w_up=w_up[0], a0=_row(a0[0]), a_up=a_up[0],
                  g_up=g_up[0], k_k=_row(k_k[0]), k_a=_row(k_a[0]), r_k=_row(r_k[0]),
                  gn_g=_row(gn_g[0]), gn_b=_row(gn_b[0]))

    qt_p, kt_p, vt_p, kb_p, rw_p3, kmean_p = _proj_prompt(x_prompt, w_in_b, tm=512)
    y_att_p = _moba_prompt(qt_p, kb_p, vt_p, kmean_p)
    y_rw_p, s_p = _rwkv_prompt(rw_p3, jnp.zeros((bsz, 1, RWKV_COLS), F32), jnp.zeros((bsz, hn, RWKV_HEAD), F32), rwkv_p,
                               tile=RWKV_TILE)
    mk_p, mv_p, mk_p4, mv_p4 = _mem_kv(mem_prompt, wkv_b)
    x2_p = _mix_memx_prompt(x_prompt, y_att_p, y_rw_p, mk_p, mv_p, w_out_b[:ATT_WIDTH], w_out_b[ATT_WIDTH:],
                            wq_b, wo_b, ln1, ln2, tm=512)

    kt_pool = cache_k_pool[0].transpose(0, 2, 3, 1)
    vt_pool = cache_v_pool[0].transpose(0, 2, 3, 1)
    xs = x_sample.reshape(bd, D_MODEL)
    q_s = _mm_f32(xs, w_in[0], ATT_WIDTH)
    k_s, v_s, rw_s = _mm_multi(xs, w_in_b, (ATT_WIDTH, ATT_WIDTH, RWKV_COLS), tm=bd, first_col=ATT_WIDTH)
    q_s3, k_s3, v_s3 = (z.reshape(bd, H_ATT, HEAD_DIM) for z in (q_s, k_s, v_s))
    y_p, scores_s, top = _mlp_scores(x2_p.reshape(bsz * t, D_MODEL), w1_b, w2_b, *ln3, q_s3, kt_pool, page_table,
                                     tm=MLP_TM)
    y_att_s = _moba_sample(q_s3, k_s3, v_s3, scores_s, vt_pool, page_table, top[:, :, :MOBA_TOPK])
    y_rw_s, s_s = _rwkv_step(rw_s, state_shift[0], state_wkv[0].reshape(bd, hn, RWKV_HEAD), rwkv_p)
    x1_s = _mm_res_ln([y_att_s.reshape(bd, ATT_WIDTH), y_rw_s],
                      [w_out_b[:ATT_WIDTH], w_out_b[ATT_WIDTH:]], xs, *ln1, tm=bd)
    (qm_s,) = _mm_multi(x1_s, wq_b, (D_MODEL,), tm=bd)
    att_s = _memx_sample(qm_s.reshape(bd, 1, D_MODEL), cache_mem_k[0], cache_mem_v[0])
    x2_s = _mm_res_ln([att_s.reshape(bd, D_MODEL)], [wo_b], x1_s, *ln2, tm=bd)
    y_s = _mlp(x2_s, w1_b, w2_b, *ln3, tm=bd)

    return (y_p.reshape(bsz, t, D_MODEL), y_s.reshape(bd, 1, D_MODEL),
            kt_p.reshape(bsz, H_ATT, HEAD_DIM, t).transpose(0, 3, 1, 2)[None],
            vt_p.reshape(bsz, H_ATT, HEAD_DIM, t).transpose(0, 3, 1, 2)[None],
            k_s.reshape(1, bd, 1, H_ATT, HEAD_DIM), v_s.reshape(1, bd, 1, H_ATT, HEAD_DIM),
            mk_p4[None], mv_p4[None],
            s_p.reshape(1, bsz, H_RWKV, RWKV_HEAD, RWKV_HEAD), rw_p3[:, -1][None],
            s_s.reshape(1, bd, H_RWKV, RWKV_HEAD, RWKV_HEAD), rw_s[None])
```

```python
import functools

import jax
import jax.numpy as jnp
from jax import lax
from jax.experimental import pallas as pl
from jax.experimental.pallas import tpu as pltpu

F32 = jnp.float32
BF16 = jnp.bfloat16
HIGHEST = lax.Precision.HIGHEST

D_MODEL = 1024
PAGE_SIZE = 128
ATT_WIDTH = 512
RWKV_WIDTH = 512
HEAD_DIM = 64
H_ATT = 8
MOBA_BLOCK = 256
MOBA_TOPK = 3
RWKV_HEAD = 64
H_RWKV = 8
W_LORA = 64
A_LORA = 64
G_LORA = 128
RWKV_COLS = 3 * RWKV_WIDTH + W_LORA + A_LORA + G_LORA
PROJ_COLS = 3 * ATT_WIDTH + RWKV_COLS
N_MEM = 256
MEM_HEADS = 4
MEM_HEAD_DIM = 256
D_FF = 4 * D_MODEL
LN_EPS = 1e-5
GN_EPS = 64e-5
DEPTH = 1
ALPHA = (2.0 * DEPTH) ** 0.25

MLP_TM = 512
RWKV_TILE = 512
RWKV_CHUNK = 64
NEG = -1e30
VMEM_LIMIT = 56 * 1024 * 1024
LANES = 128
SUBLANES = 8

_NT = (((1,), (1,)), ((), ()))
_TN = (((0,), (0,)), ((), ()))


def _cparams(*sem):
    return pltpu.CompilerParams(dimension_semantics=sem, vmem_limit_bytes=VMEM_LIMIT)


def _dot(a, b, dims=None, precision=None):
    if dims is None:
        return jnp.dot(a, b, preferred_element_type=F32, precision=precision)
    return lax.dot_general(a, b, dims, preferred_element_type=F32, precision=precision)


def _bdot(a, b, dims=None):
    return _dot(a.astype(BF16), b.astype(BF16), dims)


def _layer_norm(z, g, b):
    mu = jnp.mean(z, axis=-1, keepdims=True)
    d = z - mu
    var = jnp.mean(d * d, axis=-1, keepdims=True)
    return d * lax.rsqrt(var + LN_EPS) * g + b


def _full(shape):
    n = len(shape)
    return pl.BlockSpec(shape, lambda *_: (0,) * n)


def _col_chunks(n, width=512):
    out, c = [], 0
    while c < n:
        w = min(width, n - c)
        out.append((c, w))
        c += w
    return out


def _mm_multi_body(x_ref, w_ref, *o_refs, col0s):
    xb = x_ref[...].astype(BF16)
    for o_ref, c0 in zip(o_refs, col0s):
        for c, w in _col_chunks(o_ref.shape[1]):
            o_ref[:, c:c + w] = _dot(xb, w_ref[:, c0 + c:c0 + c + w])


def _mm_f32_body(x_ref, w_ref, o_ref):
    o_ref[...] = _dot(x_ref[...], w_ref[...], precision=HIGHEST)


def _mm_f32(x, w, n_cols):
    m, k = x.shape
    return pl.pallas_call(
        _mm_f32_body,
        grid=(1,),
        in_specs=[_full(x.shape), pl.BlockSpec((k, n_cols), lambda i: (0, 0))],
        out_specs=_full((m, n_cols)),
        out_shape=jax.ShapeDtypeStruct((m, n_cols), F32),
        compiler_params=_cparams("arbitrary"),
    )(x, w)


def _mm_multi(x, w_bf16, widths, tm, first_col=0):
    m, k = x.shape
    col0s, c = [], first_col
    for wd in widths:
        col0s.append(c)
        c += wd
    assert c == w_bf16.shape[1] and m % tm == 0
    return pl.pallas_call(
        functools.partial(_mm_multi_body, col0s=tuple(col0s)),
        grid=(m // tm,),
        in_specs=[pl.BlockSpec((tm, k), lambda i: (i, 0)), _full(w_bf16.shape)],
        out_specs=[pl.BlockSpec((tm, wd), lambda i: (i, 0)) for wd in widths],
        out_shape=[jax.ShapeDtypeStruct((m, wd), F32) for wd in widths],
        compiler_params=_cparams("parallel"),
    )(x, w_bf16)


def _proj_prompt_body(x_ref, w_ref, qt_ref, kt_ref, vt_ref, kb_ref, rw_ref, kmean_ref):
    j = pl.program_id(1)
    tm = x_ref.shape[1]
    xb = x_ref[0].astype(BF16)
    aw = ATT_WIDTH
    q = _dot(xb, w_ref[:, 0:aw])
    k = _dot(xb, w_ref[:, aw:2 * aw])
    v = _dot(xb, w_ref[:, 2 * aw:3 * aw])
    qt_ref[0] = q.T
    kt_ref[0] = k.T
    vt_ref[0] = v.T
    kb_ref[0] = k.astype(BF16)
    for c, w in _col_chunks(RWKV_COLS):
        rw_ref[0, :, c:c + w] = _dot(xb, w_ref[:, 3 * aw + c:3 * aw + c + w])
    blk_per_tile = tm // MOBA_BLOCK
    for i in range(blk_per_tile):
        kmean_ref[0, pl.ds(j * blk_per_tile + i, 1), :] = jnp.mean(
            k[i * MOBA_BLOCK:(i + 1) * MOBA_BLOCK], axis=0, keepdims=True)


def _proj_prompt(x, w_bf16, tm):
    bsz, t, _ = x.shape
    assert t % tm == 0 and tm % MOBA_BLOCK == 0
    n_blk = t // MOBA_BLOCK
    tr_spec = pl.BlockSpec((1, ATT_WIDTH, tm), lambda i, j: (i, 0, j))
    tr_shape = jax.ShapeDtypeStruct((bsz, ATT_WIDTH, t), F32)
    return pl.pallas_call(
        _proj_prompt_body,
        grid=(bsz, t // tm),
        in_specs=[pl.BlockSpec((1, tm, D_MODEL), lambda i, j: (i, j, 0)), _full(w_bf16.shape)],
        out_specs=[tr_spec, tr_spec, tr_spec,
                   pl.BlockSpec((1, tm, ATT_WIDTH), lambda i, j: (i, j, 0)),
                   pl.BlockSpec((1, tm, RWKV_COLS), lambda i, j: (i, j, 0)),
                   pl.BlockSpec((1, n_blk, ATT_WIDTH), lambda i, j: (i, 0, 0))],
        out_shape=[tr_shape, tr_shape, tr_shape,
                   jax.ShapeDtypeStruct((bsz, t, ATT_WIDTH), BF16),
                   jax.ShapeDtypeStruct((bsz, t, RWKV_COLS), F32),
                   jax.ShapeDtypeStruct((bsz, n_blk, ATT_WIDTH), F32)],
        compiler_params=_cparams("parallel", "arbitrary"),
    )(x, w_bf16)


def _mm_res_ln_body(*refs, n_in):
    x_refs = refs[:n_in]
    w_refs = refs[n_in:2 * n_in]
    res_ref, g_ref, b_ref, o_ref = refs[2 * n_in:]
    acc = _dot(x_refs[0][...].astype(BF16), w_refs[0][...])
    for x_ref, w_ref in zip(x_refs[1:], w_refs[1:]):
        acc = acc + _dot(x_ref[...].astype(BF16), w_ref[...])
    o_ref[...] = _layer_norm(ALPHA * res_ref[...] + acc, g_ref[...], b_ref[...])


def _mm_res_ln(xs, ws, res, g, b, tm):
    m = res.shape[0]
    n_in = len(xs)
    assert m % tm == 0
    in_specs = [pl.BlockSpec((tm, x.shape[1]), lambda i: (i, 0)) for x in xs]
    in_specs += [_full(w.shape) for w in ws]
    in_specs += [pl.BlockSpec((tm, D_MODEL), lambda i: (i, 0)), _full((1, D_MODEL)), _full((1, D_MODEL))]
    return pl.pallas_call(
        functools.partial(_mm_res_ln_body, n_in=n_in),
        grid=(m // tm,),
        in_specs=in_specs,
        out_specs=pl.BlockSpec((tm, D_MODEL), lambda i: (i, 0)),
        out_shape=jax.ShapeDtypeStruct((m, D_MODEL), F32),
        compiler_params=_cparams("parallel"),
    )(*xs, *ws, res, g, b)


def _mlp_body(x_ref, w1_ref, w2_ref, g_ref, b_ref, o_ref):
    x = x_ref[...]
    xb = x.astype(BF16)
    acc = jnp.zeros(x.shape, F32)
    for c, w in _col_chunks(D_FF):
        h = jnp.maximum(_dot(xb, w1_ref[:, c:c + w]), 0.0)
        acc = acc + _dot((h * h).astype(BF16), w2_ref[c:c + w, :])
    o_ref[...] = _layer_norm(ALPHA * x + acc, g_ref[...], b_ref[...])


def _mlp(x, w1, w2, g, b, tm):
    m = x.shape[0]
    assert m % tm == 0
    return pl.pallas_call(
        _mlp_body,
        grid=(m // tm,),
        in_specs=[pl.BlockSpec((tm, D_MODEL), lambda i: (i, 0)), _full(w1.shape), _full(w2.shape),
                  _full((1, D_MODEL)), _full((1, D_MODEL))],
        out_specs=pl.BlockSpec((tm, D_MODEL), lambda i: (i, 0)),
        out_shape=jax.ShapeDtypeStruct((m, D_MODEL), F32),
        compiler_params=_cparams("parallel"),
    )(x, w1, w2, g, b)


def _mem_kv_body(x_ref, w_ref, k2_ref, v2_ref, k4_ref, v4_ref):
    xb = x_ref[0].astype(BF16)
    for o2_ref, o4_ref, c0 in ((k2_ref, k4_ref, 0), (v2_ref, v4_ref, D_MODEL)):
        for h in range(MEM_HEADS):
            cols = slice(h * MEM_HEAD_DIM, (h + 1) * MEM_HEAD_DIM)
            part = _dot(xb, w_ref[:, c0 + h * MEM_HEAD_DIM:c0 + (h + 1) * MEM_HEAD_DIM])
            o2_ref[0, :, cols] = part
            o4_ref[0, :, h, :] = part


def _mem_kv(mem, w_bf16):
    bsz = mem.shape[0]
    flat = pl.BlockSpec((1, N_MEM, D_MODEL), lambda i: (i, 0, 0))
    heads = pl.BlockSpec((1, N_MEM, MEM_HEADS, MEM_HEAD_DIM), lambda i: (i, 0, 0, 0))
    flat_t = jax.ShapeDtypeStruct((bsz, N_MEM, D_MODEL), F32)
    heads_t = jax.ShapeDtypeStruct((bsz, N_MEM, MEM_HEADS, MEM_HEAD_DIM), F32)
    return pl.pallas_call(
        _mem_kv_body,
        grid=(bsz,),
        in_specs=[flat, _full(w_bf16.shape)],
        out_specs=[flat, flat, heads, heads],
        out_shape=[flat_t, flat_t, heads_t, heads_t],
        compiler_params=_cparams("parallel"),
    )(mem, w_bf16)


def _mem_attend(q, k_head, v_head):
    heads = range(MEM_HEADS)
    s = [_bdot(q[:, h * MEM_HEAD_DIM:(h + 1) * MEM_HEAD_DIM], k_head(h), _NT) for h in heads]
    p = [jnp.exp(x - jnp.max(x, axis=-1, keepdims=True)) for x in s]
    l = [jnp.sum(x, axis=-1, keepdims=True) for x in p]
    return jnp.concatenate([_bdot(x, v_head(h)) / z for h, x, z in zip(heads, p, l)], axis=-1)


def _mix_memx_prompt_body(x_ref, ya_ref, yr_ref, mk_ref, mv_ref, wa_ref, wr_ref, wq_ref, wo_ref,
                          g1_ref, b1_ref, g2_ref, b2_ref, o_ref):
    mix = _dot(ya_ref[0].astype(BF16), wa_ref[...]) + _dot(yr_ref[0].astype(BF16), wr_ref[...])
    x1 = _layer_norm(ALPHA * x_ref[0] + mix, g1_ref[...], b1_ref[...])
    q = _dot(x1.astype(BF16), wq_ref[...]) * (MEM_HEAD_DIM ** -0.5)
    cols = lambda h: slice(h * MEM_HEAD_DIM, (h + 1) * MEM_HEAD_DIM)
    att = _mem_attend(q, lambda h: mk_ref[0, :, cols(h)], lambda h: mv_ref[0, :, cols(h)])
    acc = _dot(att.astype(BF16), wo_ref[...])
    o_ref[0] = _layer_norm(ALPHA * x1 + acc, g2_ref[...], b2_ref[...])


def _mix_memx_prompt(x, y_att, y_rw, mk, mv, w_att, w_rw, wq, wo, ln1, ln2, tm):
    bsz, t, _ = x.shape
    assert t % tm == 0
    row = lambda w: pl.BlockSpec((1, tm, w), lambda i, j: (i, j, 0))
    mem = pl.BlockSpec((1, N_MEM, D_MODEL), lambda i, j: (i, 0, 0))
    vec = _full((1, D_MODEL))
    return pl.pallas_call(
        _mix_memx_prompt_body,
        grid=(bsz, t // tm),
        in_specs=[row(D_MODEL), row(ATT_WIDTH), row(RWKV_WIDTH), mem, mem,
                  _full(w_att.shape), _full(w_rw.shape), _full(wq.shape), _full(wo.shape), vec, vec, vec, vec],
        out_specs=row(D_MODEL),
        out_shape=jax.ShapeDtypeStruct(x.shape, F32),
        compiler_params=_cparams("parallel", "parallel"),
    )(x, y_att, y_rw, mk, mv, w_att, w_rw, wq, wo, *ln1, *ln2)


def _memx_sample_body(q_ref, mk_ref, mv_ref, o_ref):
    n_rows = N_MEM * MEM_HEADS
    k_all = mk_ref[0].reshape(n_rows, MEM_HEAD_DIM)
    v_all = mv_ref[0].reshape(n_rows, MEM_HEAD_DIM)
    q = q_ref[0] * (MEM_HEAD_DIM ** -0.5)
    q_rows = [q[:, h * MEM_HEAD_DIM:(h + 1) * MEM_HEAD_DIM] for h in range(MEM_HEADS)]
    q8 = jnp.concatenate(q_rows + [jnp.zeros((SUBLANES - MEM_HEADS, MEM_HEAD_DIM), F32)], axis=0)
    s = _bdot(q8, k_all, _NT)
    row = lax.broadcasted_iota(jnp.int32, s.shape, 0)
    own = (lax.broadcasted_iota(jnp.int32, s.shape, 1) % MEM_HEADS) == row
    s = jnp.where(own, s, NEG)
    p = jnp.where(own, jnp.exp(s - jnp.max(s, axis=-1, keepdims=True)), 0.0)
    l = jnp.sum(p, axis=-1, keepdims=True)
    o8 = _bdot(p, v_all) / jnp.where(l > 0.0, l, 1.0)
    o_ref[0] = jnp.concatenate([o8[h:h + 1] for h in range(MEM_HEADS)], axis=-1)


def _memx_sample(q, mk, mv):
    bd = q.shape[0]
    mem_spec = pl.BlockSpec((1, N_MEM, MEM_HEADS, MEM_HEAD_DIM), lambda i: (i, 0, 0, 0))
    return pl.pallas_call(
        _memx_sample_body,
        grid=(bd,),
        in_specs=[pl.BlockSpec((1, 1, D_MODEL), lambda i: (i, 0, 0)), mem_spec, mem_spec],
        out_specs=pl.BlockSpec((1, 1, D_MODEL), lambda i: (i, 0, 0)),
        out_shape=jax.ShapeDtypeStruct((bd, 1, D_MODEL), F32),
        compiler_params=_cparams("parallel"),
    )(q, mk, mv)


def _top3_rows(g, n_iota, n_lim):
    g = jnp.where(n_iota < n_lim, g, -jnp.inf)
    n_rows = g.shape[0]
    sel = n_iota < 0
    for _ in range(MOBA_TOPK):
        mx = jnp.max(g, axis=0, keepdims=True)
        idx = jnp.min(jnp.where(g == mx, n_iota, n_rows), axis=0, keepdims=True)
        hit = n_iota == idx
        sel = sel | hit
        g = jnp.where(hit, -jnp.inf, g)
    return sel & (n_iota < n_lim)


MOBA_QT = 256
MASK_BIG = 1e30
FEAT_ROWS = SUBLANES


def _moba_prompt_body(qt_ref, kb_ref, vt_ref, kmean_ref, o_ref, vt_s, fe_s, fo_s, acc_s, *, n_blk):
    c = pl.program_id(1)
    half = HEAD_DIM
    lane_k = lax.broadcasted_iota(jnp.int32, (MOBA_BLOCK, 2 * half), 1)
    row_k = lax.broadcasted_iota(jnp.int32, (MOBA_BLOCK, 2 * half), 0)

    @pl.when(c == 0)
    def _():
        for n in range(n_blk):
            vt_s[n] = vt_ref[0, :, n * MOBA_BLOCK:(n + 1) * MOBA_BLOCK].astype(BF16)
            for tbl, f in ((fe_s, lane_k - half), (fo_s, lane_k)):
                feat = jnp.where((f == 0) | (f == 1) | (f == FEAT_ROWS + n), 1.0, 0.0)
                feat = jnp.where(f == 2, row_k.astype(F32), feat)
                feat = jnp.where(f == 3, float(n * MOBA_BLOCK), feat)
                tbl[n] = feat.astype(BF16)

    heads = [slice(h * HEAD_DIM, (h + 1) * HEAD_DIM) for h in range(H_ATT)]
    q_blk = (c * MOBA_QT) // MOBA_BLOCK
    qt = qt_ref[0]
    n_iota = lax.broadcasted_iota(jnp.int32, (n_blk, MOBA_QT), 0)
    gates = [_dot(kmean_ref[0, :, hs], qt[hs, :], precision=HIGHEST) for hs in heads]
    sel = [_top3_rows(g, n_iota, q_blk) | (n_iota == q_blk) for g in gates]
    acc_s[...] = jnp.zeros(acc_s.shape, F32)

    r8 = lax.broadcasted_iota(jnp.int32, (FEAT_ROWS, MOBA_QT), 0)
    q_rel = lax.broadcasted_iota(jnp.int32, (FEAT_ROWS, MOBA_QT), 1).astype(F32)
    chunk_start = (c * MOBA_QT).astype(F32)
    pad = jnp.zeros((half - FEAT_ROWS - n_blk, MOBA_QT), F32)
    q_aug = []
    for h, hs in enumerate(heads):
        slope = 2.0 ** -(h + 1)
        base = jnp.where(r8 == 0, -slope * chunk_start, 0.0)
        base = jnp.where(r8 == 1, -slope * q_rel, base)
        base = jnp.where((r8 == 2) | (r8 == 3), slope, base)
        feat = jnp.concatenate([base, (sel[h].astype(F32) - 1.0) * MASK_BIG, pad], axis=0)
        q_h = qt[hs, :] * (HEAD_DIM ** -0.5)
        q_aug.append(jnp.concatenate([q_h, feat] if h % 2 == 0 else [feat, q_h], axis=0).astype(BF16))

    left = lane_k < half
    causal = (lax.broadcasted_iota(jnp.int32, (MOBA_BLOCK, MOBA_QT), 0)
              <= lax.broadcasted_iota(jnp.int32, (MOBA_BLOCK, MOBA_QT), 1) + (c * MOBA_QT - q_blk * MOBA_BLOCK))
    head_row = lax.broadcasted_iota(jnp.int32, (H_ATT, MOBA_QT), 0)

    def attend(n, m_all, l_all, own):
        rows = pl.ds(pl.multiple_of(n * MOBA_BLOCK, MOBA_BLOCK), MOBA_BLOCK)
        f_even, f_odd = fe_s[n], fo_s[n]
        vblk = vt_s[n]
        k_aug = []
        for j in range(H_ATT // 2):
            k_pair = kb_ref[0, rows, j * 2 * half:(j + 1) * 2 * half]
            k_aug += [jnp.where(left, k_pair, f_even), jnp.where(left, f_odd, k_pair)]
        s = [_dot(k, q) for k, q in zip(k_aug, q_aug)]
        if own:
            s = [jnp.where(causal, x, NEG) for x in s]
        m_old = [m_all[h:h + 1, :] for h in range(H_ATT)]
        m_new = [jnp.maximum(mo, jnp.max(x, axis=0, keepdims=True)) for mo, x in zip(m_old, s)]
        p = [jnp.exp(x - mn) for x, mn in zip(s, m_new)]
        corr = [jnp.exp(mo - mn) for mo, mn in zip(m_old, m_new)]
        pv = [_dot(vblk[hs, :], x.astype(BF16)) for hs, x in zip(heads, p)]
        for h, hs in enumerate(heads):
            acc_s[hs, :] = acc_s[hs, :] * corr[h] + pv[h]
            l_h = l_all[h:h + 1, :] * corr[h] + jnp.sum(p[h], axis=0, keepdims=True)
            l_all = jnp.where(head_row == h, l_h, l_all)
            m_all = jnp.where(head_row == h, m_new[h], m_all)
        return m_all, l_all

    m0 = jnp.full((H_ATT, MOBA_QT), NEG, F32)
    l0 = jnp.zeros((H_ATT, MOBA_QT), F32)
    m_all, l_all = lax.fori_loop(0, q_blk, lambda n, carry: attend(n, *carry, False), (m0, l0))
    m_all, l_all = attend(q_blk, m_all, l_all, True)
    for h, hs in enumerate(heads):
        acc_s[hs, :] = acc_s[hs, :] / l_all[h:h + 1, :]
    o_ref[0] = acc_s[...].T


def _moba_prompt(qt, kb, vt, kmean):
    bsz, t, _ = kb.shape
    assert t % MOBA_BLOCK == 0 and MOBA_BLOCK % MOBA_QT == 0
    n_blk = t // MOBA_BLOCK
    assert n_blk % SUBLANES == 0 and FEAT_ROWS + n_blk <= HEAD_DIM and H_ATT % 2 == 0
    return pl.pallas_call(
        functools.partial(_moba_prompt_body, n_blk=n_blk),
        grid=(bsz, t // MOBA_QT),
        in_specs=[pl.BlockSpec((1, ATT_WIDTH, MOBA_QT), lambda i, j: (i, 0, j)),
                  pl.BlockSpec((1, t, ATT_WIDTH), lambda i, j: (i, 0, 0)),
                  pl.BlockSpec((1, ATT_WIDTH, t), lambda i, j: (i, 0, 0)),
                  pl.BlockSpec((1, n_blk, ATT_WIDTH), lambda i, j: (i, 0, 0))],
        out_specs=pl.BlockSpec((1, MOBA_QT, ATT_WIDTH), lambda i, j: (i, j, 0)),
        out_shape=jax.ShapeDtypeStruct((bsz, t, ATT_WIDTH), F32),
        scratch_shapes=[pltpu.VMEM((n_blk, ATT_WIDTH, MOBA_BLOCK), BF16),
                        pltpu.VMEM((n_blk, MOBA_BLOCK, 2 * HEAD_DIM), BF16),
                        pltpu.VMEM((n_blk, MOBA_BLOCK, 2 * HEAD_DIM), BF16),
                        pltpu.VMEM((ATT_WIDTH, MOBA_QT), F32)],
        compiler_params=_cparams("parallel", "arbitrary"),
    )(qt, kb, vt, kmean)


PAGES_PER_STEP = 32
PAGES_PER_BLOCK = MOBA_BLOCK // PAGE_SIZE
N_SEL_PAGES = MOBA_TOPK * PAGES_PER_BLOCK


def _mlp_scores_body(pt_ref, x_ref, w1_ref, w2_ref, g_ref, b_ref, q_ref, kt_hbm, o_ref, s_ref, top_ref,
                     kbuf, sem, qcol_s, gate_s, *, n_pages, steps_per_seq):
    i = pl.program_id(0)
    part = i % steps_per_seq
    n_groups = n_pages // PAGES_PER_STEP // steps_per_seq
    n_blk = n_pages // PAGES_PER_BLOCK
    blk_per_group = PAGES_PER_STEP // PAGES_PER_BLOCK
    ff_chunks = _col_chunks(D_FF)
    ff_per_group = len(ff_chunks) // n_groups
    blk_per_ff = blk_per_group // ff_per_group

    def page_copies(group, slot):
        return [pltpu.make_async_copy(kt_hbm.at[pt_ref[group * PAGES_PER_STEP + j]], kbuf.at[slot, j], sem.at[slot])
                for j in range(PAGES_PER_STEP)]

    @pl.when(i == 0)
    def _():
        for n, cp in enumerate(page_copies(0, 0)):
            cp.start(priority=n % 2)

    @pl.when(part == 0)
    def _():
        for h in range(H_ATT):
            qcol_s[h] = jnp.broadcast_to(q_ref[0, h:h + 1, :], (PAGE_SIZE, HEAD_DIM)).T
        gate_s[...] = jnp.zeros(gate_s.shape, F32)

    lane = lax.broadcasted_iota(jnp.int32, (H_ATT, LANES), 1)
    gates = gate_s[...]
    x = x_ref[...]
    xb = x.astype(BF16)
    acc = jnp.zeros(x.shape, F32)
    for g in range(n_groups):
        group = i * n_groups + g

        @pl.when(group + 1 < pl.num_programs(0) * n_groups)
        def _():
            for n, cp in enumerate(page_copies(group + 1, (g + 1) % 2)):
                cp.start(priority=n % 2)

        for cp in page_copies(group, g % 2):
            cp.wait()
        for f, (c, w) in enumerate(ff_chunks[g * ff_per_group:(g + 1) * ff_per_group]):
            h = jnp.maximum(_dot(xb, w1_ref[:, c:c + w]), 0.0)
            acc = acc + _dot((h * h).astype(BF16), w2_ref[c:c + w, :])
            for blk in range(f * blk_per_ff, (f + 1) * blk_per_ff):
                blk_sum = jnp.zeros((H_ATT, PAGE_SIZE), F32)
                for jj in range(PAGES_PER_BLOCK):
                    p = blk * PAGES_PER_BLOCK + jj
                    s = jnp.sum(kbuf[g % 2, p] * qcol_s[...], axis=1)
                    s_ref[0, g * PAGES_PER_STEP + p] = s
                    blk_sum = blk_sum + s
                gate = jnp.sum(blk_sum, axis=1, keepdims=True) * (1.0 / MOBA_BLOCK)
                gates = jnp.where(lane == (part * n_groups + g) * blk_per_group + blk, gate, gates)
    o_ref[...] = _layer_norm(ALPHA * x + acc, g_ref[...], b_ref[...])
    gate_s[...] = gates

    @pl.when(part == steps_per_seq - 1)
    def _():
        g_work = jnp.where(lane < n_blk, gates, -jnp.inf)
        out = jnp.zeros((H_ATT, LANES), jnp.int32)
        for r in range(MOBA_TOPK):
            mx = jnp.max(g_work, axis=1, keepdims=True)
            idx = jnp.min(jnp.where(g_work == mx, lane, LANES), axis=1, keepdims=True)
            out = jnp.where(lane == r, idx, out)
            g_work = jnp.where(lane == idx, -jnp.inf, g_work)
        top_ref[0] = out


def _mlp_scores(x, w1, w2, g, b, q, kt_pool, page_table, tm):
    m = x.shape[0]
    bd, n_pages = page_table.shape
    assert m % (bd * tm) == 0 and n_pages % PAGES_PER_STEP == 0 and PAGES_PER_STEP % PAGES_PER_BLOCK == 0
    sps = m // (bd * tm)
    n_groups = n_pages // PAGES_PER_STEP
    assert n_groups % sps == 0 and (n_groups // sps) % 2 == 0 and len(_col_chunks(D_FF)) % (n_groups // sps) == 0
    assert (PAGES_PER_STEP // PAGES_PER_BLOCK) % (len(_col_chunks(D_FF)) // (n_groups // sps)) == 0
    assert MOBA_TOPK <= n_pages // PAGES_PER_BLOCK <= LANES
    resident = lambda shape: pl.BlockSpec(shape, lambda i, pt: (0,) * len(shape), pipeline_mode=pl.Buffered(1))
    grid_spec = pltpu.PrefetchScalarGridSpec(
        num_scalar_prefetch=1,
        grid=(bd * sps,),
        in_specs=[pl.BlockSpec((tm, D_MODEL), lambda i, pt: (i, 0)), resident(w1.shape), resident(w2.shape),
                  resident((1, D_MODEL)), resident((1, D_MODEL)),
                  pl.BlockSpec((1, H_ATT, HEAD_DIM), lambda i, pt: (i // sps, 0, 0)),
                  pl.BlockSpec(memory_space=pl.ANY)],
        out_specs=[pl.BlockSpec((tm, D_MODEL), lambda i, pt: (i, 0)),
                   pl.BlockSpec((1, n_pages // sps, H_ATT, PAGE_SIZE), lambda i, pt: (i // sps, i % sps, 0, 0)),
                   pl.BlockSpec((1, H_ATT, LANES), lambda i, pt: (i // sps, 0, 0))],
        scratch_shapes=[pltpu.VMEM((2, PAGES_PER_STEP, H_ATT, HEAD_DIM, PAGE_SIZE), F32),
                        pltpu.SemaphoreType.DMA((2,)),
                        pltpu.VMEM((H_ATT, HEAD_DIM, PAGE_SIZE), F32), pltpu.VMEM((H_ATT, LANES), F32)],
    )
    return pl.pallas_call(
        functools.partial(_mlp_scores_body, n_pages=n_pages, steps_per_seq=sps),
        grid_spec=grid_spec,
        out_shape=[jax.ShapeDtypeStruct((m, D_MODEL), F32),
                   jax.ShapeDtypeStruct((bd, n_pages, H_ATT, PAGE_SIZE), F32),
                   jax.ShapeDtypeStruct((bd, H_ATT, LANES), jnp.int32)],
        compiler_params=_cparams("arbitrary"),
    )(page_table.reshape(-1), x, w1, w2, g, b, q, kt_pool)


def _moba_sample_body(pt_ref, top_ref, q_ref, kn_ref, vn_ref, s_ref, vt_hbm, o_ref, vbuf, sem, *, past, n_pages):
    b = pl.program_id(0)
    units = [(h, j) for h in range(H_ATT) for j in range(N_SEL_PAGES)]

    def seq_page(seq, h, j):
        return top_ref[(seq * H_ATT + h) * MOBA_TOPK + j // PAGES_PER_BLOCK] * PAGES_PER_BLOCK + j % PAGES_PER_BLOCK

    def tile_copies(seq, slot):
        return [pltpu.make_async_copy(vt_hbm.at[pt_ref[seq * n_pages + seq_page(seq, h, j)], h],
                                      vbuf.at[slot, h * N_SEL_PAGES + j], sem.at[slot]) for h, j in units]

    @pl.when(b == 0)
    def _():
        for n, cp in enumerate(tile_copies(0, 0)):
            cp.start(priority=n % 2)

    @pl.when(b + 1 < pl.num_programs(0))
    def _():
        for n, cp in enumerate(tile_copies(b + 1, (b + 1) % 2)):
            cp.start(priority=n % 2)

    slot = b % 2
    scale = HEAD_DIM ** -0.5
    lane = lax.broadcasted_iota(jnp.int32, (1, PAGE_SIZE), 1)
    logits = []
    for h, j in units:
        page = seq_page(b, h, j)
        dist = (past - (page * PAGE_SIZE + lane)).astype(F32)
        logits.append(s_ref[0, page, h:h + 1, :] * scale - (2.0 ** -(h + 1)) * dist)
    s_self = jnp.sum(q_ref[0] * kn_ref[0], axis=-1, keepdims=True) * scale
    m, p_self = [], []
    for h in range(H_ATT):
        m_h = s_self[h:h + 1]
        for s in logits[h * N_SEL_PAGES:(h + 1) * N_SEL_PAGES]:
            m_h = jnp.maximum(m_h, jnp.max(s, axis=-1, keepdims=True))
        m.append(m_h)
        p_self.append(jnp.exp(s_self[h:h + 1] - m_h))
    p = [jnp.exp(s - m[h]) for s, (h, j) in zip(logits, units)]
    for cp in tile_copies(b, slot):
        cp.wait()
    pv = [_bdot(jnp.broadcast_to(x, (SUBLANES, PAGE_SIZE)), vbuf[slot, h * N_SEL_PAGES + j], _NT)[0:1]
          for x, (h, j) in zip(p, units)]
    for h in range(H_ATT):
        sel = range(h * N_SEL_PAGES, (h + 1) * N_SEL_PAGES)
        l = p_self[h] + sum(jnp.sum(p[i], axis=-1, keepdims=True) for i in sel)
        acc = p_self[h] * vn_ref[0, h:h + 1, :] + sum(pv[i] for i in sel)
        o_ref[0, h] = acc / l


def _moba_sample(q, k_new, v_new, scores, vt_pool, page_table, top):
    bd, n_pages = page_table.shape
    past = n_pages * PAGE_SIZE
    assert past % MOBA_BLOCK == 0

    def tok_spec():
        return pl.BlockSpec((1, H_ATT, HEAD_DIM), lambda b, pt, tp: (b, 0, 0))

    grid_spec = pltpu.PrefetchScalarGridSpec(
        num_scalar_prefetch=2,
        grid=(bd,),
        in_specs=[tok_spec(), tok_spec(), tok_spec(),
                  pl.BlockSpec((1, n_pages, H_ATT, PAGE_SIZE), lambda b, pt, tp: (b, 0, 0, 0)),
                  pl.BlockSpec(memory_space=pl.ANY)],
        out_specs=pl.BlockSpec((1, H_ATT, 1, HEAD_DIM), lambda b, pt, tp: (b, 0, 0, 0)),
        scratch_shapes=[pltpu.VMEM((2, H_ATT * N_SEL_PAGES, HEAD_DIM, PAGE_SIZE), F32),
                        pltpu.SemaphoreType.DMA((2,))],
    )
    return pl.pallas_call(
        functools.partial(_moba_sample_body, past=past, n_pages=n_pages),
        grid_spec=grid_spec,
        out_shape=jax.ShapeDtypeStruct((bd, H_ATT, 1, HEAD_DIM), F32),
        compiler_params=_cparams("arbitrary"),
    )(page_table.reshape(-1), top.reshape(-1), q, k_new, v_new, scores, vt_pool)


def _softplus(z):
    return jnp.maximum(z, 0.0) + jnp.log(1.0 + jnp.exp(-jnp.abs(z)))


def _sigmoid(z):
    return 1.0 / (1.0 + jnp.exp(-z))


def _rwkv_token_terms(m, w0, w_up, a0, a_up, g_up, k_k, k_a):
    rw = RWKV_WIDTH
    r, k, v = m[:, 0:rw], m[:, rw:2 * rw], m[:, 2 * rw:3 * rw]
    xw = m[:, 3 * rw:3 * rw + W_LORA]
    xa = m[:, 3 * rw + W_LORA:3 * rw + W_LORA + A_LORA]
    xg = m[:, 3 * rw + W_LORA + A_LORA:]
    w_log = -_softplus(-(w0 + _dot(jnp.tanh(xw), w_up, precision=HIGHEST))) - 0.5
    log_decay = -jnp.exp(w_log)
    a = _sigmoid(a0 + _dot(xa, a_up, precision=HIGHEST))
    g = _bdot(_sigmoid(xg), g_up)
    kk = k * k_k
    k2 = k * (1.0 + (a - 1.0) * k_a)
    return r, log_decay, k2, v, kk, a, g


def _head_norm(kk_h):
    return kk_h * lax.rsqrt(jnp.maximum(jnp.sum(kk_h * kk_h, axis=-1, keepdims=True), 1e-24))


def _rwkv_head_out(y, r_h, k_h, v_h, g_h, rk_h, gng_h, gnb_h):
    mu = jnp.mean(y, axis=-1, keepdims=True)
    d = y - mu
    var = jnp.mean(d * d, axis=-1, keepdims=True)
    yn = d * lax.rsqrt(var + GN_EPS) * gng_h + gnb_h
    bonus = jnp.sum(r_h * k_h * rk_h, axis=-1, keepdims=True) * v_h
    return (yn + bonus) * g_h


PAIR = 2 * RWKV_HEAD


def _pair_blockdiag(y):
    left = lax.broadcasted_iota(jnp.int32, y.shape, 1) < RWKV_HEAD
    zero = jnp.zeros_like(y)
    return jnp.concatenate([jnp.where(left, y, zero), jnp.where(left, zero, y)], axis=0)


def _pair_nn(x, y):
    return _dot(x.astype(BF16), _pair_blockdiag(y.astype(BF16)))


def _pair_nt(x, y):
    return _dot(x.astype(BF16), _pair_blockdiag(y.astype(BF16)), _NT)


def _pair_tn(x, y):
    full = _dot(x.astype(BF16), y.astype(BF16), _TN)
    left = lax.broadcasted_iota(jnp.int32, (RWKV_HEAD, PAIR), 1) < RWKV_HEAD
    return jnp.where(left, full[:RWKV_HEAD], full[RWKV_HEAD:])


def _pair_sum(x):
    left = lax.broadcasted_iota(jnp.int32, x.shape, 1) < RWKV_HEAD
    s_a = jnp.sum(jnp.where(left, x, 0.0), axis=-1, keepdims=True)
    s_b = jnp.sum(jnp.where(left, 0.0, x), axis=-1, keepdims=True)
    return jnp.where(left, s_a, s_b)


def _rwkv_pair_out(y, r_p, k_p, v_p, g_p, rk_p, gng_p, gnb_p):
    inv_n = 1.0 / RWKV_HEAD
    d = y - _pair_sum(y) * inv_n
    var = _pair_sum(d * d) * inv_n
    yn = d * lax.rsqrt(var + GN_EPS) * gng_p + gnb_p
    return (yn + _pair_sum(r_p * k_p * rk_p) * v_p) * g_p


def _rwkv_prompt_body(rw_ref, prev_ref, s0_ref, mu_ref, w0_ref, wup_ref, a0_ref, aup_ref, gup_ref,
                      kk_ref, ka_ref, rk_ref, gng_ref, gnb_ref, y_ref, sfin_ref, state_s, prev_s):
    c = pl.program_id(1)
    ch = RWKV_CHUNK
    tile = rw_ref.shape[1]
    n_ch = tile // ch
    n_pairs = H_RWKV // 2
    pairs = [slice(p * PAIR, (p + 1) * PAIR) for p in range(n_pairs)]

    @pl.when(c == 0)
    def _():
        for p in range(n_pairs):
            state_s[p] = jnp.concatenate([s0_ref[0, (2 * p) * RWKV_HEAD:(2 * p + 1) * RWKV_HEAD, :],
                                          s0_ref[0, (2 * p + 1) * RWKV_HEAD:(2 * p + 2) * RWKV_HEAD, :]], axis=1)
        prev_s[...] = prev_ref[0]

    rw = rw_ref[0]
    row = lax.broadcasted_iota(jnp.int32, rw.shape, 0)
    rw_prev = jnp.where(row == 0, prev_s[...], pltpu.roll(rw, 1, 0))
    prev_s[...] = rw[tile - 1:tile, :]
    m = rw + (rw_prev - rw) * mu_ref[...]
    r, log_decay, k2, v, kk, a, g = _rwkv_token_terms(
        m, w0_ref[...], wup_ref[...], a0_ref[...], aup_ref[...], gup_ref[...], kk_ref[...], ka_ref[...])

    ti = lax.broadcasted_iota(jnp.int32, (tile, tile), 0)
    si = lax.broadcasted_iota(jnp.int32, (tile, tile), 1)
    in_chunk_lower = (si <= ti) & (si // ch == ti // ch)
    cs = _dot(in_chunk_lower.astype(F32), log_decay, precision=HIGHEST)
    cs_end = cs[ch - 1:ch, :]
    rows = lax.broadcasted_iota(jnp.int32, cs.shape, 0)
    for j in range(1, n_ch):
        cs_end = jnp.where(rows >= j * ch, cs[(j + 1) * ch - 1:(j + 1) * ch, :], cs_end)
    gam = jnp.exp(cs)
    gam_prev = jnp.exp(cs - log_decay)
    gam_inv = jnp.exp(-cs)
    gam_tail = jnp.exp(cs_end - cs)

    t_p = lax.broadcasted_iota(jnp.int32, (ch, PAIR), 0)
    s_p = lax.broadcasted_iota(jnp.int32, (ch, PAIR), 1) % RWKV_HEAD
    lower = s_p <= t_p
    strict = s_p < t_p
    eye = (s_p == t_p).astype(F32)

    units = [(slice(j * ch, (j + 1) * ch), ps) for j in range(n_ch) for ps in pairs]
    kk_n = [kk[ts, ps] for ts, ps in units]
    kk_n = [x * lax.rsqrt(jnp.maximum(_pair_sum(x * x), 1e-24)) for x in kk_n]
    a_t = [-n * gam_prev[ts, ps] for n, (ts, ps) in zip(kk_n, units)]
    b_h = [n * a[ts, ps] for n, (ts, ps) in zip(kk_n, units)]
    b_t = [x * gam_inv[ts, ps] for x, (ts, ps) in zip(b_h, units)]
    k_t = [k2[ts, ps] * gam_inv[ts, ps] for ts, ps in units]
    r_t = [r[ts, ps] * gam[ts, ps] for ts, ps in units]
    v_h = [v[ts, ps] for ts, ps in units]
    l_ab = [jnp.where(strict, _pair_nt(x, z), 0.0) for x, z in zip(a_t, b_t)]
    l_ak = [jnp.where(strict, _pair_nt(x, z), 0.0) for x, z in zip(a_t, k_t)]
    m_rb = [jnp.where(lower, _pair_nt(x, z), 0.0) for x, z in zip(r_t, b_t)]
    m_rk = [jnp.where(lower, _pair_nt(x, z), 0.0) for x, z in zip(r_t, k_t)]
    inv = [eye + x for x in l_ab]
    pw = list(l_ab)
    span = 2
    while span < ch:
        pw = [_pair_nn(x, x) for x in pw]
        inv = [i + _pair_nn(i, x) for i, x in zip(inv, pw)]
        span *= 2
    lv = [_pair_nn(l, vv) for l, vv in zip(l_ak, v_h)]
    t_a = [_pair_nn(i, x) for i, x in zip(inv, a_t)]
    t_l = [_pair_nn(i, x) for i, x in zip(inv, lv)]
    b_e = [x * gam_tail[u_] for x, u_ in zip(b_h, units)]
    k_e = [k2[u_] * gam_tail[u_] for u_ in units]
    r_a = [x + _pair_nn(mb, ta) for x, mb, ta in zip(r_t, m_rb, t_a)]
    y_0 = [_pair_nn(mb, tl) + _pair_nn(mk, vv) for mb, tl, mk, vv in zip(m_rb, t_l, m_rk, v_h)]
    m_s = [_pair_tn(ta, be) for ta, be in zip(t_a, b_e)]
    c_s = [_pair_tn(tl, be) + _pair_tn(vv, ke) for tl, be, vv, ke in zip(t_l, b_e, v_h, k_e)]
    s_cur = [state_s[p] for p in range(n_pairs)]
    for j in range(n_ch):
        sel = range(j * n_pairs, (j + 1) * n_pairs)
        y = [_pair_nt(r_a[i], s) + y_0[i] for i, s in zip(sel, s_cur)]
        s_cur = [s * gam[(j + 1) * ch - 1:(j + 1) * ch, units[i][1]] + _pair_nn(s, m_s[i]) + c_s[i]
                 for i, s in zip(sel, s_cur)]
        for i, yy in zip(sel, y):
            ts, ps = units[i]
            y_ref[0, ts, ps] = _rwkv_pair_out(yy, r[ts, ps], k2[ts, ps], v_h[i], g[ts, ps],
                                              rk_ref[:, ps], gng_ref[:, ps], gnb_ref[:, ps])
    for p in range(n_pairs):
        state_s[p] = s_cur[p]

    @pl.when(c == pl.num_programs(1) - 1)
    def _():
        for p in range(n_pairs):
            sfin_ref[0, (2 * p) * RWKV_HEAD:(2 * p + 1) * RWKV_HEAD, :] = state_s[p][:, :RWKV_HEAD]
            sfin_ref[0, (2 * p + 1) * RWKV_HEAD:(2 * p + 2) * RWKV_HEAD, :] = state_s[p][:, RWKV_HEAD:]


def _rwkv_prompt(rw, prev_row, s0, p, tile):
    bsz, t, _ = rw.shape
    assert t % tile == 0 and tile % RWKV_CHUNK == 0 and H_RWKV % 2 == 0
    hn = H_RWKV * RWKV_HEAD
    vec = lambda n: _full((1, n))
    return pl.pallas_call(
        _rwkv_prompt_body,
        grid=(bsz, t // tile),
        in_specs=[pl.BlockSpec((1, tile, RWKV_COLS), lambda i, j: (i, j, 0)),
                  pl.BlockSpec((1, 1, RWKV_COLS), lambda i, j: (i, 0, 0)),
                  pl.BlockSpec((1, hn, RWKV_HEAD), lambda i, j: (i, 0, 0)),
                  vec(RWKV_COLS), vec(RWKV_WIDTH), _full((W_LORA, RWKV_WIDTH)), vec(RWKV_WIDTH),
                  _full((A_LORA, RWKV_WIDTH)), _full((G_LORA, RWKV_WIDTH)),
                  vec(RWKV_WIDTH), vec(RWKV_WIDTH), vec(RWKV_WIDTH), vec(RWKV_WIDTH), vec(RWKV_WIDTH)],
        out_specs=[pl.BlockSpec((1, tile, RWKV_WIDTH), lambda i, j: (i, j, 0)),
                   pl.BlockSpec((1, hn, RWKV_HEAD), lambda i, j: (i, 0, 0))],
        out_shape=[jax.ShapeDtypeStruct((bsz, t, RWKV_WIDTH), F32),
                   jax.ShapeDtypeStruct((bsz, hn, RWKV_HEAD), F32)],
        scratch_shapes=[pltpu.VMEM((H_RWKV // 2, RWKV_HEAD, PAIR), F32), pltpu.VMEM((1, RWKV_COLS), F32)],
        compiler_params=_cparams("parallel", "arbitrary"),
    )(rw, prev_row, s0, p["mu_shift"], p["w0"], p["w_up"], p["a0"], p["a_up"], p["g_up"],
      p["k_k"], p["k_a"], p["r_k"], p["gn_g"], p["gn_b"])


def _rows8(*rows):
    ri = lax.broadcasted_iota(jnp.int32, (SUBLANES, rows[0].shape[1]), 0)
    out = jnp.zeros((SUBLANES, rows[0].shape[1]), F32)
    for i, x in enumerate(rows):
        out = jnp.where(ri == i, x, out)
    return out


STEP_ROWS = SUBLANES


def _rwkv_step_body(rw_ref, prev_ref, s0_ref, mu_ref, w0_ref, wup_ref, a0_ref, aup_ref, gup_ref,
                    kk_ref, ka_ref, rk_ref, gng_ref, gnb_ref, y_ref, sfin_ref):
    rw = rw_ref[...]
    m = rw + (prev_ref[...] - rw) * mu_ref[...]
    r, log_decay, k2, v, kk, a, g = _rwkv_token_terms(
        m, w0_ref[...], wup_ref[...], a0_ref[...], aup_ref[...], gup_ref[...], kk_ref[...], ka_ref[...])
    decay = jnp.exp(log_decay)
    heads = [slice(h * RWKV_HEAD, (h + 1) * RWKV_HEAD) for h in range(H_RWKV)]
    kkn = [_head_norm(kk[:, hs]) for hs in heads]
    b_h = [n * a[:, hs] for n, hs in zip(kkn, heads)]
    units = [(i, h) for i in range(STEP_ROWS) for h in range(H_RWKV)]
    row = lambda x, i: x[i:i + 1, :]
    s_0 = [s0_ref[i, heads[h], :] for i, h in units]
    sa = [_dot(_rows8(-row(kkn[h], i)), s, _NT, HIGHEST)[0:1] for (i, h), s in zip(units, s_0)]
    s_1 = [s * row(decay[:, heads[h]], i)
           + _dot(_rows8(x, row(v[:, heads[h]], i)), _rows8(row(b_h[h], i), row(k2[:, heads[h]], i)), _TN, HIGHEST)
           for (i, h), s, x in zip(units, s_0, sa)]
    y = [_dot(_rows8(row(r[:, heads[h]], i)), s, _NT, HIGHEST)[0:1] for (i, h), s in zip(units, s_1)]
    for (i, h), s in zip(units, s_1):
        sfin_ref[i, heads[h], :] = s
    seq = lax.broadcasted_iota(jnp.int32, (STEP_ROWS, RWKV_HEAD), 0)
    for h, hs in enumerate(heads):
        y_h = jnp.zeros((STEP_ROWS, RWKV_HEAD), F32)
        for i in range(STEP_ROWS):
            y_h = jnp.where(seq == i, y[i * H_RWKV + h], y_h)
        y_ref[:, hs] = _rwkv_head_out(y_h, r[:, hs], k2[:, hs], v[:, hs], g[:, hs],
                                      rk_ref[:, hs], gng_ref[:, hs], gnb_ref[:, hs])


def _rwkv_step(rw, prev_row, s0, p):
    bd = rw.shape[0]
    assert bd % STEP_ROWS == 0
    hn = H_RWKV * RWKV_HEAD
    vec = lambda n: _full((1, n))
    rows = lambda n: pl.BlockSpec((STEP_ROWS, n), lambda i: (i, 0))
    state = pl.BlockSpec((STEP_ROWS, hn, RWKV_HEAD), lambda i: (i, 0, 0))
    return pl.pallas_call(
        _rwkv_step_body,
        grid=(bd // STEP_ROWS,),
        in_specs=[rows(RWKV_COLS), rows(RWKV_COLS), state,
                  vec(RWKV_COLS), vec(RWKV_WIDTH), _full((W_LORA, RWKV_WIDTH)), vec(RWKV_WIDTH),
                  _full((A_LORA, RWKV_WIDTH)), _full((G_LORA, RWKV_WIDTH)),
                  vec(RWKV_WIDTH), vec(RWKV_WIDTH), vec(RWKV_WIDTH), vec(RWKV_WIDTH), vec(RWKV_WIDTH)],
        out_specs=[rows(RWKV_WIDTH), state],
        out_shape=[jax.ShapeDtypeStruct((bd, RWKV_WIDTH), F32),
                   jax.ShapeDtypeStruct((bd, hn, RWKV_HEAD), F32)],
        compiler_params=_cparams("parallel"),
    )(rw, prev_row, s0, p["mu_shift"], p["w0"], p["w_up"], p["a0"], p["a_up"], p["g_up"],
      p["k_k"], p["k_a"], p["r_k"], p["gn_g"], p["gn_b"])


def _row(x):
    return x.reshape(1, -1)


def kernel(x_prompt, x_sample, mem_prompt, cache_k_pool, cache_v_pool, page_table, cache_mem_k, cache_mem_v, state_wkv, state_shift, w_in, mu_shift, w0, w_up, a0, a_up, g_up, k_k, k_a, r_k, gn_g, gn_b, w_out, ln1_g, ln1_b, w_q_mem, w_kv_mem, w_o_mem, ln2_g, ln2_b, w_ff1, w_ff2, ln3_g, ln3_b):
    assert w_in.shape[0] == DEPTH == 1
    bsz, t, _ = x_prompt.shape
    bd = x_sample.shape[0]
    hn = H_RWKV * RWKV_HEAD

    w_in_b = w_in[0].astype(BF16)
    w_out_b = w_out[0].astype(BF16)
    wq_b = w_q_mem[0].astype(BF16)
    wkv_b = w_kv_mem[0].astype(BF16)
    wo_b = w_o_mem[0].astype(BF16)
    w1_b = w_ff1[0].astype(BF16)
    w2_b = w_ff2[0].astype(BF16)
    ln1 = (_row(ln1_g[0]), _row(ln1_b[0]))
    ln2 = (_row(ln2_g[0]), _row(ln2_b[0]))
    ln3 = (_row(ln3_g[0]), _row(ln3_b[0]))
    rwkv_p = dict(mu_shift=_row(mu_shift[0]), w0=_row(w0[0]), w_up=w_up[0], a0=_row(a0[0]), a_up=a_up[0],
                  g_up=g_up[0], k_k=_row(k_k[0]), k_a=_row(k_a[0]), r_k=_row(r_k[0]),
                  gn_g=_row(gn_g[0]), gn_b=_row(gn_b[0]))

    qt_p, kt_p, vt_p, kb_p, rw_p3, kmean_p = _proj_prompt(x_prompt, w_in_b, tm=512)
    y_att_p = _moba_prompt(qt_p, kb_p, vt_p, kmean_p)
    y_rw_p, s_p = _rwkv_prompt(rw_p3, jnp.zeros((bsz, 1, RWKV_COLS), F32), jnp.zeros((bsz, hn, RWKV_HEAD), F32), rwkv_p,
                               tile=RWKV_TILE)
    mk_p, mv_p, mk_p4, mv_p4 = _mem_kv(mem_prompt, wkv_b)
    x2_p = _mix_memx_prompt(x_prompt, y_att_p, y_rw_p, mk_p, mv_p, w_out_b[:ATT_WIDTH], w_out_b[ATT_WIDTH:],
                            wq_b, wo_b, ln1, ln2, tm=512)

    kt_pool = cache_k_pool[0].transpose(0, 2, 3, 1)
    vt_pool = cache_v_pool[0].transpose(0, 2, 3, 1)
    xs = x_sample.reshape(bd, D_MODEL)
    q_s = _mm_f32(xs, w_in[0], ATT_WIDTH)
    k_s, v_s, rw_s = _mm_multi(xs, w_in_b, (ATT_WIDTH, ATT_WIDTH, RWKV_COLS), tm=bd, first_col=ATT_WIDTH)
    q_s3, k_s3, v_s3 = (z.reshape(bd, H_ATT, HEAD_DIM) for z in (q_s, k_s, v_s))
    y_p, scores_s, top = _mlp_scores(x2_p.reshape(bsz * t, D_MODEL), w1_b, w2_b, *ln3, q_s3, kt_pool, page_table,
                                     tm=MLP_TM)
    y_att_s = _moba_sample(q_s3, k_s3, v_s3, scores_s, vt_pool, page_table, top[:, :, :MOBA_TOPK])
    y_rw_s, s_s = _rwkv_step(rw_s, state_shift[0], state_wkv[0].reshape(bd, hn, RWKV_HEAD), rwkv_p)
    x1_s = _mm_res_ln([y_att_s.reshape(bd, ATT_WIDTH), y_rw_s],
                      [w_out_b[:ATT_WIDTH], w_out_b[ATT_WIDTH:]], xs, *ln1, tm=bd)
    (qm_s,) = _mm_multi(x1_s, wq_b, (D_MODEL,), tm=bd)
    att_s = _memx_sample(qm_s.reshape(bd, 1, D_MODEL), cache_mem_k[0], cache_mem_v[0])
    x2_s = _mm_res_ln([att_s.reshape(bd, D_MODEL)], [wo_b], x1_s, *ln2, tm=bd)
    y_s = _mlp(x2_s, w1_b, w2_b, *ln3, tm=bd)

    return (y_p.reshape(bsz, t, D_MODEL), y_s.reshape(bd, 1, D_MODEL),
            kt_p.reshape(bsz, H_ATT, HEAD_DIM, t).transpose(0, 3, 1, 2)[None],
            vt_p.reshape(bsz, H_ATT, HEAD_DIM, t).transpose(0, 3, 1, 2)[None],
            k_s.reshape(1, bd, 1, H_ATT, HEAD_DIM), v_s.reshape(1, bd, 1, H_ATT, HEAD_DIM),
            mk_p4[None], mv_p4[None],
            s_p.reshape(1, bsz, H_RWKV, RWKV_HEAD, RWKV_HEAD), rw_p3[:, -1][None],
            s_s.reshape(1, bd, H_RWKV, RWKV_HEAD, RWKV_HEAD), rw_s[None])
```
